```python
import math
import jax, jax.numpy as jnp
from jax import lax
import numpy as np

D_MODEL = 2048
BATCH = 4
SEQ = 4096
DEPTH = 4

HEAD_DIM = 128
H_FOX = 6
H_DSA = 6
H_MEM = 4
D_FOX = H_FOX * HEAD_DIM
D_DSA = H_DSA * HEAD_DIM
D_MEMG = H_MEM * HEAD_DIM
D_MIX = D_FOX + D_DSA + D_MEMG
IDX_HEADS = 16
IDX_DIM = 64
INDEX_TOPK = 256
N_MEM = 256
Q_BLOCK = 128
ROPE_THETA = 10000.0
LN_EPS = 1e-5
ALPHA = (2 * DEPTH) ** 0.25
BETA = (8 * DEPTH) ** -0.25
NEG = -1e30

SPLITS = (
    D_FOX, D_FOX, D_FOX, D_FOX, H_FOX,
    D_DSA, D_DSA, D_DSA, D_DSA,
    IDX_HEADS * IDX_DIM, IDX_DIM, IDX_HEADS,
    D_MEMG, D_MEMG,
)
N_IN = sum(SPLITS)

kernel_name = "hybrid_fox_dsa_memory_deepnorm"


def rope(x, pos):
    d = x.shape[-1]
    half = d // 2
    inv_freq = ROPE_THETA ** (-jnp.arange(half, dtype=jnp.float32) / half)
    ang = pos.astype(jnp.float32)[:, None] * inv_freq[None, :]
    cos = jnp.cos(ang)[None, :, None, :].astype(x.dtype)
    sin = jnp.sin(ang)[None, :, None, :].astype(x.dtype)
    x1, x2 = x[..., :half], x[..., half:]
    return jnp.concatenate([x1 * cos - x2 * sin, x2 * cos + x1 * sin], axis=-1)


def to_blocks(a):
    b, s = a.shape[0], a.shape[1]
    return jnp.moveaxis(a.reshape(b, s // Q_BLOCK, Q_BLOCK, *a.shape[2:]), 1, 0)


def from_blocks(a):
    a = jnp.moveaxis(a, 0, 1)
    return a.reshape(a.shape[0], a.shape[1] * a.shape[2], *a.shape[3:])


def fox_attention(q, k, v, cum_logf):
    b, s, h, d = q.shape
    scale = d ** -0.5
    key_pos = jnp.arange(s)
    c_keys = jnp.transpose(cum_logf, (0, 2, 1))
    q_pos = key_pos.reshape(s // Q_BLOCK, Q_BLOCK)

    def one_block(args):
        qb, cb, tb = args
        sc = jnp.einsum('bqhd,bkhd->bhqk', qb, k).astype(jnp.float32) * scale
        sc = sc + (jnp.transpose(cb, (0, 2, 1))[:, :, :, None] - c_keys[:, :, None, :])
        causal = tb[:, None] >= key_pos[None, :]
        sc = jnp.where(causal[None, None], sc, NEG)
        p = jax.nn.softmax(sc, axis=-1).astype(v.dtype)
        return jnp.einsum('bhqk,bkhd->bqhd', p, v)

    out = lax.map(one_block, (to_blocks(q), to_blocks(cum_logf), q_pos))
    return from_blocks(out)


def dsa_attention(q, k, v, q_idx, k_idx, w_idx):
    b, s, h, d = q.shape
    topk = min(INDEX_TOPK, s // 4)
    scale = d ** -0.5
    idx_scale = IDX_DIM ** -0.5
    key_pos = jnp.arange(s)
    q_pos = key_pos.reshape(s // Q_BLOCK, Q_BLOCK)

    def one_block(args):
        qb, qib, wb, tb = args
        rel = jax.nn.relu(jnp.einsum('bqhi,bki->bqhk', qib, k_idx).astype(jnp.float32) * idx_scale)
        score = jnp.einsum('bqhk,bqh->bqk', rel, wb.astype(jnp.float32))
        causal = tb[:, None] >= key_pos[None, :]
        score = jnp.where(causal[None], score, NEG)
        _, sel = lax.top_k(score, topk)
        valid = sel <= tb[None, :, None]
        kg = jax.vmap(lambda kk, ii: kk[ii])(k, sel)
        vg = jax.vmap(lambda vv, ii: vv[ii])(v, sel)
        sc = jnp.einsum('bqhd,bqkhd->bhqk', qb, kg).astype(jnp.float32) * scale
        sc = jnp.where(jnp.transpose(valid, (0, 1, 2))[:, None], sc, NEG)
        p = jax.nn.softmax(sc, axis=-1).astype(vg.dtype)
        return jnp.einsum('bhqk,bqkhd->bqhd', p, vg)

    out = lax.map(one_block, (to_blocks(q), to_blocks(q_idx), to_blocks(w_idx), q_pos))
    return from_blocks(out)


def memory_attention(q, k_mem, v_mem):
    scale = q.shape[-1] ** -0.5
    sc = jnp.einsum('bshd,bmhd->bhsm', q, k_mem).astype(jnp.float32) * scale
    p = jax.nn.softmax(sc, axis=-1).astype(v_mem.dtype)
    return jnp.einsum('bhsm,bmhd->bshd', p, v_mem)


def layer_norm(z, g, b):
    zf = z.astype(jnp.float32)
    mu = jnp.mean(zf, axis=-1, keepdims=True)
    var = jnp.mean(jnp.square(zf - mu), axis=-1, keepdims=True)
    out = (zf - mu) * lax.rsqrt(var + LN_EPS) * g.astype(jnp.float32) + b.astype(jnp.float32)
    return out.astype(z.dtype)


def hybrid_layer(x, mem, w_in, b_forget, w_mem_kv, w_out, ln_gain, ln_bias):
    b, s, _ = x.shape
    pos = jnp.arange(s)
    h = jnp.einsum('bsd,dn->bsn', x, w_in)
    offsets = np.cumsum(SPLITS)[:-1].tolist()
    (fq, fk, fv, fg, f_logit,
     dq, dk, dv, dg,
     iq, ik, iw,
     mq, mg) = jnp.split(h, offsets, axis=-1)

    heads = lambda t, nh: t.reshape(b, s, nh, HEAD_DIM)

    log_f = jax.nn.log_sigmoid(f_logit.astype(jnp.float32) + b_forget.astype(jnp.float32))
    cum_logf = jnp.cumsum(log_f, axis=1)
    y_fox = fox_attention(heads(fq, H_FOX), heads(fk, H_FOX), heads(fv, H_FOX), cum_logf)
    y_fox = y_fox.reshape(b, s, D_FOX) * jax.nn.silu(fg)

    q_idx = rope(iq.reshape(b, s, IDX_HEADS, IDX_DIM), pos)
    k_idx = rope(ik.reshape(b, s, 1, IDX_DIM), pos)[:, :, 0, :]
    w_idx = iw * (IDX_HEADS ** -0.5)
    y_dsa = dsa_attention(rope(heads(dq, H_DSA), pos), rope(heads(dk, H_DSA), pos), heads(dv, H_DSA),
                          q_idx, k_idx, w_idx)
    y_dsa = y_dsa.reshape(b, s, D_DSA) * jax.nn.silu(dg)

    kv = jnp.einsum('bmd,dn->bmn', mem, w_mem_kv)
    k_mem, v_mem = jnp.split(kv, 2, axis=-1)
    m = mem.shape[1]
    y_mem = memory_attention(heads(mq, H_MEM), k_mem.reshape(b, m, H_MEM, HEAD_DIM),
                             v_mem.reshape(b, m, H_MEM, HEAD_DIM))
    y_mem = y_mem.reshape(b, s, D_MEMG) * jax.nn.silu(mg)

    y = jnp.concatenate([y_fox, y_dsa, y_mem], axis=-1)
    y = jnp.einsum('bsn,nd->bsd', y, w_out)
    return layer_norm(ALPHA * x + y, ln_gain, ln_bias)


def setup_inputs(seed: int = 0) -> dict:
    key = jax.random.key(seed)
    ks = jax.random.split(key, 8)
    x = jax.random.normal(ks[0], (BATCH, SEQ, D_MODEL), jnp.float32)
    mem = jax.random.normal(ks[1], (BATCH, N_MEM, D_MODEL), jnp.float32)
    w_in = jax.random.normal(ks[2], (DEPTH, D_MODEL, N_IN), jnp.float32) * D_MODEL ** -0.5
    b_forget = 3.0 + 0.5 * jax.random.normal(ks[3], (DEPTH, H_FOX), jnp.float32)
    w_mem_kv = jax.random.normal(ks[4], (DEPTH, D_MODEL, 2 * D_MEMG), jnp.float32) * D_MODEL ** -0.5
    w_out = jax.random.normal(ks[5], (DEPTH, D_MIX, D_MODEL), jnp.float32) * (D_MIX ** -0.5) * BETA
    ln_gain = 1.0 + 0.05 * jax.random.normal(ks[6], (DEPTH, D_MODEL), jnp.float32)
    ln_bias = 0.02 * jax.random.normal(ks[7], (DEPTH, D_MODEL), jnp.float32)
    return {"x": x, "mem": mem, "w_in": w_in, "b_forget": b_forget, "w_mem_kv": w_mem_kv,
            "w_out": w_out, "ln_gain": ln_gain, "ln_bias": ln_bias}


def reference(x, mem, w_in, b_forget, w_mem_kv, w_out, ln_gain, ln_bias):
    h = x
    for l in range(DEPTH):
        h = hybrid_layer(h, mem, w_in[l], b_forget[l], w_mem_kv[l], w_out[l], ln_gain[l], ln_bias[l])
    return h
```

```python
import functools

import numpy as np
import jax
import jax.numpy as jnp
from jax import lax
from jax.experimental import pallas as pl
from jax.experimental.pallas import tpu as pltpu

HEAD_DIM = 128
H_FOX = 6
H_DSA = 6
H_MEM = 4
D_FOX = H_FOX * HEAD_DIM
D_DSA = H_DSA * HEAD_DIM
D_MEMG = H_MEM * HEAD_DIM
IDX_HEADS = 16
IDX_DIM = 64
INDEX_TOPK = 256
ROPE_THETA = 10000.0
LN_EPS = 1e-5
NEG = -1e30

SEQ_TILE = 256
VMEM_LIMIT_BYTES = 56 * 1024 * 1024

F32 = jnp.float32
BF16 = jnp.bfloat16


def _sortable_key(bits):
    return bits ^ ((bits >> 31) & 0x7FFFFFFF)


_NEG_BITS = int(np.array(NEG, np.float32).view(np.int32))
KEY_NEG = int(np.int32(_NEG_BITS) ^ np.int32(0x7FFFFFFF))
INT_MIN = -(2 ** 31)


def _cparams(sem, vmem=None):
    return pltpu.CompilerParams(dimension_semantics=sem, vmem_limit_bytes=vmem)


def _proj_plain_kernel(x_ref, w_ref, o_ref):
    o_ref[...] = jnp.dot(x_ref[...], w_ref[...], preferred_element_type=F32).astype(o_ref.dtype)


def _proj_plain(xb, w, tm, tn, name):
    m, d = xb.shape
    n = w.shape[1]
    return pl.pallas_call(
        _proj_plain_kernel,
        out_shape=jax.ShapeDtypeStruct((m, n), BF16),
        grid=(n // tn, m // tm),
        in_specs=[pl.BlockSpec((tm, d), lambda j, i: (i, 0)),
                  pl.BlockSpec((d, tn), lambda j, i: (0, j))],
        out_specs=pl.BlockSpec((tm, tn), lambda j, i: (i, j)),
        compiler_params=_cparams(("parallel", "parallel"), VMEM_LIMIT_BYTES),
        name=name,
    )(xb, w)


def _proj_rope_kernel(x_ref, w_ref, cos_ref, sin_ref, o_ref, *, heads):
    y = jnp.dot(x_ref[...], w_ref[...], preferred_element_type=F32)
    c = cos_ref[...]
    s = sin_ref[...]
    for h in range(heads):
        yh = y[:, h * HEAD_DIM:(h + 1) * HEAD_DIM]
        rot = pltpu.roll(yh, HEAD_DIM // 2, axis=1)
        o_ref[:, h * HEAD_DIM:(h + 1) * HEAD_DIM] = (yh * c + rot * s).astype(o_ref.dtype)


def _proj_rope(xb, w, cos128, sin128, seq, tm, name):
    m, d = xb.shape
    n = w.shape[1]
    nseq = seq // tm
    return pl.pallas_call(
        functools.partial(_proj_rope_kernel, heads=n // HEAD_DIM),
        out_shape=jax.ShapeDtypeStruct((m, n), BF16),
        grid=(m // tm,),
        in_specs=[pl.BlockSpec((tm, d), lambda i: (i, 0)),
                  pl.BlockSpec((d, n), lambda i: (0, 0)),
                  pl.BlockSpec((tm, HEAD_DIM), lambda i: (i % nseq, 0)),
                  pl.BlockSpec((tm, HEAD_DIM), lambda i: (i % nseq, 0))],
        out_specs=pl.BlockSpec((tm, n), lambda i: (i, 0)),
        compiler_params=_cparams(("parallel",), VMEM_LIMIT_BYTES),
        name=name,
    )(xb, w, cos128, sin128)


def _proj_t_kernel(x_ref, wt_ref, cos64_ref, sin64_ref, cos32_ref, sin32_ref,
                   dq_ref, dv_ref, iq_ref, *, scale):
    yt = lax.dot_general(wt_ref[...], x_ref[...], (((1,), (1,)), ((), ())), preferred_element_type=F32)
    c64 = cos64_ref[...]
    s64 = sin64_ref[...]
    half = HEAD_DIM // 2
    for h in range(H_DSA):
        x1 = yt[h * HEAD_DIM:h * HEAD_DIM + half]
        x2 = yt[h * HEAD_DIM + half:(h + 1) * HEAD_DIM]
        dq_ref[0, 0, h * HEAD_DIM:h * HEAD_DIM + half, :] = ((x1 * c64 - x2 * s64) * scale).astype(dq_ref.dtype)
        dq_ref[0, 0, h * HEAD_DIM + half:(h + 1) * HEAD_DIM, :] = ((x2 * c64 + x1 * s64) * scale).astype(dq_ref.dtype)
    dv_ref[0, 0] = yt[D_DSA:2 * D_DSA].astype(dv_ref.dtype)
    c32 = cos32_ref[...]
    s32 = sin32_ref[...]
    base = 2 * D_DSA
    ih = IDX_DIM // 2
    for h in range(IDX_HEADS):
        x1 = yt[base + h * IDX_DIM:base + h * IDX_DIM + ih]
        x2 = yt[base + h * IDX_DIM + ih:base + (h + 1) * IDX_DIM]
        iq_ref[0, 0, h * IDX_DIM:h * IDX_DIM + ih, :] = (x1 * c32 - x2 * s32).astype(iq_ref.dtype)
        iq_ref[0, 0, h * IDX_DIM + ih:(h + 1) * IDX_DIM, :] = (x2 * c32 + x1 * s32).astype(iq_ref.dtype)


def _proj_t(xb, wt, cos64t, sin64t, cos32t, sin32t, batch, seq, scale):
    m, d = xb.shape
    ts = SEQ_TILE
    nb = seq // ts
    rows = wt.shape[0]
    out_shapes = (jax.ShapeDtypeStruct((batch, nb, D_DSA, ts), BF16),
                  jax.ShapeDtypeStruct((batch, nb, D_DSA, ts), BF16),
                  jax.ShapeDtypeStruct((batch, nb, IDX_HEADS * IDX_DIM, ts), BF16))
    blk = lambda r: pl.BlockSpec((1, 1, r, ts), lambda i: (i // nb, i % nb, 0, 0))
    return pl.pallas_call(
        functools.partial(_proj_t_kernel, scale=scale),
        out_shape=out_shapes,
        grid=(m // ts,),
        in_specs=[pl.BlockSpec((ts, d), lambda i: (i, 0)),
                  pl.BlockSpec((rows, d), lambda i: (0, 0)),
                  pl.BlockSpec((HEAD_DIM // 2, ts), lambda i: (0, i % nb)),
                  pl.BlockSpec((HEAD_DIM // 2, ts), lambda i: (0, i % nb)),
                  pl.BlockSpec((IDX_DIM // 2, ts), lambda i: (0, i % nb)),
                  pl.BlockSpec((IDX_DIM // 2, ts), lambda i: (0, i % nb))],
        out_specs=(blk(D_DSA), blk(D_DSA), blk(IDX_HEADS * IDX_DIM)),
        compiler_params=_cparams(("parallel",), VMEM_LIMIT_BYTES),
        name="proj_transposed",
    )(xb, wt, cos64t, sin64t, cos32t, sin32t)


def _prep_kernel(x_ref, w_ref, bf_ref, cos_ref, sin_ref, tri_ref, ik_ref, iwt_ref, ct_ref, carry_ref, *, w_scale):
    j = pl.program_id(1)
    raw = jnp.dot(x_ref[...], w_ref[...], preferred_element_type=F32)
    ts = raw.shape[0]

    lane = lax.broadcasted_iota(jnp.int32, raw.shape, 1)
    q = IDX_DIM // 2
    rot = jnp.where(lane < q, -pltpu.roll(raw, 128 - q, axis=1), pltpu.roll(raw, q, axis=1))
    ik = raw * cos_ref[...] + rot * sin_ref[...]
    ik_ref[0] = ik[:, :IDX_DIM].astype(ik_ref.dtype)

    rawt = raw.T
    iwt_ref[0, 0] = rawt[IDX_DIM:IDX_DIM + IDX_HEADS] * w_scale

    z = raw + bf_ref[...]
    logf = jnp.minimum(z, 0.0) - jnp.log(1.0 + jnp.exp(-jnp.abs(z)))
    tri = tri_ref[...]
    p1 = logf.astype(BF16)
    r1 = logf - p1.astype(F32)
    p2 = r1.astype(BF16)
    p3 = (r1 - p2.astype(F32)).astype(BF16)
    cs = (jnp.dot(tri, p1, preferred_element_type=F32)
          + jnp.dot(tri, p2, preferred_element_type=F32)
          + jnp.dot(tri, p3, preferred_element_type=F32))

    @pl.when(j == 0)
    def _():
        carry_ref[...] = jnp.zeros_like(carry_ref)

    cs = cs + carry_ref[...]
    carry_ref[...] = cs[ts - 1:ts, :]
    off = IDX_DIM + IDX_HEADS
    ct_ref[0, 0] = cs.T[off:off + 8]


def _prep(xb, w_small, bf_row, cos_ik, sin_ik, tri, batch, seq, w_scale):
    m, d = xb.shape
    ts = SEQ_TILE
    nb = seq // ts
    out_shapes = (jax.ShapeDtypeStruct((batch, seq, IDX_DIM), BF16),
                  jax.ShapeDtypeStruct((batch, nb, IDX_HEADS, ts), F32),
                  jax.ShapeDtypeStruct((batch, nb, 8, ts), F32))
    return pl.pallas_call(
        functools.partial(_prep_kernel, w_scale=w_scale),
        out_shape=out_shapes,
        grid=(batch, nb),
        in_specs=[pl.BlockSpec((ts, d), lambda b, j: (b * nb + j, 0)),
                  pl.BlockSpec((d, 128), lambda b, j: (0, 0)),
                  pl.BlockSpec((1, 128), lambda b, j: (0, 0)),
                  pl.BlockSpec((ts, 128), lambda b, j: (j, 0)),
                  pl.BlockSpec((ts, 128), lambda b, j: (j, 0)),
                  pl.BlockSpec((ts, ts), lambda b, j: (0, 0))],
        out_specs=(pl.BlockSpec((1, ts, IDX_DIM), lambda b, j: (b, j, 0)),
                   pl.BlockSpec((1, 1, IDX_HEADS, ts), lambda b, j: (b, j, 0, 0)),
                   pl.BlockSpec((1, 1, 8, ts), lambda b, j: (b, j, 0, 0))),
        scratch_shapes=[pltpu.VMEM((1, 128), F32)],
        compiler_params=_cparams(("parallel", "arbitrary")),
        name="prep_small",
    )(xb, w_small, bf_row, cos_ik, sin_ik, tri)


def _silu(g):
    return g / (1.0 + jnp.exp(-g))


def _fox_kernel(q_ref, k_ref, v_ref, g_ref, ct_ref, o_ref, *, scale):
    qi = pl.program_id(1)
    tq = q_ref.shape[0]
    tk = tq
    row = lax.broadcasted_iota(jnp.int32, (tq, tk), 0)
    col = lax.broadcasted_iota(jnp.int32, (tq, tk), 1)
    for h in range(H_FOX):
        hs = slice(h * HEAD_DIM, (h + 1) * HEAD_DIM)
        q = (q_ref[:, hs].astype(F32) * scale).astype(BF16)
        c0 = ct_ref[0, qi, h:h + 1, 0:1]

        def step(kb, carry, diagonal):
            m, l, acc = carry
            start = pl.multiple_of(kb * tk, tk)
            k = k_ref[pl.ds(start, tk), hs]
            v = v_ref[pl.ds(start, tk), hs]
            s = lax.dot_general(q, k, (((1,), (1,)), ((), ())), preferred_element_type=F32)
            s = s + (c0 - ct_ref[0, kb, h:h + 1, :])
            if diagonal:
                s = jnp.where(col <= row, s, NEG)
            m_new = jnp.maximum(m, jnp.max(s, axis=1, keepdims=True))
            a = jnp.exp(m - m_new)
            p = jnp.exp(s - m_new)
            l = a * l + jnp.sum(p, axis=1, keepdims=True)
            acc = a * acc + jnp.dot(p.astype(BF16), v, preferred_element_type=F32)
            return m_new, l, acc

        init = (jnp.full((tq, 1), NEG, F32), jnp.zeros((tq, 1), F32), jnp.zeros((tq, HEAD_DIM), F32))
        carry = lax.fori_loop(0, qi, functools.partial(step, diagonal=False), init)
        _, l, acc = step(qi, carry, True)
        o_ref[:, hs] = (acc / l * _silu(g_ref[:, hs].astype(F32))).astype(o_ref.dtype)


def _fox_attention(h_plain, ct, batch, seq, scale):
    tq = SEQ_TILE
    nb = seq // tq
    m = batch * seq
    return pl.pallas_call(
        functools.partial(_fox_kernel, scale=scale),
        out_shape=jax.ShapeDtypeStruct((m, D_FOX), BF16),
        grid=(batch, nb),
        in_specs=[pl.BlockSpec((tq, D_FOX), lambda b, i: (b * nb + i, 0)),
                  pl.BlockSpec((seq, D_FOX), lambda b, i: (b, 1)),
                  pl.BlockSpec((seq, D_FOX), lambda b, i: (b, 2)),
                  pl.BlockSpec((tq, D_FOX), lambda b, i: (b * nb + i, 3)),
                  pl.BlockSpec((1, nb, 8, tq), lambda b, i: (b, 0, 0, 0))],
        out_specs=pl.BlockSpec((tq, D_FOX), lambda b, i: (b * nb + i, 0)),
        compiler_params=_cparams(("parallel", "parallel"), VMEM_LIMIT_BYTES),
        name="fox_attention",
    )(h_plain, h_plain, h_plain, h_plain, ct)


def _dsa_kernel(dk_ref, dvt_ref, ik_ref, dqt_ref, iqt_ref, iwt_ref, g_ref, o_ref, keys_ref, *, topk):
    qi = pl.program_id(1)
    tq = o_ref.shape[0]
    tk = tq
    nkb = qi + 1
    row = lax.broadcasted_iota(jnp.int32, (tk, tq), 0)
    col = lax.broadcasted_iota(jnp.int32, (tk, tq), 1)

    def score_block(kb, carry):
        start = pl.multiple_of(kb * tk, tk)
        ikb = ik_ref[0, pl.ds(start, tk), :]
        acc = jnp.zeros((tk, tq), F32)
        for h in range(IDX_HEADS):
            z = jnp.dot(ikb, iqt_ref[0, 0, h * IDX_DIM:(h + 1) * IDX_DIM, :], preferred_element_type=F32)
            acc = acc + jnp.maximum(z, 0.0) * iwt_ref[0, 0, h:h + 1, :]
        key = _sortable_key(lax.bitcast_convert_type(acc, jnp.int32))
        key = jnp.where((kb == qi) & (row > col), KEY_NEG, key)
        keys_ref[kb] = key
        return carry

    lax.fori_loop(0, nkb, score_block, 0)

    def count(pred):
        def body(kb, acc):
            hit = pred(keys_ref[kb], row + kb * tk).astype(jnp.int32)
            return acc + jnp.sum(hit.reshape(tk // 8, 8, tq), axis=0)
        part = lax.fori_loop(0, nkb, body, jnp.zeros((8, tq), jnp.int32))
        return jnp.sum(part, axis=0, keepdims=True)

    n_nonneg = count(lambda k, _: k >= 0)
    prefix0 = jnp.where(n_nonneg >= topk, 0, INT_MIN).astype(jnp.int32)

    def bit_step(i, prefix):
        cand = prefix | (jnp.int32(1) << (30 - i))
        n = count(lambda k, _: k >= cand)
        return jnp.where(n >= topk, cand, prefix)

    vstar = lax.fori_loop(0, 31, bit_step, prefix0)

    n_ge = count(lambda k, _: k >= vstar)
    n_gt = count(lambda k, _: k > vstar)
    tie = jnp.max(((n_ge > topk) & (vstar > KEY_NEG)).astype(jnp.int32)) > 0
    big = jnp.full((1, tq), 2 ** 30, jnp.int32)

    def tie_index(_):
        def idx_step(i, jx):
            cand = jx | (jnp.int32(1) << (29 - i))
            n = n_gt + count(lambda k, idx: (k == vstar) & (idx < cand))
            return jnp.where(n <= topk, cand, jx)
        return lax.fori_loop(0, 30, idx_step, jnp.zeros((1, tq), jnp.int32))

    jx = lax.cond(tie, tie_index, lambda _: big, 0)

    def to_bias(kb, carry):
        k = keys_ref[kb]
        sel = ((k > vstar) | ((k == vstar) & ((row + kb * tk) < jx))) & (k > KEY_NEG)
        keys_ref[kb] = lax.bitcast_convert_type(jnp.where(sel, 0.0, NEG).astype(F32), jnp.int32)
        return carry

    lax.fori_loop(0, nkb, to_bias, 0)

    for h in range(H_DSA):
        hs = slice(h * HEAD_DIM, (h + 1) * HEAD_DIM)
        qt = dqt_ref[0, 0, hs, :]

        def step(kb, carry):
            m, l, acc = carry
            start = pl.multiple_of(kb * tk, tk)
            k = dk_ref[pl.ds(start, tk), hs]
            s = jnp.dot(k, qt, preferred_element_type=F32)
            s = s + lax.bitcast_convert_type(keys_ref[kb], F32)
            m_new = jnp.maximum(m, jnp.max(s, axis=0, keepdims=True))
            a = jnp.exp(m - m_new)
            p = jnp.exp(s - m_new)
            l = a * l + jnp.sum(p, axis=0, keepdims=True)
            acc = a * acc + jnp.dot(dvt_ref[0, kb, hs, :], p.astype(BF16), preferred_element_type=F32)
            return m_new, l, acc

        init = (jnp.full((1, tq), NEG, F32), jnp.zeros((1, tq), F32), jnp.zeros((HEAD_DIM, tq), F32))
        _, l, acc = lax.fori_loop(0, nkb, step, init)
        o = (acc / l).T
        o_ref[:, hs] = (o * _silu(g_ref[:, hs].astype(F32))).astype(o_ref.dtype)


def _dsa_attention(dk, dvt, ik, dqt, iqt, iwt, h_plain, batch, seq, topk):
    tq = SEQ_TILE
    nb = seq // tq
    m = batch * seq
    return pl.pallas_call(
        functools.partial(_dsa_kernel, topk=topk),
        out_shape=jax.ShapeDtypeStruct((m, D_DSA), BF16),
        grid=(batch, nb),
        in_specs=[pl.BlockSpec((seq, D_DSA), lambda b, i: (b, 0)),
                  pl.BlockSpec((1, nb, D_DSA, tq), lambda b, i: (b, 0, 0, 0)),
                  pl.BlockSpec((1, seq, IDX_DIM), lambda b, i: (b, 0, 0)),
                  pl.BlockSpec((1, 1, D_DSA, tq), lambda b, i: (b, i, 0, 0)),
                  pl.BlockSpec((1, 1, IDX_HEADS * IDX_DIM, tq), lambda b, i: (b, i, 0, 0)),
                  pl.BlockSpec((1, 1, IDX_HEADS, tq), lambda b, i: (b, i, 0, 0)),
                  pl.BlockSpec((tq, D_DSA), lambda b, i: (b * nb + i, 4))],
        out_specs=pl.BlockSpec((tq, D_DSA), lambda b, i: (b * nb + i, 0)),
        scratch_shapes=[pltpu.VMEM((nb, tq, tq), jnp.int32)],
        compiler_params=_cparams(("parallel", "parallel"), VMEM_LIMIT_BYTES),
        name="dsa_attention",
    )(dk, dvt, ik, dqt, iqt, iwt, h_plain)


def _mem_kernel(q0_ref, q1_ref, g0_ref, g1_ref, kv_ref, o_ref, *, scale):
    for h in range(H_MEM):
        q_ref = (q0_ref, q1_ref)[h // 2]
        g_ref = (g0_ref, g1_ref)[h // 2]
        ls = slice((h % 2) * HEAD_DIM, (h % 2 + 1) * HEAD_DIM)
        q = (q_ref[:, ls].astype(F32) * scale).astype(BF16)
        k = kv_ref[0, :, h * HEAD_DIM:(h + 1) * HEAD_DIM]
        v = kv_ref[0, :, D_MEMG + h * HEAD_DIM:D_MEMG + (h + 1) * HEAD_DIM]
        s = lax.dot_general(q, k, (((1,), (1,)), ((), ())), preferred_element_type=F32)
        m = jnp.max(s, axis=1, keepdims=True)
        p = jnp.exp(s - m)
        l = jnp.sum(p, axis=1, keepdims=True)
        o = jnp.dot(p.astype(BF16), v, preferred_element_type=F32) / l
        o_ref[:, h * HEAD_DIM:(h + 1) * HEAD_DIM] = (o * _silu(g_ref[:, ls].astype(F32))).astype(o_ref.dtype)


def _mem_attention(h_plain, kv, batch, seq, scale, q_col):
    tq = 2 * SEQ_TILE
    nb = seq // tq
    m = batch * seq
    n_mem = kv.shape[1]
    cb = q_col // 256
    spec = lambda c: pl.BlockSpec((tq, 256), lambda i: (i, c))
    return pl.pallas_call(
        functools.partial(_mem_kernel, scale=scale),
        out_shape=jax.ShapeDtypeStruct((m, D_MEMG), BF16),
        grid=(m // tq,),
        in_specs=[spec(cb), spec(cb + 1), spec(cb + 2), spec(cb + 3),
                  pl.BlockSpec((1, n_mem, 2 * D_MEMG), lambda i: (i // nb, 0, 0))],
        out_specs=pl.BlockSpec((tq, D_MEMG), lambda i: (i, 0)),
        compiler_params=_cparams(("parallel",)),
        name="mem_attention",
    )(h_plain, h_plain, h_plain, h_plain, kv)


def _out_ln_kernel(yf_ref, yd_ref, ym_ref, w_ref, x_ref, g_ref, b_ref, xo_ref, xb_ref, *, alpha):
    y = jnp.dot(yf_ref[...], w_ref[0:D_FOX, :], preferred_element_type=F32)
    y = y + jnp.dot(yd_ref[...], w_ref[D_FOX:D_FOX + D_DSA, :], preferred_element_type=F32)
    y = y + jnp.dot(ym_ref[...], w_ref[D_FOX + D_DSA:, :], preferred_element_type=F32)
    z = alpha * x_ref[...] + y
    mu = jnp.mean(z, axis=1, keepdims=True)
    zc = z - mu
    var = jnp.mean(zc * zc, axis=1, keepdims=True)
    out = zc * lax.rsqrt(var + LN_EPS) * g_ref[...] + b_ref[...]
    xo_ref[...] = out
    xb_ref[...] = out.astype(xb_ref.dtype)


def _out_ln(yf, yd, ym, w_out, x, gain, bias, alpha):
    m, d = x.shape
    tm = SEQ_TILE
    row = lambda n: pl.BlockSpec((tm, n), lambda i: (i, 0))
    const = lambda r, c: pl.BlockSpec((r, c), lambda i: (0, 0))
    return pl.pallas_call(
        functools.partial(_out_ln_kernel, alpha=alpha),
        out_shape=(jax.ShapeDtypeStruct((m, d), F32), jax.ShapeDtypeStruct((m, d), BF16)),
        grid=(m // tm,),
        in_specs=[row(D_FOX), row(D_DSA), row(D_MEMG), const(w_out.shape[0], d), row(d), const(1, d), const(1, d)],
        out_specs=(row(d), row(d)),
        compiler_params=_cparams(("parallel",), VMEM_LIMIT_BYTES),
        name="out_proj_layernorm",
    )(yf, yd, ym, w_out, x, gain, bias)


def _rope_tables(seq):
    pos = jnp.arange(seq).astype(F32)

    def tables(half):
        inv_freq = ROPE_THETA ** (-jnp.arange(half, dtype=F32) / half)
        ang = pos[:, None] * inv_freq[None, :]
        return jnp.cos(ang), jnp.sin(ang)

    c64, s64 = tables(HEAD_DIM // 2)
    c32, s32 = tables(IDX_DIM // 2)
    cos128 = jnp.concatenate([c64, c64], axis=1)
    sin128 = jnp.concatenate([-s64, s64], axis=1)
    zeros = jnp.zeros((seq, 128 - IDX_DIM), F32)
    cos_ik = jnp.concatenate([c32, c32, zeros], axis=1)
    sin_ik = jnp.concatenate([s32, s32, zeros], axis=1)
    return cos128, sin128, c64.T, s64.T, c32.T, s32.T, cos_ik, sin_ik


def _prepare(seq, d_model, w_in, b_forget, w_mem_kv, w_out):
    depth = w_in.shape[0]
    splits = (D_FOX, D_FOX, D_FOX, D_FOX, H_FOX, D_DSA, D_DSA, D_DSA, D_DSA,
              IDX_HEADS * IDX_DIM, IDX_DIM, IDX_HEADS, D_MEMG, D_MEMG)
    offs = np.concatenate([[0], np.cumsum(splits)]).tolist()
    seg = lambda i: w_in[:, :, offs[i]:offs[i + 1]]
    (w_fq, w_fk, w_fv, w_fg, w_fl, w_dq, w_dk, w_dv, w_dg, w_iq, w_ik, w_iw, w_mq, w_mg) = [seg(i) for i in range(14)]
    pad = jnp.zeros((depth, d_model, 128 - IDX_DIM - IDX_HEADS - H_FOX), w_in.dtype)
    off_fl = IDX_DIM + IDX_HEADS
    cos128, sin128, c64t, s64t, c32t, s32t, cos_ik, sin_ik = _rope_tables(seq)
    ts = SEQ_TILE
    return dict(
        w_plain=jnp.concatenate([w_fq, w_fk, w_fv, w_fg, w_dg, w_mq, w_mg], axis=2).astype(BF16),
        q_col=4 * D_FOX + D_DSA,
        w_dk=w_dk.astype(BF16),
        w_t=jnp.swapaxes(jnp.concatenate([w_dq, w_dv, w_iq], axis=2), 1, 2).astype(BF16),
        w_small=jnp.concatenate([w_ik, w_iw, w_fl, pad], axis=2).astype(BF16),
        w_out=w_out.astype(BF16),
        w_mem=w_mem_kv.astype(BF16),
        bf_rows=jnp.zeros((depth, 1, 128), F32).at[:, 0, off_fl:off_fl + H_FOX].set(b_forget.astype(F32)),
        cos128=cos128, sin128=sin128, c64t=c64t, s64t=s64t, c32t=c32t, s32t=s32t, cos_ik=cos_ik, sin_ik=sin_ik,
        tri=(jnp.arange(ts)[:, None] >= jnp.arange(ts)[None, :]).astype(BF16),
    )


def _mixers(xb, mem_b, l, p, batch, seq, n_mem):
    scale = HEAD_DIM ** -0.5
    topk = min(INDEX_TOPK, seq // 4)
    w_scale = (IDX_HEADS ** -0.5) * (IDX_DIM ** -0.5)
    kv = _proj_plain(mem_b, p["w_mem"][l], batch * n_mem, 2 * D_MEMG, "proj_mem_kv")
    kv = kv.reshape(batch, n_mem, 2 * D_MEMG)
    h_plain = _proj_plain(xb, p["w_plain"][l], 2 * SEQ_TILE, p["w_plain"].shape[2] // 2, "proj_plain")
    dk = _proj_rope(xb, p["w_dk"][l], p["cos128"], p["sin128"], seq, 2 * SEQ_TILE, "proj_dsa_key")
    dqt, dvt, iqt = _proj_t(xb, p["w_t"][l], p["c64t"], p["s64t"], p["c32t"], p["s32t"], batch, seq, scale)
    ik, iwt, ct = _prep(xb, p["w_small"][l], p["bf_rows"][l], p["cos_ik"], p["sin_ik"], p["tri"], batch, seq, w_scale)
    y_fox = _fox_attention(h_plain, ct, batch, seq, scale)
    y_dsa = _dsa_attention(dk, dvt, ik, dqt, iqt, iwt, h_plain, batch, seq, topk)
    y_mem = _mem_attention(h_plain, kv, batch, seq, scale, p["q_col"])
    return y_fox, y_dsa, y_mem


def kernel(x, mem, w_in, b_forget, w_mem_kv, w_out, ln_gain, ln_bias):
    batch, seq, d_model = x.shape
    depth = w_in.shape[0]
    n_mem = mem.shape[1]
    m = batch * seq
    alpha = (2 * depth) ** 0.25
    assert seq % (2 * SEQ_TILE) == 0 and min(INDEX_TOPK, seq // 4) <= SEQ_TILE

    p = _prepare(seq, d_model, w_in, b_forget, w_mem_kv, w_out)
    mem_b = mem.reshape(batch * n_mem, d_model).astype(BF16)
    xf = x.reshape(m, d_model).astype(F32)
    xb = xf.astype(BF16)
    for l in range(depth):
        y_fox, y_dsa, y_mem = _mixers(xb, mem_b, l, p, batch, seq, n_mem)
        xf, xb = _out_ln(y_fox, y_dsa, y_mem, p["w_out"][l], xf, ln_gain[l][None, :].astype(F32),
                         ln_bias[l][None, :].astype(F32), alpha)
    return xf.reshape(batch, seq, d_model).astype(x.dtype)
```

```python
import functools

import numpy as np
import jax
import jax.numpy as jnp
from jax import lax
from jax.experimental import pallas as pl
from jax.experimental.pallas import tpu as pltpu

HEAD_DIM = 128
H_FOX = 6
H_DSA = 6
H_MEM = 4
D_FOX = H_FOX * HEAD_DIM
D_DSA = H_DSA * HEAD_DIM
D_MEMG = H_MEM * HEAD_DIM
IDX_HEADS = 16
IDX_DIM = 64
INDEX_TOPK = 256
ROPE_THETA = 10000.0
LN_EPS = 1e-5
NEG = -1e30

SEQ_TILE = 256
VMEM_LIMIT_BYTES = 56 * 1024 * 1024
OFF_IW = IDX_DIM
OFF_FL = IDX_DIM + IDX_HEADS
BIAS_PIECES = 3

F32 = jnp.float32
BF16 = jnp.bfloat16


def _sortable_key(bits):
    return bits ^ ((bits >> 31) & 0x7FFFFFFF)


_NEG_BITS = int(np.array(NEG, np.float32).view(np.int32))
KEY_NEG = int(np.int32(_NEG_BITS) ^ np.int32(0x7FFFFFFF))
INT_MIN = -(2 ** 31)


def _cparams(sem, vmem=None):
    return pltpu.CompilerParams(dimension_semantics=sem, vmem_limit_bytes=vmem)


def _silu(g):
    return g / (1.0 + jnp.exp(-g))


def _split_bf16(x, pieces):
    out = []
    r = x
    for _ in range(pieces - 1):
        p = r.astype(BF16)
        out.append(p)
        r = r - p.astype(F32)
    out.append(r.astype(BF16))
    return out


def _proj_plain_kernel(x_ref, w_ref, o_ref):
    o_ref[...] = jnp.dot(x_ref[...], w_ref[...], preferred_element_type=F32).astype(o_ref.dtype)


def _proj_plain(xb, w, tm, tn, name):
    m, d = xb.shape
    n = w.shape[1]
    return pl.pallas_call(
        _proj_plain_kernel,
        out_shape=jax.ShapeDtypeStruct((m, n), BF16),
        grid=(n // tn, m // tm),
        in_specs=[pl.BlockSpec((tm, d), lambda j, i: (i, 0)),
                  pl.BlockSpec((d, tn), lambda j, i: (0, j))],
        out_specs=pl.BlockSpec((tm, tn), lambda j, i: (i, j)),
        compiler_params=_cparams(("parallel", "parallel"), VMEM_LIMIT_BYTES),
        name=name,
    )(xb, w)


def _proj_rope_kernel(x_ref, w_ref, cos_ref, sin_ref, o_ref, *, heads):
    y = jnp.dot(x_ref[...], w_ref[...], preferred_element_type=F32)
    c = cos_ref[...]
    s = sin_ref[...]
    for h in range(heads):
        yh = y[:, h * HEAD_DIM:(h + 1) * HEAD_DIM]
        rot = pltpu.roll(yh, HEAD_DIM // 2, axis=1)
        o_ref[:, h * HEAD_DIM:(h + 1) * HEAD_DIM] = (yh * c + rot * s).astype(o_ref.dtype)


def _proj_rope(xb, w, cos128, sin128, seq, tm, name):
    m, d = xb.shape
    n = w.shape[1]
    nseq = seq // tm
    return pl.pallas_call(
        functools.partial(_proj_rope_kernel, heads=n // HEAD_DIM),
        out_shape=jax.ShapeDtypeStruct((m, n), BF16),
        grid=(m // tm,),
        in_specs=[pl.BlockSpec((tm, d), lambda i: (i, 0)),
                  pl.BlockSpec((d, n), lambda i: (0, 0)),
                  pl.BlockSpec((tm, HEAD_DIM), lambda i: (i % nseq, 0)),
                  pl.BlockSpec((tm, HEAD_DIM), lambda i: (i % nseq, 0))],
        out_specs=pl.BlockSpec((tm, n), lambda i: (i, 0)),
        compiler_params=_cparams(("parallel",), VMEM_LIMIT_BYTES),
        name=name,
    )(xb, w, cos128, sin128)


def _proj_t_kernel(x_ref, wt_ref, cos64_ref, sin64_ref, cos32_ref, sin32_ref, *out_refs, groups, scale):
    yt = lax.dot_general(wt_ref[...], x_ref[...], (((1,), (1,)), ((), ())), preferred_element_type=F32)
    base = 0
    for (kind, rows), o_ref in zip(groups, out_refs):
        y = yt[base:base + rows]
        base += rows
        if kind == "plain":
            o_ref[0, 0] = y.astype(o_ref.dtype)
        elif kind == "scaled":
            o_ref[0, 0] = (y * scale).astype(o_ref.dtype)
        else:
            dim, cos_t, sin_t, mul = ((HEAD_DIM, cos64_ref[...], sin64_ref[...], scale) if kind == "rope_head_scaled"
                                      else (IDX_DIM, cos32_ref[...], sin32_ref[...], 1.0))
            half = dim // 2
            for h in range(rows // dim):
                x1 = y[h * dim:h * dim + half]
                x2 = y[h * dim + half:(h + 1) * dim]
                o_ref[0, 0, h * dim:h * dim + half, :] = ((x1 * cos_t - x2 * sin_t) * mul).astype(o_ref.dtype)
                o_ref[0, 0, h * dim + half:(h + 1) * dim, :] = ((x2 * cos_t + x1 * sin_t) * mul).astype(o_ref.dtype)


def _proj_t(xb, wt, tables, batch, seq, scale, groups, name):
    m, d = xb.shape
    ts = SEQ_TILE
    nb = seq // ts
    rows = wt.shape[0]
    assert rows == sum(r for _, r in groups)
    blk = lambda r: pl.BlockSpec((1, 1, r, ts), lambda i: (i // nb, i % nb, 0, 0))
    tab = lambda r: pl.BlockSpec((r, ts), lambda i: (0, i % nb))
    return pl.pallas_call(
        functools.partial(_proj_t_kernel, groups=groups, scale=scale),
        out_shape=tuple(jax.ShapeDtypeStruct((batch, nb, r, ts), BF16) for _, r in groups),
        grid=(m // ts,),
        in_specs=[pl.BlockSpec((ts, d), lambda i: (i, 0)),
                  pl.BlockSpec((rows, d), lambda i: (0, 0)),
                  tab(HEAD_DIM // 2), tab(HEAD_DIM // 2), tab(IDX_DIM // 2), tab(IDX_DIM // 2)],
        out_specs=tuple(blk(r) for _, r in groups),
        compiler_params=_cparams(("parallel",), VMEM_LIMIT_BYTES),
        name=name,
    )(xb, wt, *tables)


def _prep_kernel(x_ref, w_ref, bf_ref, cos_ref, sin_ref, tri_ref, ik_ref, iwt_ref, cb_ref, carry_ref, *, w_scale):
    j = pl.program_id(1)
    raw = jnp.dot(x_ref[...], w_ref[...], preferred_element_type=F32)
    ts = raw.shape[0]

    lane = lax.broadcasted_iota(jnp.int32, raw.shape, 1)
    q = IDX_DIM // 2
    rot = jnp.where(lane < q, -pltpu.roll(raw, 128 - q, axis=1), pltpu.roll(raw, q, axis=1))
    ik = raw * cos_ref[...] + rot * sin_ref[...]
    ik_ref[0] = ik[:, :IDX_DIM].astype(ik_ref.dtype)

    iwt_ref[0, 0] = raw.T[OFF_IW:OFF_IW + IDX_HEADS] * w_scale

    z = raw + bf_ref[...]
    logf = jnp.minimum(z, 0.0) - jnp.log(1.0 + jnp.exp(-jnp.abs(z)))
    tri = tri_ref[...]
    cs = sum(jnp.dot(tri, p, preferred_element_type=F32) for p in _split_bf16(logf, 3))

    @pl.when(j == 0)
    def _():
        carry_ref[...] = jnp.zeros_like(carry_ref)

    cs = cs + carry_ref[...]
    carry_ref[...] = cs[ts - 1:ts, :]

    r = lax.broadcasted_iota(jnp.int32, (128, 128), 0)
    c = lax.broadcasted_iota(jnp.int32, (128, 128), 1)
    head_row = (r >= OFF_FL) & (r < OFF_FL + H_FOX)
    cb = jnp.zeros((ts, 128), F32)
    for k, piece in enumerate(_split_bf16(-cs, BIAS_PIECES)):
        place = (head_row & (c == BIAS_PIECES * (r - OFF_FL) + k)).astype(BF16)
        cb = cb + jnp.dot(piece, place, preferred_element_type=F32)
    cb_ref[0] = cb.astype(cb_ref.dtype)


def _prep(xb, w_small, bf_row, cos_ik, sin_ik, tri, batch, seq, w_scale):
    m, d = xb.shape
    ts = SEQ_TILE
    nb = seq // ts
    out_shapes = (jax.ShapeDtypeStruct((batch, seq, IDX_DIM), BF16),
                  jax.ShapeDtypeStruct((batch, nb, IDX_HEADS, ts), F32),
                  jax.ShapeDtypeStruct((batch, seq, 128), BF16))
    return pl.pallas_call(
        functools.partial(_prep_kernel, w_scale=w_scale),
        out_shape=out_shapes,
        grid=(batch, nb),
        in_specs=[pl.BlockSpec((ts, d), lambda b, j: (b * nb + j, 0)),
                  pl.BlockSpec((d, 128), lambda b, j: (0, 0)),
                  pl.BlockSpec((1, 128), lambda b, j: (0, 0)),
                  pl.BlockSpec((ts, 128), lambda b, j: (j, 0)),
                  pl.BlockSpec((ts, 128), lambda b, j: (j, 0)),
                  pl.BlockSpec((ts, ts), lambda b, j: (0, 0))],
        out_specs=(pl.BlockSpec((1, ts, IDX_DIM), lambda b, j: (b, j, 0)),
                   pl.BlockSpec((1, 1, IDX_HEADS, ts), lambda b, j: (b, j, 0, 0)),
                   pl.BlockSpec((1, ts, 128), lambda b, j: (b, j, 0))),
        scratch_shapes=[pltpu.VMEM((1, 128), F32)],
        compiler_params=_cparams(("parallel", "arbitrary")),
        name="prep_small",
    )(xb, w_small, bf_row, cos_ik, sin_ik, tri)


def _attention_scratch(heads, tq):
    return [pltpu.VMEM((heads, 1, tq), F32),
            pltpu.VMEM((heads, 1, tq), F32),
            pltpu.VMEM((heads, HEAD_DIM, tq), F32),
            pltpu.VMEM((heads, 1, tq), F32),
            pltpu.VMEM((heads, tq, tq), BF16)]


def _attention_core(heads, qi, score_fn, values_fn, g_ref, o_ref, m_ref, l_ref, acc_ref, a_ref, p_ref):
    m_ref[...] = jnp.full(m_ref.shape, NEG, F32)
    l_ref[...] = jnp.zeros(l_ref.shape, F32)
    acc_ref[...] = jnp.zeros(acc_ref.shape, F32)
    a_ref[...] = jnp.ones(a_ref.shape, F32)
    p_ref[...] = jnp.zeros(p_ref.shape, BF16)

    def score_softmax(kb, last):
        for h in range(heads):
            s = score_fn(kb, h, last)
            m = m_ref[h]
            m_new = jnp.maximum(m, jnp.max(s, axis=0, keepdims=True))
            a = jnp.exp(m - m_new)
            p = jnp.exp(s - m_new)
            l_ref[h] = a * l_ref[h] + jnp.sum(p, axis=0, keepdims=True)
            m_ref[h] = m_new
            a_ref[h] = a
            p_ref[h] = p.astype(BF16)

    def weighted_values(kb):
        for h in range(heads):
            pv = jnp.dot(values_fn(kb, h), p_ref[h], preferred_element_type=F32)
            acc_ref[h] = a_ref[h] * acc_ref[h] + pv

    def step(kb, carry):
        weighted_values(jnp.maximum(kb - 1, 0))
        score_softmax(kb, False)
        return carry

    lax.fori_loop(0, qi, step, 0)
    weighted_values(jnp.maximum(qi - 1, 0))
    score_softmax(qi, True)
    weighted_values(qi)
    for h in range(heads):
        hs = slice(h * HEAD_DIM, (h + 1) * HEAD_DIM)
        o = (acc_ref[h] / l_ref[h]).T
        o_ref[:, hs] = (o * _silu(g_ref[:, hs].astype(F32))).astype(o_ref.dtype)


def _fox_kernel(k_ref, cb_ref, qt_ref, vt_ref, g_ref, o_ref, qa_ref, *scratch):
    qi = pl.program_id(1)
    tq = o_ref.shape[0]
    tk = tq
    row = lax.broadcasted_iota(jnp.int32, (tk, tq), 0)
    col = lax.broadcasted_iota(jnp.int32, (tk, tq), 1)
    sel_row = lax.broadcasted_iota(jnp.int32, (HEAD_DIM, tq), 0)
    for h in range(H_FOX):
        qa_ref[h, 0:HEAD_DIM, :] = qt_ref[0, 0, h * HEAD_DIM:(h + 1) * HEAD_DIM, :]
        pick = (sel_row >= BIAS_PIECES * h) & (sel_row < BIAS_PIECES * (h + 1))
        qa_ref[h, HEAD_DIM:2 * HEAD_DIM, :] = pick.astype(BF16)

    def score_fn(kb, h, last):
        start = pl.multiple_of(kb * tk, tk)
        keys = jnp.concatenate([k_ref[pl.ds(start, tk), h * HEAD_DIM:(h + 1) * HEAD_DIM],
                                cb_ref[0, pl.ds(start, tk), :]], axis=1)
        s = jnp.dot(keys, qa_ref[h], preferred_element_type=F32)
        if last:
            s = jnp.where(row <= col, s, NEG)
        return s

    def values_fn(kb, h):
        return vt_ref[0, kb, h * HEAD_DIM:(h + 1) * HEAD_DIM, :]

    _attention_core(H_FOX, qi, score_fn, values_fn, g_ref, o_ref, *scratch)


def _fox_attention(h_plain, cb, fqt, fvt, batch, seq):
    tq = SEQ_TILE
    nb = seq // tq
    m = batch * seq
    return pl.pallas_call(
        _fox_kernel,
        out_shape=jax.ShapeDtypeStruct((m, D_FOX), BF16),
        grid=(batch, nb),
        in_specs=[pl.BlockSpec((seq, D_FOX), lambda b, i: (b, 0)),
                  pl.BlockSpec((1, seq, 128), lambda b, i: (b, 0, 0)),
                  pl.BlockSpec((1, 1, D_FOX, tq), lambda b, i: (b, i, 0, 0)),
                  pl.BlockSpec((1, nb, D_FOX, tq), lambda b, i: (b, 0, 0, 0)),
                  pl.BlockSpec((tq, D_FOX), lambda b, i: (b * nb + i, 1))],
        out_specs=pl.BlockSpec((tq, D_FOX), lambda b, i: (b * nb + i, 0)),
        scratch_shapes=[pltpu.VMEM((H_FOX, 2 * HEAD_DIM, tq), BF16)] + _attention_scratch(H_FOX, tq),
        compiler_params=_cparams(("parallel", "parallel"), VMEM_LIMIT_BYTES),
        name="fox_attention",
    )(h_plain, cb, fqt, fvt, h_plain)


def _dsa_kernel(dk_ref, dvt_ref, ik_ref, dqt_ref, iqt_ref, iwt_ref, g_ref, o_ref, keys_ref, *scratch, topk):
    qi = pl.program_id(1)
    tq = o_ref.shape[0]
    tk = tq
    nkb = qi + 1
    row = lax.broadcasted_iota(jnp.int32, (tk, tq), 0)
    col = lax.broadcasted_iota(jnp.int32, (tk, tq), 1)

    def score_block(kb, carry):
        start = pl.multiple_of(kb * tk, tk)
        ikb = ik_ref[0, pl.ds(start, tk), :]
        acc = jnp.zeros((tk, tq), F32)
        for h in range(IDX_HEADS):
            z = jnp.dot(ikb, iqt_ref[0, 0, h * IDX_DIM:(h + 1) * IDX_DIM, :], preferred_element_type=F32)
            acc = acc + jnp.maximum(z, 0.0) * iwt_ref[0, 0, h:h + 1, :]
        key = _sortable_key(lax.bitcast_convert_type(acc, jnp.int32))
        key = jnp.where((kb == qi) & (row > col), KEY_NEG, key)
        keys_ref[kb] = key
        return carry

    lax.fori_loop(0, nkb, score_block, 0)

    def count(pred):
        def body(kb, acc):
            hit = pred(keys_ref[kb], row + kb * tk).astype(jnp.int32)
            return acc + jnp.sum(hit.reshape(tk // 8, 8, tq), axis=0)
        part = lax.fori_loop(0, nkb, body, jnp.zeros((8, tq), jnp.int32))
        return jnp.sum(part, axis=0, keepdims=True)

    n_nonneg = count(lambda k, _: k >= 0)
    prefix0 = jnp.where(n_nonneg >= topk, 0, INT_MIN).astype(jnp.int32)

    def bit_step(i, prefix):
        cand = prefix | (jnp.int32(1) << (30 - i))
        n = count(lambda k, _: k >= cand)
        return jnp.where(n >= topk, cand, prefix)

    vstar = lax.fori_loop(0, 31, bit_step, prefix0)

    n_ge = count(lambda k, _: k >= vstar)
    n_gt = count(lambda k, _: k > vstar)
    tie = jnp.max(((n_ge > topk) & (vstar > KEY_NEG)).astype(jnp.int32)) > 0
    big = jnp.full((1, tq), 2 ** 30, jnp.int32)

    def tie_index(_):
        def idx_step(i, jx):
            cand = jx | (jnp.int32(1) << (29 - i))
            n = n_gt + count(lambda k, idx: (k == vstar) & (idx < cand))
            return jnp.where(n <= topk, cand, jx)
        return lax.fori_loop(0, 30, idx_step, jnp.zeros((1, tq), jnp.int32))

    jx = lax.cond(tie, tie_index, lambda _: big, 0)

    def to_bias(kb, carry):
        k = keys_ref[kb]
        sel = ((k > vstar) | ((k == vstar) & ((row + kb * tk) < jx))) & (k > KEY_NEG)
        keys_ref[kb] = lax.bitcast_convert_type(jnp.where(sel, 0.0, NEG).astype(F32), jnp.int32)
        return carry

    lax.fori_loop(0, nkb, to_bias, 0)

    def score_fn(kb, h, last):
        start = pl.multiple_of(kb * tk, tk)
        hs = slice(h * HEAD_DIM, (h + 1) * HEAD_DIM)
        s = jnp.dot(dk_ref[pl.ds(start, tk), hs], dqt_ref[0, 0, hs, :], preferred_element_type=F32)
        return s + lax.bitcast_convert_type(keys_ref[kb], F32)

    def values_fn(kb, h):
        return dvt_ref[0, kb, h * HEAD_DIM:(h + 1) * HEAD_DIM, :]

    _attention_core(H_DSA, qi, score_fn, values_fn, g_ref, o_ref, *scratch)


def _dsa_attention(dk, dvt, ik, dqt, iqt, iwt, h_plain, batch, seq, topk):
    tq = SEQ_TILE
    nb = seq // tq
    m = batch * seq
    return pl.pallas_call(
        functools.partial(_dsa_kernel, topk=topk),
        out_shape=jax.ShapeDtypeStruct((m, D_DSA), BF16),
        grid=(batch, nb),
        in_specs=[pl.BlockSpec((seq, D_DSA), lambda b, i: (b, 0)),
                  pl.BlockSpec((1, nb, D_DSA, tq), lambda b, i: (b, 0, 0, 0)),
                  pl.BlockSpec((1, seq, IDX_DIM), lambda b, i: (b, 0, 0)),
                  pl.BlockSpec((1, 1, D_DSA, tq), lambda b, i: (b, i, 0, 0)),
                  pl.BlockSpec((1, 1, IDX_HEADS * IDX_DIM, tq), lambda b, i: (b, i, 0, 0)),
                  pl.BlockSpec((1, 1, IDX_HEADS, tq), lambda b, i: (b, i, 0, 0)),
                  pl.BlockSpec((tq, D_DSA), lambda b, i: (b * nb + i, 2))],
        out_specs=pl.BlockSpec((tq, D_DSA), lambda b, i: (b * nb + i, 0)),
        scratch_shapes=[pltpu.VMEM((nb, tq, tq), jnp.int32)] + _attention_scratch(H_DSA, tq),
        compiler_params=_cparams(("parallel", "parallel"), VMEM_LIMIT_BYTES),
        name="dsa_attention",
    )(dk, dvt, ik, dqt, iqt, iwt, h_plain)


def _mem_kernel(q0_ref, q1_ref, g0_ref, g1_ref, kv_ref, o_ref, *, scale):
    for h in range(H_MEM):
        q_ref = (q0_ref, q1_ref)[h // 2]
        g_ref = (g0_ref, g1_ref)[h // 2]
        ls = slice((h % 2) * HEAD_DIM, (h % 2 + 1) * HEAD_DIM)
        q = (q_ref[:, ls].astype(F32) * scale).astype(BF16)
        k = kv_ref[0, :, h * HEAD_DIM:(h + 1) * HEAD_DIM]
        v = kv_ref[0, :, D_MEMG + h * HEAD_DIM:D_MEMG + (h + 1) * HEAD_DIM]
        s = lax.dot_general(q, k, (((1,), (1,)), ((), ())), preferred_element_type=F32)
        m = jnp.max(s, axis=1, keepdims=True)
        p = jnp.exp(s - m)
        l = jnp.sum(p, axis=1, keepdims=True)
        o = jnp.dot(p.astype(BF16), v, preferred_element_type=F32) / l
        o_ref[:, h * HEAD_DIM:(h + 1) * HEAD_DIM] = (o * _silu(g_ref[:, ls].astype(F32))).astype(o_ref.dtype)


def _mem_attention(h_plain, kv, batch, seq, scale, q_col):
    tq = 2 * SEQ_TILE
    nb = seq // tq
    m = batch * seq
    n_mem = kv.shape[1]
    cb = q_col // 256
    spec = lambda c: pl.BlockSpec((tq, 256), lambda i: (i, c))
    return pl.pallas_call(
        functools.partial(_mem_kernel, scale=scale),
        out_shape=jax.ShapeDtypeStruct((m, D_MEMG), BF16),
        grid=(m // tq,),
        in_specs=[spec(cb), spec(cb + 1), spec(cb + 2), spec(cb + 3),
                  pl.BlockSpec((1, n_mem, 2 * D_MEMG), lambda i: (i // nb, 0, 0))],
        out_specs=pl.BlockSpec((tq, D_MEMG), lambda i: (i, 0)),
        compiler_params=_cparams(("parallel",)),
        name="mem_attention",
    )(h_plain, h_plain, h_plain, h_plain, kv)


def _out_ln_kernel(yf_ref, yd_ref, ym_ref, w_ref, x_ref, g_ref, b_ref, xo_ref, xb_ref, *, alpha):
    y = jnp.dot(yf_ref[...], w_ref[0:D_FOX, :], preferred_element_type=F32)
    y = y + jnp.dot(yd_ref[...], w_ref[D_FOX:D_FOX + D_DSA, :], preferred_element_type=F32)
    y = y + jnp.dot(ym_ref[...], w_ref[D_FOX + D_DSA:, :], preferred_element_type=F32)
    z = alpha * x_ref[...] + y
    mu = jnp.mean(z, axis=1, keepdims=True)
    zc = z - mu
    var = jnp.mean(zc * zc, axis=1, keepdims=True)
    out = zc * lax.rsqrt(var + LN_EPS) * g_ref[...] + b_ref[...]
    xo_ref[...] = out
    xb_ref[...] = out.astype(xb_ref.dtype)


def _out_ln(yf, yd, ym, w_out, x, gain, bias, alpha):
    m, d = x.shape
    tm = SEQ_TILE
    row = lambda n: pl.BlockSpec((tm, n), lambda i: (i, 0))
    const = lambda r, c: pl.BlockSpec((r, c), lambda i: (0, 0))
    return pl.pallas_call(
        functools.partial(_out_ln_kernel, alpha=alpha),
        out_shape=(jax.ShapeDtypeStruct((m, d), F32), jax.ShapeDtypeStruct((m, d), BF16)),
        grid=(m // tm,),
        in_specs=[row(D_FOX), row(D_DSA), row(D_MEMG), const(w_out.shape[0], d), row(d), const(1, d), const(1, d)],
        out_specs=(row(d), row(d)),
        compiler_params=_cparams(("parallel",), VMEM_LIMIT_BYTES),
        name="out_proj_layernorm",
    )(yf, yd, ym, w_out, x, gain, bias)


def _rope_tables(seq):
    pos = jnp.arange(seq).astype(F32)

    def tables(half):
        inv_freq = ROPE_THETA ** (-jnp.arange(half, dtype=F32) / half)
        ang = pos[:, None] * inv_freq[None, :]
        return jnp.cos(ang), jnp.sin(ang)

    c64, s64 = tables(HEAD_DIM // 2)
    c32, s32 = tables(IDX_DIM // 2)
    cos128 = jnp.concatenate([c64, c64], axis=1)
    sin128 = jnp.concatenate([-s64, s64], axis=1)
    zeros = jnp.zeros((seq, 128 - IDX_DIM), F32)
    cos_ik = jnp.concatenate([c32, c32, zeros], axis=1)
    sin_ik = jnp.concatenate([s32, s32, zeros], axis=1)
    return cos128, sin128, (c64.T, s64.T, c32.T, s32.T), cos_ik, sin_ik


def _prepare(seq, d_model, w_in, b_forget, w_mem_kv, w_out):
    depth = w_in.shape[0]
    splits = (D_FOX, D_FOX, D_FOX, D_FOX, H_FOX, D_DSA, D_DSA, D_DSA, D_DSA,
              IDX_HEADS * IDX_DIM, IDX_DIM, IDX_HEADS, D_MEMG, D_MEMG)
    offs = np.concatenate([[0], np.cumsum(splits)]).tolist()
    seg = lambda i: w_in[:, :, offs[i]:offs[i + 1]]
    (w_fq, w_fk, w_fv, w_fg, w_fl, w_dq, w_dk, w_dv, w_dg, w_iq, w_ik, w_iw, w_mq, w_mg) = [seg(i) for i in range(14)]
    pad = jnp.zeros((depth, d_model, 128 - IDX_DIM - IDX_HEADS - H_FOX), w_in.dtype)
    transposed = lambda ws: jnp.swapaxes(jnp.concatenate(ws, axis=2), 1, 2).astype(BF16)
    cos128, sin128, t_tables, cos_ik, sin_ik = _rope_tables(seq)
    ts = SEQ_TILE
    return dict(
        w_plain=jnp.concatenate([w_fk, w_fg, w_dg, w_mq, w_mg], axis=2).astype(BF16),
        q_col=2 * D_FOX + D_DSA,
        w_dk=w_dk.astype(BF16),
        w_t_dsa=transposed([w_dq, w_dv, w_iq]),
        w_t_fox=transposed([w_fq, w_fv]),
        w_small=jnp.concatenate([w_ik, w_iw, w_fl, pad], axis=2).astype(BF16),
        w_out=w_out.astype(BF16),
        w_mem=w_mem_kv.astype(BF16),
        bf_rows=jnp.zeros((depth, 1, 128), F32).at[:, 0, OFF_FL:OFF_FL + H_FOX].set(b_forget.astype(F32)),
        cos128=cos128, sin128=sin128, t_tables=t_tables, cos_ik=cos_ik, sin_ik=sin_ik,
        tri=(jnp.arange(ts)[:, None] >= jnp.arange(ts)[None, :]).astype(BF16),
    )


def _mixers(xb, mem_b, l, p, batch, seq, n_mem):
    scale = HEAD_DIM ** -0.5
    topk = min(INDEX_TOPK, seq // 4)
    w_scale = (IDX_HEADS ** -0.5) * (IDX_DIM ** -0.5)
    kv = _proj_plain(mem_b, p["w_mem"][l], batch * n_mem, 2 * D_MEMG, "proj_mem_kv")
    kv = kv.reshape(batch, n_mem, 2 * D_MEMG)
    h_plain = _proj_plain(xb, p["w_plain"][l], 2 * SEQ_TILE, p["w_plain"].shape[2] // 2, "proj_plain")
    dk = _proj_rope(xb, p["w_dk"][l], p["cos128"], p["sin128"], seq, 2 * SEQ_TILE, "proj_dsa_key")
    dqt, dvt, iqt = _proj_t(xb, p["w_t_dsa"][l], p["t_tables"], batch, seq, scale,
                            (("rope_head_scaled", D_DSA), ("plain", D_DSA), ("rope_index", IDX_HEADS * IDX_DIM)),
                            "proj_transposed_dsa")
    fqt, fvt = _proj_t(xb, p["w_t_fox"][l], p["t_tables"], batch, seq, scale,
                       (("scaled", D_FOX), ("plain", D_FOX)), "proj_transposed_fox")
    ik, iwt, cb = _prep(xb, p["w_small"][l], p["bf_rows"][l], p["cos_ik"], p["sin_ik"], p["tri"], batch, seq, w_scale)
    y_fox = _fox_attention(h_plain, cb, fqt, fvt, batch, seq)
    y_dsa = _dsa_attention(dk, dvt, ik, dqt, iqt, iwt, h_plain, batch, seq, topk)
    y_mem = _mem_attention(h_plain, kv, batch, seq, scale, p["q_col"])
    return y_fox, y_dsa, y_mem


def kernel(x, mem, w_in, b_forget, w_mem_kv, w_out, ln_gain, ln_bias):
    batch, seq, d_model = x.shape
    depth = w_in.shape[0]
    n_mem = mem.shape[1]
    m = batch * seq
    alpha = (2 * depth) ** 0.25
    assert seq % (2 * SEQ_TILE) == 0 and min(INDEX_TOPK, seq // 4) <= SEQ_TILE

    p = _prepare(seq, d_model, w_in, b_forget, w_mem_kv, w_out)
    mem_b = mem.reshape(batch * n_mem, d_model).astype(BF16)
    xf = x.reshape(m, d_model).astype(F32)
    xb = xf.astype(BF16)
    for l in range(depth):
        y_fox, y_dsa, y_mem = _mixers(xb, mem_b, l, p, batch, seq, n_mem)
        xf, xb = _out_ln(y_fox, y_dsa, y_mem, p["w_out"][l], xf, ln_gain[l][None, :].astype(F32),
                         ln_bias[l][None, :].astype(F32), alpha)
    return xf.reshape(batch, seq, d_model).astype(x.dtype)
```

```python
import functools

import numpy as np
import jax
import jax.numpy as jnp
from jax import lax
from jax.experimental import pallas as pl
from jax.experimental.pallas import tpu as pltpu

HEAD_DIM = 128
H_FOX = 6
H_DSA = 6
H_MEM = 4
D_FOX = H_FOX * HEAD_DIM
D_DSA = H_DSA * HEAD_DIM
D_MEMG = H_MEM * HEAD_DIM
IDX_HEADS = 16
IDX_DIM = 64
INDEX_TOPK = 256
ROPE_THETA = 10000.0
LN_EPS = 1e-5
NEG = -1e30
M_INIT = -1e29
LOG2E = float(np.log2(np.e))

SEQ_TILE = 256
VMEM_LIMIT_BYTES = 56 * 1024 * 1024
OFF_IW = IDX_DIM
OFF_FL = IDX_DIM + IDX_HEADS
BIAS_PIECES = 3
ONES_ROWS = 16
V_ROWS = HEAD_DIM + ONES_ROWS

F32 = jnp.float32
BF16 = jnp.bfloat16


def _sortable_key(bits):
    return bits ^ ((bits >> 31) & 0x7FFFFFFF)


_NEG_BITS = int(np.array(NEG, np.float32).view(np.int32))
KEY_NEG = int(np.int32(_NEG_BITS) ^ np.int32(0x7FFFFFFF))
INT_MIN = -(2 ** 31)


def _cparams(sem, vmem=None):
    return pltpu.CompilerParams(dimension_semantics=sem, vmem_limit_bytes=vmem)


def _silu(g):
    return g / (1.0 + jnp.exp(-g))


def _split_bf16(x, pieces):
    out = []
    r = x
    for _ in range(pieces - 1):
        p = r.astype(BF16)
        out.append(p)
        r = r - p.astype(F32)
    out.append(r.astype(BF16))
    return out


def _proj_plain_kernel(x_ref, w_ref, o_ref):
    o_ref[...] = jnp.dot(x_ref[...], w_ref[...], preferred_element_type=F32).astype(o_ref.dtype)


def _proj_plain(xb, w, tm, tn, name):
    m, d = xb.shape
    n = w.shape[1]
    return pl.pallas_call(
        _proj_plain_kernel,
        out_shape=jax.ShapeDtypeStruct((m, n), BF16),
        grid=(n // tn, m // tm),
        in_specs=[pl.BlockSpec((tm, d), lambda j, i: (i, 0)),
                  pl.BlockSpec((d, tn), lambda j, i: (0, j))],
        out_specs=pl.BlockSpec((tm, tn), lambda j, i: (i, j)),
        compiler_params=_cparams(("parallel", "parallel"), VMEM_LIMIT_BYTES),
        name=name,
    )(xb, w)


def _proj_rope_kernel(x_ref, w_ref, cos_ref, sin_ref, o_ref, *, heads):
    y = jnp.dot(x_ref[...], w_ref[...], preferred_element_type=F32)
    c = cos_ref[...]
    s = sin_ref[...]
    for h in range(heads):
        yh = y[:, h * HEAD_DIM:(h + 1) * HEAD_DIM]
        rot = pltpu.roll(yh, HEAD_DIM // 2, axis=1)
        o_ref[:, h * HEAD_DIM:(h + 1) * HEAD_DIM] = (yh * c + rot * s).astype(o_ref.dtype)


def _proj_rope(xb, w, cos128, sin128, seq, tm, name):
    m, d = xb.shape
    n = w.shape[1]
    nseq = seq // tm
    return pl.pallas_call(
        functools.partial(_proj_rope_kernel, heads=n // HEAD_DIM),
        out_shape=jax.ShapeDtypeStruct((m, n), BF16),
        grid=(m // tm,),
        in_specs=[pl.BlockSpec((tm, d), lambda i: (i, 0)),
                  pl.BlockSpec((d, n), lambda i: (0, 0)),
                  pl.BlockSpec((tm, HEAD_DIM), lambda i: (i % nseq, 0)),
                  pl.BlockSpec((tm, HEAD_DIM), lambda i: (i % nseq, 0))],
        out_specs=pl.BlockSpec((tm, n), lambda i: (i, 0)),
        compiler_params=_cparams(("parallel",), VMEM_LIMIT_BYTES),
        name=name,
    )(xb, w, cos128, sin128)


def _proj_t_kernel(x_ref, wt_ref, cos64_ref, sin64_ref, cos32_ref, sin32_ref, *out_refs, groups, scale):
    yt = lax.dot_general(wt_ref[...], x_ref[...], (((1,), (1,)), ((), ())), preferred_element_type=F32)
    base = 0
    for (kind, rows), o_ref in zip(groups, out_refs):
        y = yt[base:base + rows]
        base += rows
        if kind == "values":
            ones = jnp.ones((ONES_ROWS, y.shape[1]), o_ref.dtype)
            for h in range(rows // HEAD_DIM):
                o_ref[0, 0, h * V_ROWS:h * V_ROWS + HEAD_DIM, :] = y[h * HEAD_DIM:(h + 1) * HEAD_DIM].astype(o_ref.dtype)
                o_ref[0, 0, h * V_ROWS + HEAD_DIM:(h + 1) * V_ROWS, :] = ones
        elif kind == "scaled":
            o_ref[0, 0] = (y * scale).astype(o_ref.dtype)
        else:
            dim, cos_t, sin_t, mul = ((HEAD_DIM, cos64_ref[...], sin64_ref[...], scale) if kind == "rope_head_scaled"
                                      else (IDX_DIM, cos32_ref[...], sin32_ref[...], 1.0))
            half = dim // 2
            for h in range(rows // dim):
                x1 = y[h * dim:h * dim + half]
                x2 = y[h * dim + half:(h + 1) * dim]
                o_ref[0, 0, h * dim:h * dim + half, :] = ((x1 * cos_t - x2 * sin_t) * mul).astype(o_ref.dtype)
                o_ref[0, 0, h * dim + half:(h + 1) * dim, :] = ((x2 * cos_t + x1 * sin_t) * mul).astype(o_ref.dtype)


def _proj_t(xb, wt, tables, batch, seq, scale, groups, name):
    m, d = xb.shape
    ts = SEQ_TILE
    nb = seq // ts
    rows = wt.shape[0]
    assert rows == sum(r for _, r in groups)
    out_rows = [r // HEAD_DIM * V_ROWS if kind == "values" else r for kind, r in groups]
    blk = lambda r: pl.BlockSpec((1, 1, r, ts), lambda i: (i // nb, i % nb, 0, 0))
    tab = lambda r: pl.BlockSpec((r, ts), lambda i: (0, i % nb))
    return pl.pallas_call(
        functools.partial(_proj_t_kernel, groups=groups, scale=scale),
        out_shape=tuple(jax.ShapeDtypeStruct((batch, nb, r, ts), BF16) for r in out_rows),
        grid=(m // ts,),
        in_specs=[pl.BlockSpec((ts, d), lambda i: (i, 0)),
                  pl.BlockSpec((rows, d), lambda i: (0, 0)),
                  tab(HEAD_DIM // 2), tab(HEAD_DIM // 2), tab(IDX_DIM // 2), tab(IDX_DIM // 2)],
        out_specs=tuple(blk(r) for r in out_rows),
        compiler_params=_cparams(("parallel",), VMEM_LIMIT_BYTES),
        name=name,
    )(xb, wt, *tables)


def _prep_kernel(x_ref, w_ref, bf_ref, cos_ref, sin_ref, tri_ref, ik_ref, iwt_ref, cb_ref, carry_ref, *, w_scale):
    j = pl.program_id(1)
    raw = jnp.dot(x_ref[...], w_ref[...], preferred_element_type=F32)
    ts = raw.shape[0]

    lane = lax.broadcasted_iota(jnp.int32, raw.shape, 1)
    q = IDX_DIM // 2
    rot = jnp.where(lane < q, -pltpu.roll(raw, 128 - q, axis=1), pltpu.roll(raw, q, axis=1))
    ik = raw * cos_ref[...] + rot * sin_ref[...]
    ik_ref[0] = ik[:, :IDX_DIM].astype(ik_ref.dtype)

    iwt_ref[0, 0] = raw.T[OFF_IW:OFF_IW + IDX_HEADS] * w_scale

    z = raw + bf_ref[...]
    logf = jnp.minimum(z, 0.0) - jnp.log(1.0 + jnp.exp(-jnp.abs(z)))
    tri = tri_ref[...]
    cs = sum(jnp.dot(tri, p, preferred_element_type=F32) for p in _split_bf16(logf, 3))

    @pl.when(j == 0)
    def _():
        carry_ref[...] = jnp.zeros_like(carry_ref)

    cs = cs + carry_ref[...]
    carry_ref[...] = cs[ts - 1:ts, :]

    r = lax.broadcasted_iota(jnp.int32, (128, 128), 0)
    c = lax.broadcasted_iota(jnp.int32, (128, 128), 1)
    head_row = (r >= OFF_FL) & (r < OFF_FL + H_FOX)
    cb = jnp.zeros((ts, 128), F32)
    for k, piece in enumerate(_split_bf16(-LOG2E * cs, BIAS_PIECES)):
        place = (head_row & (c == BIAS_PIECES * (r - OFF_FL) + k)).astype(BF16)
        cb = cb + jnp.dot(piece, place, preferred_element_type=F32)
    cb_ref[0] = cb.astype(cb_ref.dtype)


def _prep(xb, w_small, bf_row, cos_ik, sin_ik, tri, batch, seq, w_scale):
    m, d = xb.shape
    ts = SEQ_TILE
    nb = seq // ts
    out_shapes = (jax.ShapeDtypeStruct((batch, seq, IDX_DIM), BF16),
                  jax.ShapeDtypeStruct((batch, nb, IDX_HEADS, ts), F32),
                  jax.ShapeDtypeStruct((batch, seq, 128), BF16))
    return pl.pallas_call(
        functools.partial(_prep_kernel, w_scale=w_scale),
        out_shape=out_shapes,
        grid=(batch, nb),
        in_specs=[pl.BlockSpec((ts, d), lambda b, j: (b * nb + j, 0)),
                  pl.BlockSpec((d, 128), lambda b, j: (0, 0)),
                  pl.BlockSpec((1, 128), lambda b, j: (0, 0)),
                  pl.BlockSpec((ts, 128), lambda b, j: (j, 0)),
                  pl.BlockSpec((ts, 128), lambda b, j: (j, 0)),
                  pl.BlockSpec((ts, ts), lambda b, j: (0, 0))],
        out_specs=(pl.BlockSpec((1, ts, IDX_DIM), lambda b, j: (b, j, 0)),
                   pl.BlockSpec((1, 1, IDX_HEADS, ts), lambda b, j: (b, j, 0, 0)),
                   pl.BlockSpec((1, ts, 128), lambda b, j: (b, j, 0))),
        scratch_shapes=[pltpu.VMEM((1, 128), F32)],
        compiler_params=_cparams(("parallel", "arbitrary")),
        name="prep_small",
    )(xb, w_small, bf_row, cos_ik, sin_ik, tri)


def _attention_scratch(heads, tq):
    return [pltpu.VMEM((heads, 1, tq), F32),
            pltpu.VMEM((heads, 1, tq), F32),
            pltpu.VMEM((heads, HEAD_DIM, tq), F32),
            pltpu.VMEM((heads, 1, tq), F32),
            pltpu.VMEM((heads, tq, tq), BF16)]


def _attention_core(heads, qi, score_fn, values_fn, diag_mask, g_ref, o_ref,
                    m_ref, l_ref, acc_ref, a_ref, p_ref):
    m_ref[...] = jnp.full(m_ref.shape, M_INIT, F32)
    l_ref[...] = jnp.zeros(l_ref.shape, F32)
    acc_ref[...] = jnp.zeros(acc_ref.shape, F32)
    a_ref[...] = jnp.ones(a_ref.shape, F32)
    p_ref[...] = jnp.zeros(p_ref.shape, BF16)

    def values(kb, h):
        pv = jnp.dot(values_fn(kb, h), p_ref[h], preferred_element_type=F32)
        a = a_ref[h]
        acc_ref[h] = a * acc_ref[h] + pv[:HEAD_DIM]
        l_ref[h] = a * l_ref[h] + pv[HEAD_DIM:HEAD_DIM + 1]

    def score_softmax(kb, h, last):
        s = score_fn(kb, h)
        if last and diag_mask is not None:
            s = diag_mask(s)
        m = m_ref[h]
        m_new = jnp.maximum(m, jnp.max(s, axis=0, keepdims=True))
        a = jnp.exp2(m - m_new)
        p = jnp.exp2(s - m_new)
        m_ref[h] = m_new
        a_ref[h] = a
        p_ref[h] = p.astype(BF16)

    def block(kb, last):
        prev = jnp.maximum(kb - 1, 0)
        for h in range(heads):
            values(prev, h)
            score_softmax(kb, h, last)

    def step(kb, carry):
        block(kb, False)
        return carry

    lax.fori_loop(0, qi, step, 0)
    block(qi, True)
    for h in range(heads):
        values(qi, h)
    for h in range(heads):
        hs = slice(h * HEAD_DIM, (h + 1) * HEAD_DIM)
        o = (acc_ref[h] / l_ref[h]).T
        o_ref[:, hs] = (o * _silu(g_ref[:, hs].astype(F32))).astype(o_ref.dtype)


def _fox_kernel(k_ref, cb_ref, qt_ref, vt_ref, g_ref, o_ref, qa_ref, *scratch):
    qi = pl.program_id(1)
    tq = o_ref.shape[0]
    tk = tq
    row = lax.broadcasted_iota(jnp.int32, (tk, tq), 0)
    col = lax.broadcasted_iota(jnp.int32, (tk, tq), 1)
    sel_row = lax.broadcasted_iota(jnp.int32, (HEAD_DIM, tq), 0)
    for h in range(H_FOX):
        qa_ref[h, 0:HEAD_DIM, :] = qt_ref[0, 0, h * HEAD_DIM:(h + 1) * HEAD_DIM, :]
        pick = (sel_row >= BIAS_PIECES * h) & (sel_row < BIAS_PIECES * (h + 1))
        qa_ref[h, HEAD_DIM:2 * HEAD_DIM, :] = pick.astype(BF16)

    def score_fn(kb, h):
        start = pl.multiple_of(kb * tk, tk)
        keys = jnp.concatenate([k_ref[pl.ds(start, tk), h * HEAD_DIM:(h + 1) * HEAD_DIM],
                                cb_ref[0, pl.ds(start, tk), :]], axis=1)
        return jnp.dot(keys, qa_ref[h], preferred_element_type=F32)

    def values_fn(kb, h):
        return vt_ref[0, kb, h * V_ROWS:(h + 1) * V_ROWS, :]

    def diag_mask(s):
        return jnp.where(row <= col, s, NEG)

    _attention_core(H_FOX, qi, score_fn, values_fn, diag_mask, g_ref, o_ref, *scratch)


def _fox_attention(h_plain, cb, fqt, fvt, batch, seq):
    tq = SEQ_TILE
    nb = seq // tq
    m = batch * seq
    return pl.pallas_call(
        _fox_kernel,
        out_shape=jax.ShapeDtypeStruct((m, D_FOX), BF16),
        grid=(batch, nb),
        in_specs=[pl.BlockSpec((seq, D_FOX), lambda b, i: (b, 0)),
                  pl.BlockSpec((1, seq, 128), lambda b, i: (b, 0, 0)),
                  pl.BlockSpec((1, 1, D_FOX, tq), lambda b, i: (b, i, 0, 0)),
                  pl.BlockSpec((1, nb, H_FOX * V_ROWS, tq), lambda b, i: (b, 0, 0, 0)),
                  pl.BlockSpec((tq, D_FOX), lambda b, i: (b * nb + i, 1))],
        out_specs=pl.BlockSpec((tq, D_FOX), lambda b, i: (b * nb + i, 0)),
        scratch_shapes=[pltpu.VMEM((H_FOX, 2 * HEAD_DIM, tq), BF16)] + _attention_scratch(H_FOX, tq),
        compiler_params=_cparams(("parallel", "parallel"), VMEM_LIMIT_BYTES),
        name="fox_attention",
    )(h_plain, cb, fqt, fvt, h_plain)


def _dsa_kernel(dk_ref, dvt_ref, ik_ref, dqt_ref, iqt_ref, iwt_ref, g_ref, o_ref, keys_ref, *scratch, topk):
    qi = pl.program_id(1)
    tq = o_ref.shape[0]
    tk = tq
    nkb = qi + 1
    row = lax.broadcasted_iota(jnp.int32, (tk, tq), 0)
    col = lax.broadcasted_iota(jnp.int32, (tk, tq), 1)

    def score_block(kb, carry):
        start = pl.multiple_of(kb * tk, tk)
        ikb = ik_ref[0, pl.ds(start, tk), :]
        acc = jnp.zeros((tk, tq), F32)
        for h in range(IDX_HEADS):
            z = jnp.dot(ikb, iqt_ref[0, 0, h * IDX_DIM:(h + 1) * IDX_DIM, :], preferred_element_type=F32)
            acc = acc + jnp.maximum(z, 0.0) * iwt_ref[0, 0, h:h + 1, :]
        key = _sortable_key(lax.bitcast_convert_type(acc, jnp.int32))
        key = jnp.where((kb == qi) & (row > col), KEY_NEG, key)
        keys_ref[kb] = key
        return carry

    lax.fori_loop(0, nkb, score_block, 0)

    def count(pred):
        def body(kb, acc):
            hit = pred(keys_ref[kb], row + kb * tk).astype(jnp.int32)
            return acc + jnp.sum(hit.reshape(tk // 8, 8, tq), axis=0)
        part = lax.fori_loop(0, nkb, body, jnp.zeros((8, tq), jnp.int32))
        return jnp.sum(part, axis=0, keepdims=True)

    n_nonneg = count(lambda k, _: k >= 0)
    prefix0 = jnp.where(n_nonneg >= topk, 0, INT_MIN).astype(jnp.int32)

    def bit_step(i, prefix):
        cand = prefix | (jnp.int32(1) << (30 - i))
        n = count(lambda k, _: k >= cand)
        return jnp.where(n >= topk, cand, prefix)

    vstar = lax.fori_loop(0, 31, bit_step, prefix0)

    n_ge = count(lambda k, _: k >= vstar)
    n_gt = count(lambda k, _: k > vstar)
    tie = jnp.max(((n_ge > topk) & (vstar > KEY_NEG)).astype(jnp.int32)) > 0
    big = jnp.full((1, tq), 2 ** 30, jnp.int32)

    def tie_index(_):
        def idx_step(i, jx):
            cand = jx | (jnp.int32(1) << (29 - i))
            n = n_gt + count(lambda k, idx: (k == vstar) & (idx < cand))
            return jnp.where(n <= topk, cand, jx)
        return lax.fori_loop(0, 30, idx_step, jnp.zeros((1, tq), jnp.int32))

    jx = lax.cond(tie, tie_index, lambda _: big, 0)

    def to_bias(kb, carry):
        k = keys_ref[kb]
        sel = ((k > vstar) | ((k == vstar) & ((row + kb * tk) < jx))) & (k > KEY_NEG)
        keys_ref[kb] = lax.bitcast_convert_type(jnp.where(sel, 0.0, NEG).astype(F32), jnp.int32)
        return carry

    lax.fori_loop(0, nkb, to_bias, 0)

    def score_fn(kb, h):
        start = pl.multiple_of(kb * tk, tk)
        hs = slice(h * HEAD_DIM, (h + 1) * HEAD_DIM)
        s = jnp.dot(dk_ref[pl.ds(start, tk), hs], dqt_ref[0, 0, hs, :], preferred_element_type=F32)
        return s + lax.bitcast_convert_type(keys_ref[kb], F32)

    def values_fn(kb, h):
        return dvt_ref[0, kb, h * V_ROWS:(h + 1) * V_ROWS, :]

    _attention_core(H_DSA, qi, score_fn, values_fn, None, g_ref, o_ref, *scratch)


def _dsa_attention(dk, dvt, ik, dqt, iqt, iwt, h_plain, batch, seq, topk):
    tq = SEQ_TILE
    nb = seq // tq
    m = batch * seq
    return pl.pallas_call(
        functools.partial(_dsa_kernel, topk=topk),
        out_shape=jax.ShapeDtypeStruct((m, D_DSA), BF16),
        grid=(batch, nb),
        in_specs=[pl.BlockSpec((seq, D_DSA), lambda b, i: (b, 0)),
                  pl.BlockSpec((1, nb, H_DSA * V_ROWS, tq), lambda b, i: (b, 0, 0, 0)),
                  pl.BlockSpec((1, seq, IDX_DIM), lambda b, i: (b, 0, 0)),
                  pl.BlockSpec((1, 1, D_DSA, tq), lambda b, i: (b, i, 0, 0)),
                  pl.BlockSpec((1, 1, IDX_HEADS * IDX_DIM, tq), lambda b, i: (b, i, 0, 0)),
                  pl.BlockSpec((1, 1, IDX_HEADS, tq), lambda b, i: (b, i, 0, 0)),
                  pl.BlockSpec((tq, D_DSA), lambda b, i: (b * nb + i, 2))],
        out_specs=pl.BlockSpec((tq, D_DSA), lambda b, i: (b * nb + i, 0)),
        scratch_shapes=[pltpu.VMEM((nb, tq, tq), jnp.int32)] + _attention_scratch(H_DSA, tq),
        compiler_params=_cparams(("parallel", "parallel"), VMEM_LIMIT_BYTES),
        name="dsa_attention",
    )(dk, dvt, ik, dqt, iqt, iwt, h_plain)


def _mem_kernel(q0_ref, q1_ref, g0_ref, g1_ref, kv_ref, o_ref, *, scale):
    for h in range(H_MEM):
        q_ref = (q0_ref, q1_ref)[h // 2]
        g_ref = (g0_ref, g1_ref)[h // 2]
        ls = slice((h % 2) * HEAD_DIM, (h % 2 + 1) * HEAD_DIM)
        q = (q_ref[:, ls].astype(F32) * scale).astype(BF16)
        k = kv_ref[0, :, h * HEAD_DIM:(h + 1) * HEAD_DIM]
        v = kv_ref[0, :, D_MEMG + h * HEAD_DIM:D_MEMG + (h + 1) * HEAD_DIM]
        s = lax.dot_general(q, k, (((1,), (1,)), ((), ())), preferred_element_type=F32)
        m = jnp.max(s, axis=1, keepdims=True)
        p = jnp.exp(s - m)
        l = jnp.sum(p, axis=1, keepdims=True)
        o = jnp.dot(p.astype(BF16), v, preferred_element_type=F32) / l
        o_ref[:, h * HEAD_DIM:(h + 1) * HEAD_DIM] = (o * _silu(g_ref[:, ls].astype(F32))).astype(o_ref.dtype)


def _mem_attention(h_plain, kv, batch, seq, scale, q_col):
    tq = 2 * SEQ_TILE
    nb = seq // tq
    m = batch * seq
    n_mem = kv.shape[1]
    cb = q_col // 256
    spec = lambda c: pl.BlockSpec((tq, 256), lambda i: (i, c))
    return pl.pallas_call(
        functools.partial(_mem_kernel, scale=scale),
        out_shape=jax.ShapeDtypeStruct((m, D_MEMG), BF16),
        grid=(m // tq,),
        in_specs=[spec(cb), spec(cb + 1), spec(cb + 2), spec(cb + 3),
                  pl.BlockSpec((1, n_mem, 2 * D_MEMG), lambda i: (i // nb, 0, 0))],
        out_specs=pl.BlockSpec((tq, D_MEMG), lambda i: (i, 0)),
        compiler_params=_cparams(("parallel",)),
        name="mem_attention",
    )(h_plain, h_plain, h_plain, h_plain, kv)


def _out_ln_kernel(yf_ref, yd_ref, ym_ref, w_ref, x_ref, g_ref, b_ref, xo_ref, xb_ref, *, alpha):
    y = jnp.dot(yf_ref[...], w_ref[0:D_FOX, :], preferred_element_type=F32)
    y = y + jnp.dot(yd_ref[...], w_ref[D_FOX:D_FOX + D_DSA, :], preferred_element_type=F32)
    y = y + jnp.dot(ym_ref[...], w_ref[D_FOX + D_DSA:, :], preferred_element_type=F32)
    z = alpha * x_ref[...] + y
    mu = jnp.mean(z, axis=1, keepdims=True)
    zc = z - mu
    var = jnp.mean(zc * zc, axis=1, keepdims=True)
    out = zc * lax.rsqrt(var + LN_EPS) * g_ref[...] + b_ref[...]
    xo_ref[...] = out
    xb_ref[...] = out.astype(xb_ref.dtype)


def _out_ln(yf, yd, ym, w_out, x, gain, bias, alpha):
    m, d = x.shape
    tm = SEQ_TILE
    row = lambda n: pl.BlockSpec((tm, n), lambda i: (i, 0))
    const = lambda r, c: pl.BlockSpec((r, c), lambda i: (0, 0))
    return pl.pallas_call(
        functools.partial(_out_ln_kernel, alpha=alpha),
        out_shape=(jax.ShapeDtypeStruct((m, d), F32), jax.ShapeDtypeStruct((m, d), BF16)),
        grid=(m // tm,),
        in_specs=[row(D_FOX), row(D_DSA), row(D_MEMG), const(w_out.shape[0], d), row(d), const(1, d), const(1, d)],
        out_specs=(row(d), row(d)),
        compiler_params=_cparams(("parallel",), VMEM_LIMIT_BYTES),
        name="out_proj_layernorm",
    )(yf, yd, ym, w_out, x, gain, bias)


def _rope_tables(seq):
    pos = jnp.arange(seq).astype(F32)

    def tables(half):
        inv_freq = ROPE_THETA ** (-jnp.arange(half, dtype=F32) / half)
        ang = pos[:, None] * inv_freq[None, :]
        return jnp.cos(ang), jnp.sin(ang)

    c64, s64 = tables(HEAD_DIM // 2)
    c32, s32 = tables(IDX_DIM // 2)
    cos128 = jnp.concatenate([c64, c64], axis=1)
    sin128 = jnp.concatenate([-s64, s64], axis=1)
    zeros = jnp.zeros((seq, 128 - IDX_DIM), F32)
    cos_ik = jnp.concatenate([c32, c32, zeros], axis=1)
    sin_ik = jnp.concatenate([s32, s32, zeros], axis=1)
    return cos128, sin128, (c64.T, s64.T, c32.T, s32.T), cos_ik, sin_ik


def _prepare(seq, d_model, w_in, b_forget, w_mem_kv, w_out):
    depth = w_in.shape[0]
    splits = (D_FOX, D_FOX, D_FOX, D_FOX, H_FOX, D_DSA, D_DSA, D_DSA, D_DSA,
              IDX_HEADS * IDX_DIM, IDX_DIM, IDX_HEADS, D_MEMG, D_MEMG)
    offs = np.concatenate([[0], np.cumsum(splits)]).tolist()
    seg = lambda i: w_in[:, :, offs[i]:offs[i + 1]]
    (w_fq, w_fk, w_fv, w_fg, w_fl, w_dq, w_dk, w_dv, w_dg, w_iq, w_ik, w_iw, w_mq, w_mg) = [seg(i) for i in range(14)]
    pad = jnp.zeros((depth, d_model, 128 - IDX_DIM - IDX_HEADS - H_FOX), w_in.dtype)
    transposed = lambda ws: jnp.swapaxes(jnp.concatenate([w.astype(BF16) for w in ws], axis=2), 1, 2)
    cos128, sin128, t_tables, cos_ik, sin_ik = _rope_tables(seq)
    ts = SEQ_TILE
    return dict(
        w_plain=jnp.concatenate([w_fk, w_fg, w_dg, w_mq, w_mg], axis=2).astype(BF16),
        q_col=2 * D_FOX + D_DSA,
        w_dk=w_dk.astype(BF16),
        w_t_dsa=transposed([w_dq, w_dv, w_iq]),
        w_t_fox=transposed([w_fq, w_fv]),
        w_small=jnp.concatenate([w_ik, w_iw, w_fl, pad], axis=2).astype(BF16),
        w_out=w_out.astype(BF16),
        w_mem=w_mem_kv.astype(BF16),
        bf_rows=jnp.zeros((depth, 1, 128), F32).at[:, 0, OFF_FL:OFF_FL + H_FOX].set(b_forget.astype(F32)),
        cos128=cos128, sin128=sin128, t_tables=t_tables, cos_ik=cos_ik, sin_ik=sin_ik,
        tri=(jnp.arange(ts)[:, None] >= jnp.arange(ts)[None, :]).astype(BF16),
    )


def _mixers(xb, mem_b, l, p, batch, seq, n_mem):
    scale = HEAD_DIM ** -0.5
    topk = min(INDEX_TOPK, seq // 4)
    w_scale = (IDX_HEADS ** -0.5) * (IDX_DIM ** -0.5)
    kv = _proj_plain(mem_b, p["w_mem"][l], batch * n_mem, 2 * D_MEMG, "proj_mem_kv")
    kv = kv.reshape(batch, n_mem, 2 * D_MEMG)
    h_plain = _proj_plain(xb, p["w_plain"][l], 2 * SEQ_TILE, p["w_plain"].shape[2] // 2, "proj_plain")
    dk = _proj_rope(xb, p["w_dk"][l], p["cos128"], p["sin128"], seq, 2 * SEQ_TILE, "proj_dsa_key")
    dqt, dvt, iqt = _proj_t(xb, p["w_t_dsa"][l], p["t_tables"], batch, seq, scale * LOG2E,
                            (("rope_head_scaled", D_DSA), ("values", D_DSA), ("rope_index", IDX_HEADS * IDX_DIM)),
                            "proj_transposed_dsa")
    fqt, fvt = _proj_t(xb, p["w_t_fox"][l], p["t_tables"], batch, seq, scale * LOG2E,
                       (("scaled", D_FOX), ("values", D_FOX)), "proj_transposed_fox")
    ik, iwt, cb = _prep(xb, p["w_small"][l], p["bf_rows"][l], p["cos_ik"], p["sin_ik"], p["tri"], batch, seq, w_scale)
    y_fox = _fox_attention(h_plain, cb, fqt, fvt, batch, seq)
    y_dsa = _dsa_attention(dk, dvt, ik, dqt, iqt, iwt, h_plain, batch, seq, topk)
    y_mem = _mem_attention(h_plain, kv, batch, seq, scale, p["q_col"])
    return y_fox, y_dsa, y_mem


def kernel(x, mem, w_in, b_forget, w_mem_kv, w_out, ln_gain, ln_bias):
    batch, seq, d_model = x.shape
    depth = w_in.shape[0]
    n_mem = mem.shape[1]
    m = batch * seq
    alpha = (2 * depth) ** 0.25
    assert seq % (2 * SEQ_TILE) == 0 and min(INDEX_TOPK, seq // 4) <= SEQ_TILE

    p = _prepare(seq, d_model, w_in, b_forget, w_mem_kv, w_out)
    mem_b = mem.reshape(batch * n_mem, d_model).astype(BF16)
    xf = x.reshape(m, d_model).astype(F32)
    xb = xf.astype(BF16)
    for l in range(depth):
        y_fox, y_dsa, y_mem = _mixers(xb, mem_b, l, p, batch, seq, n_mem)
        xf, xb = _out_ln(y_fox, y_dsa, y_mem, p["w_out"][l], xf, ln_gain[l][None, :].astype(F32),
                         ln_bias[l][None, :].astype(F32), alpha)
    return xf.reshape(batch, seq, d_model).astype(x.dtype)
```

```python
import functools

import numpy as np
import jax
import jax.numpy as jnp
from jax import lax
from jax.experimental import pallas as pl
from jax.experimental.pallas import tpu as pltpu

HEAD_DIM = 128
H_FOX = 6
H_DSA = 6
H_MEM = 4
D_FOX = H_FOX * HEAD_DIM
D_DSA = H_DSA * HEAD_DIM
D_MEMG = H_MEM * HEAD_DIM
IDX_HEADS = 16
IDX_DIM = 64
INDEX_TOPK = 256
ROPE_THETA = 10000.0
LN_EPS = 1e-5
NEG = -1e30
M_INIT = -1e29
LOG2E = float(np.log2(np.e))

SEQ_TILE = 256
VMEM_LIMIT_BYTES = 56 * 1024 * 1024
OFF_IW = IDX_DIM
OFF_FL = IDX_DIM + IDX_HEADS
BIAS_PIECES = 3
ONES_ROWS = 16
V_ROWS = HEAD_DIM + ONES_ROWS

F32 = jnp.float32
BF16 = jnp.bfloat16


def _sortable_key(bits):
    return bits ^ ((bits >> 31) & 0x7FFFFFFF)


_NEG_BITS = int(np.array(NEG, np.float32).view(np.int32))
KEY_NEG = int(np.int32(_NEG_BITS) ^ np.int32(0x7FFFFFFF))
INT_MIN = -(2 ** 31)


def _cparams(sem, vmem=None):
    return pltpu.CompilerParams(dimension_semantics=sem, vmem_limit_bytes=vmem)


def _silu(g):
    return g / (1.0 + jnp.exp(-g))


COUNT_CHAINS = 4


def _partial_sums(x):
    rows, lanes = x.shape
    grouped = x.reshape(rows // (COUNT_CHAINS * 8), COUNT_CHAINS * 8, lanes)
    return jnp.sum(grouped, axis=0)


def _split_bf16(x, pieces):
    out = []
    r = x
    for _ in range(pieces - 1):
        p = r.astype(BF16)
        out.append(p)
        r = r - p.astype(F32)
    out.append(r.astype(BF16))
    return out


def _proj_plain_kernel(x_ref, w_ref, o_ref):
    o_ref[...] = jnp.dot(x_ref[...], w_ref[...], preferred_element_type=F32).astype(o_ref.dtype)


def _proj_plain(xb, w, tm, tn, name):
    m, d = xb.shape
    n = w.shape[1]
    return pl.pallas_call(
        _proj_plain_kernel,
        out_shape=jax.ShapeDtypeStruct((m, n), BF16),
        grid=(n // tn, m // tm),
        in_specs=[pl.BlockSpec((tm, d), lambda j, i: (i, 0)),
                  pl.BlockSpec((d, tn), lambda j, i: (0, j))],
        out_specs=pl.BlockSpec((tm, tn), lambda j, i: (i, j)),
        compiler_params=_cparams(("parallel", "parallel"), VMEM_LIMIT_BYTES),
        name=name,
    )(xb, w)


def _proj_rope_kernel(x_ref, w_ref, cos_ref, sin_ref, o_ref, *, heads):
    y = jnp.dot(x_ref[...], w_ref[...], preferred_element_type=F32)
    c = cos_ref[...]
    s = sin_ref[...]
    for h in range(heads):
        yh = y[:, h * HEAD_DIM:(h + 1) * HEAD_DIM]
        rot = pltpu.roll(yh, HEAD_DIM // 2, axis=1)
        o_ref[:, h * HEAD_DIM:(h + 1) * HEAD_DIM] = (yh * c + rot * s).astype(o_ref.dtype)


def _proj_rope(xb, w, cos128, sin128, seq, tm, name):
    m, d = xb.shape
    n = w.shape[1]
    nseq = seq // tm
    return pl.pallas_call(
        functools.partial(_proj_rope_kernel, heads=n // HEAD_DIM),
        out_shape=jax.ShapeDtypeStruct((m, n), BF16),
        grid=(m // tm,),
        in_specs=[pl.BlockSpec((tm, d), lambda i: (i, 0)),
                  pl.BlockSpec((d, n), lambda i: (0, 0)),
                  pl.BlockSpec((tm, HEAD_DIM), lambda i: (i % nseq, 0)),
                  pl.BlockSpec((tm, HEAD_DIM), lambda i: (i % nseq, 0))],
        out_specs=pl.BlockSpec((tm, n), lambda i: (i, 0)),
        compiler_params=_cparams(("parallel",), VMEM_LIMIT_BYTES),
        name=name,
    )(xb, w, cos128, sin128)


def _proj_t_kernel(x_ref, wt_ref, cos64_ref, sin64_ref, cos32_ref, sin32_ref, *out_refs, groups, scale):
    yt = lax.dot_general(wt_ref[...], x_ref[...], (((1,), (1,)), ((), ())), preferred_element_type=F32)
    base = 0
    for (kind, rows), o_ref in zip(groups, out_refs):
        y = yt[base:base + rows]
        base += rows
        if kind == "values":
            ones = jnp.ones((ONES_ROWS, y.shape[1]), o_ref.dtype)
            for h in range(rows // HEAD_DIM):
                o_ref[0, 0, h * V_ROWS:h * V_ROWS + HEAD_DIM, :] = y[h * HEAD_DIM:(h + 1) * HEAD_DIM].astype(o_ref.dtype)
                o_ref[0, 0, h * V_ROWS + HEAD_DIM:(h + 1) * V_ROWS, :] = ones
        elif kind == "scaled":
            o_ref[0, 0] = (y * scale).astype(o_ref.dtype)
        else:
            dim, cos_t, sin_t, mul = ((HEAD_DIM, cos64_ref[...], sin64_ref[...], scale) if kind == "rope_head_scaled"
                                      else (IDX_DIM, cos32_ref[...], sin32_ref[...], 1.0))
            half = dim // 2
            for h in range(rows // dim):
                x1 = y[h * dim:h * dim + half]
                x2 = y[h * dim + half:(h + 1) * dim]
                o_ref[0, 0, h * dim:h * dim + half, :] = ((x1 * cos_t - x2 * sin_t) * mul).astype(o_ref.dtype)
                o_ref[0, 0, h * dim + half:(h + 1) * dim, :] = ((x2 * cos_t + x1 * sin_t) * mul).astype(o_ref.dtype)


def _proj_t(xb, wt, tables, batch, seq, scale, groups, name):
    m, d = xb.shape
    ts = SEQ_TILE
    nb = seq // ts
    rows = wt.shape[0]
    assert rows == sum(r for _, r in groups)
    out_rows = [r // HEAD_DIM * V_ROWS if kind == "values" else r for kind, r in groups]
    blk = lambda r: pl.BlockSpec((1, 1, r, ts), lambda i: (i // nb, i % nb, 0, 0))
    tab = lambda r: pl.BlockSpec((r, ts), lambda i: (0, i % nb))
    return pl.pallas_call(
        functools.partial(_proj_t_kernel, groups=groups, scale=scale),
        out_shape=tuple(jax.ShapeDtypeStruct((batch, nb, r, ts), BF16) for r in out_rows),
        grid=(m // ts,),
        in_specs=[pl.BlockSpec((ts, d), lambda i: (i, 0)),
                  pl.BlockSpec((rows, d), lambda i: (0, 0)),
                  tab(HEAD_DIM // 2), tab(HEAD_DIM // 2), tab(IDX_DIM // 2), tab(IDX_DIM // 2)],
        out_specs=tuple(blk(r) for r in out_rows),
        compiler_params=_cparams(("parallel",), VMEM_LIMIT_BYTES),
        name=name,
    )(xb, wt, *tables)


def _prep_kernel(x_ref, w_ref, bf_ref, cos_ref, sin_ref, tri_ref, ik_ref, iwt_ref, cb_ref, carry_ref, *, w_scale):
    j = pl.program_id(1)
    raw = jnp.dot(x_ref[...], w_ref[...], preferred_element_type=F32)
    ts = raw.shape[0]

    lane = lax.broadcasted_iota(jnp.int32, raw.shape, 1)
    q = IDX_DIM // 2
    rot = jnp.where(lane < q, -pltpu.roll(raw, 128 - q, axis=1), pltpu.roll(raw, q, axis=1))
    ik = raw * cos_ref[...] + rot * sin_ref[...]
    ik_ref[0] = ik[:, :IDX_DIM].astype(ik_ref.dtype)

    iwt_ref[0, 0] = raw.T[OFF_IW:OFF_IW + IDX_HEADS] * w_scale

    z = raw + bf_ref[...]
    logf = jnp.minimum(z, 0.0) - jnp.log(1.0 + jnp.exp(-jnp.abs(z)))
    tri = tri_ref[...]
    cs = sum(jnp.dot(tri, p, preferred_element_type=F32) for p in _split_bf16(logf, 3))

    @pl.when(j == 0)
    def _():
        carry_ref[...] = jnp.zeros_like(carry_ref)

    cs = cs + carry_ref[...]
    carry_ref[...] = cs[ts - 1:ts, :]

    r = lax.broadcasted_iota(jnp.int32, (128, 128), 0)
    c = lax.broadcasted_iota(jnp.int32, (128, 128), 1)
    head_row = (r >= OFF_FL) & (r < OFF_FL + H_FOX)
    cb = jnp.zeros((ts, 128), F32)
    for k, piece in enumerate(_split_bf16(-LOG2E * cs, BIAS_PIECES)):
        place = (head_row & (c == BIAS_PIECES * (r - OFF_FL) + k)).astype(BF16)
        cb = cb + jnp.dot(piece, place, preferred_element_type=F32)
    cb_ref[0] = cb.astype(cb_ref.dtype)


def _prep(xb, w_small, bf_row, cos_ik, sin_ik, tri, batch, seq, w_scale):
    m, d = xb.shape
    ts = SEQ_TILE
    nb = seq // ts
    out_shapes = (jax.ShapeDtypeStruct((batch, seq, IDX_DIM), BF16),
                  jax.ShapeDtypeStruct((batch, nb, IDX_HEADS, ts), F32),
                  jax.ShapeDtypeStruct((batch, seq, 128), BF16))
    return pl.pallas_call(
        functools.partial(_prep_kernel, w_scale=w_scale),
        out_shape=out_shapes,
        grid=(batch, nb),
        in_specs=[pl.BlockSpec((ts, d), lambda b, j: (b * nb + j, 0)),
                  pl.BlockSpec((d, 128), lambda b, j: (0, 0)),
                  pl.BlockSpec((1, 128), lambda b, j: (0, 0)),
                  pl.BlockSpec((ts, 128), lambda b, j: (j, 0)),
                  pl.BlockSpec((ts, 128), lambda b, j: (j, 0)),
                  pl.BlockSpec((ts, ts), lambda b, j: (0, 0))],
        out_specs=(pl.BlockSpec((1, ts, IDX_DIM), lambda b, j: (b, j, 0)),
                   pl.BlockSpec((1, 1, IDX_HEADS, ts), lambda b, j: (b, j, 0, 0)),
                   pl.BlockSpec((1, ts, 128), lambda b, j: (b, j, 0))),
        scratch_shapes=[pltpu.VMEM((1, 128), F32)],
        compiler_params=_cparams(("parallel", "arbitrary")),
        name="prep_small",
    )(xb, w_small, bf_row, cos_ik, sin_ik, tri)


def _attention_scratch(heads, tq):
    return [pltpu.VMEM((heads, 1, tq), F32),
            pltpu.VMEM((heads, 1, tq), F32),
            pltpu.VMEM((heads, HEAD_DIM, tq), F32),
            pltpu.VMEM((heads, 1, tq), F32),
            pltpu.VMEM((heads, tq, tq), BF16)]


def _attention_core(heads, qi, score_fn, values_fn, diag_mask, g_ref, o_ref,
                    m_ref, l_ref, acc_ref, a_ref, p_ref):
    m_ref[...] = jnp.full(m_ref.shape, M_INIT, F32)
    l_ref[...] = jnp.zeros(l_ref.shape, F32)
    acc_ref[...] = jnp.zeros(acc_ref.shape, F32)
    a_ref[...] = jnp.ones(a_ref.shape, F32)
    p_ref[...] = jnp.zeros(p_ref.shape, BF16)

    def values(kb, h):
        pv = jnp.dot(values_fn(kb, h), p_ref[h], preferred_element_type=F32)
        a = a_ref[h]
        acc_ref[h] = a * acc_ref[h] + pv[:HEAD_DIM]
        l_ref[h] = a * l_ref[h] + pv[HEAD_DIM:HEAD_DIM + 1]

    def score_softmax(kb, h, last):
        s = score_fn(kb, h)
        if last and diag_mask is not None:
            s = diag_mask(s)
        m = m_ref[h]
        m_new = jnp.maximum(m, jnp.max(s, axis=0, keepdims=True))
        a = jnp.exp2(m - m_new)
        p = jnp.exp2(s - m_new)
        m_ref[h] = m_new
        a_ref[h] = a
        p_ref[h] = p.astype(BF16)

    def block(kb, last):
        prev = jnp.maximum(kb - 1, 0)
        for h in range(heads):
            values(prev, h)
            score_softmax(kb, h, last)

    def step(kb, carry):
        block(kb, False)
        return carry

    lax.fori_loop(0, qi, step, 0)
    block(qi, True)
    for h in range(heads):
        values(qi, h)
    for h in range(heads):
        hs = slice(h * HEAD_DIM, (h + 1) * HEAD_DIM)
        o = (acc_ref[h] / l_ref[h]).T
        o_ref[:, hs] = (o * _silu(g_ref[:, hs].astype(F32))).astype(o_ref.dtype)


def _fox_kernel(k_ref, cb_ref, qt_ref, vt_ref, g_ref, o_ref, qa_ref, *scratch):
    qi = pl.program_id(1)
    tq = o_ref.shape[0]
    tk = tq
    row = lax.broadcasted_iota(jnp.int32, (tk, tq), 0)
    col = lax.broadcasted_iota(jnp.int32, (tk, tq), 1)
    sel_row = lax.broadcasted_iota(jnp.int32, (HEAD_DIM, tq), 0)
    for h in range(H_FOX):
        qa_ref[h, 0:HEAD_DIM, :] = qt_ref[0, 0, h * HEAD_DIM:(h + 1) * HEAD_DIM, :]
        pick = (sel_row >= BIAS_PIECES * h) & (sel_row < BIAS_PIECES * (h + 1))
        qa_ref[h, HEAD_DIM:2 * HEAD_DIM, :] = pick.astype(BF16)

    def score_fn(kb, h):
        start = pl.multiple_of(kb * tk, tk)
        keys = jnp.concatenate([k_ref[pl.ds(start, tk), h * HEAD_DIM:(h + 1) * HEAD_DIM],
                                cb_ref[0, pl.ds(start, tk), :]], axis=1)
        return jnp.dot(keys, qa_ref[h], preferred_element_type=F32)

    def values_fn(kb, h):
        return vt_ref[0, kb, h * V_ROWS:(h + 1) * V_ROWS, :]

    def diag_mask(s):
        return jnp.where(row <= col, s, NEG)

    _attention_core(H_FOX, qi, score_fn, values_fn, diag_mask, g_ref, o_ref, *scratch)


def _fox_attention(h_plain, cb, fqt, fvt, batch, seq):
    tq = SEQ_TILE
    nb = seq // tq
    m = batch * seq
    return pl.pallas_call(
        _fox_kernel,
        out_shape=jax.ShapeDtypeStruct((m, D_FOX), BF16),
        grid=(batch, nb),
        in_specs=[pl.BlockSpec((seq, D_FOX), lambda b, i: (b, 0)),
                  pl.BlockSpec((1, seq, 128), lambda b, i: (b, 0, 0)),
                  pl.BlockSpec((1, 1, D_FOX, tq), lambda b, i: (b, i, 0, 0)),
                  pl.BlockSpec((1, nb, H_FOX * V_ROWS, tq), lambda b, i: (b, 0, 0, 0)),
                  pl.BlockSpec((tq, D_FOX), lambda b, i: (b * nb + i, 1))],
        out_specs=pl.BlockSpec((tq, D_FOX), lambda b, i: (b * nb + i, 0)),
        scratch_shapes=[pltpu.VMEM((H_FOX, 2 * HEAD_DIM, tq), BF16)] + _attention_scratch(H_FOX, tq),
        compiler_params=_cparams(("parallel", "parallel"), VMEM_LIMIT_BYTES),
        name="fox_attention",
    )(h_plain, cb, fqt, fvt, h_plain)


def _dsa_kernel(dk_ref, dvt_ref, ik_ref, dqt_ref, iqt_ref, iwt_ref, g_ref, o_ref, keys_ref, *scratch, topk):
    qi = pl.program_id(1)
    tq = o_ref.shape[0]
    tk = tq
    nkb = qi + 1
    row = lax.broadcasted_iota(jnp.int32, (tk, tq), 0)
    col = lax.broadcasted_iota(jnp.int32, (tk, tq), 1)

    def score_block(kb, carry):
        start = pl.multiple_of(kb * tk, tk)
        ikb = ik_ref[0, pl.ds(start, tk), :]
        acc = jnp.zeros((tk, tq), F32)
        for h in range(IDX_HEADS):
            z = jnp.dot(ikb, iqt_ref[0, 0, h * IDX_DIM:(h + 1) * IDX_DIM, :], preferred_element_type=F32)
            acc = acc + jnp.maximum(z, 0.0) * iwt_ref[0, 0, h:h + 1, :]
        key = _sortable_key(lax.bitcast_convert_type(acc, jnp.int32))
        keys_ref[kb] = jnp.where((kb == qi) & (row > col), KEY_NEG, key)
        return carry

    lax.fori_loop(0, nkb, score_block, 0)

    def count(pred):
        def body(kb, acc):
            hit = pred(keys_ref[kb], row + kb * tk).astype(jnp.int32)
            return acc + _partial_sums(hit)
        part = lax.fori_loop(0, nkb, body, jnp.zeros((COUNT_CHAINS * 8, tq), jnp.int32))
        return jnp.sum(part, axis=0, keepdims=True)

    n_nonneg = count(lambda k, _: k >= 0)
    nonneg = n_nonneg >= topk
    prefix0 = jnp.where(nonneg, 0, INT_MIN).astype(jnp.int32)
    n_all = jnp.full((1, tq), tk, jnp.int32) * nkb

    def bit_step(i, carry):
        prefix, n_ge = carry
        cand = prefix | (jnp.int32(1) << (30 - i))
        n = count(lambda k, _: k >= cand)
        keep = n >= topk
        return jnp.where(keep, cand, prefix), jnp.where(keep, n, n_ge)

    vstar, n_ge = lax.fori_loop(0, 31, bit_step, (prefix0, jnp.where(nonneg, n_nonneg, n_all)))

    tie = jnp.max(((n_ge > topk) & (vstar > KEY_NEG)).astype(jnp.int32)) > 0

    def tie_index(_):
        n_gt = count(lambda k, _: k > vstar)

        def idx_step(i, jx):
            cand = jx | (jnp.int32(1) << (29 - i))
            n = n_gt + count(lambda k, idx: (k == vstar) & (idx < cand))
            return jnp.where(n <= topk, cand, jx)
        return lax.fori_loop(0, 30, idx_step, jnp.zeros((1, tq), jnp.int32))

    def store_bias(kb, sel):
        keys_ref[kb] = lax.bitcast_convert_type(jnp.where(sel, 0.0, NEG).astype(F32), jnp.int32)

    def bias_with_ties(_):
        jx = tie_index(0)

        def body(kb, carry):
            k = keys_ref[kb]
            store_bias(kb, ((k > vstar) | ((k == vstar) & ((row + kb * tk) < jx))) & (k > KEY_NEG))
            return carry

        lax.fori_loop(0, nkb, body, 0)
        return 0

    def bias_without_ties(_):
        threshold = jnp.maximum(vstar, KEY_NEG + 1)

        def body(kb, carry):
            store_bias(kb, keys_ref[kb] >= threshold)
            return carry

        lax.fori_loop(0, nkb, body, 0)
        return 0

    lax.cond(tie, bias_with_ties, bias_without_ties, 0)

    def score_fn(kb, h):
        start = pl.multiple_of(kb * tk, tk)
        hs = slice(h * HEAD_DIM, (h + 1) * HEAD_DIM)
        s = jnp.dot(dk_ref[pl.ds(start, tk), hs], dqt_ref[0, 0, hs, :], preferred_element_type=F32)
        return s + lax.bitcast_convert_type(keys_ref[kb], F32)

    def values_fn(kb, h):
        return dvt_ref[0, kb, h * V_ROWS:(h + 1) * V_ROWS, :]

    _attention_core(H_DSA, qi, score_fn, values_fn, None, g_ref, o_ref, *scratch)


def _dsa_attention(dk, dvt, ik, dqt, iqt, iwt, h_plain, batch, seq, topk):
    tq = SEQ_TILE
    nb = seq // tq
    m = batch * seq
    return pl.pallas_call(
        functools.partial(_dsa_kernel, topk=topk),
        out_shape=jax.ShapeDtypeStruct((m, D_DSA), BF16),
        grid=(batch, nb),
        in_specs=[pl.BlockSpec((seq, D_DSA), lambda b, i: (b, 0)),
                  pl.BlockSpec((1, nb, H_DSA * V_ROWS, tq), lambda b, i: (b, 0, 0, 0)),
                  pl.BlockSpec((1, seq, IDX_DIM), lambda b, i: (b, 0, 0)),
                  pl.BlockSpec((1, 1, D_DSA, tq), lambda b, i: (b, i, 0, 0)),
                  pl.BlockSpec((1, 1, IDX_HEADS * IDX_DIM, tq), lambda b, i: (b, i, 0, 0)),
                  pl.BlockSpec((1, 1, IDX_HEADS, tq), lambda b, i: (b, i, 0, 0)),
                  pl.BlockSpec((tq, D_DSA), lambda b, i: (b * nb + i, 2))],
        out_specs=pl.BlockSpec((tq, D_DSA), lambda b, i: (b * nb + i, 0)),
        scratch_shapes=[pltpu.VMEM((nb, tq, tq), jnp.int32)] + _attention_scratch(H_DSA, tq),
        compiler_params=_cparams(("parallel", "parallel"), VMEM_LIMIT_BYTES),
        name="dsa_attention",
    )(dk, dvt, ik, dqt, iqt, iwt, h_plain)


def _mem_kernel(q0_ref, q1_ref, g0_ref, g1_ref, kv_ref, o_ref, *, scale):
    for h in range(H_MEM):
        q_ref = (q0_ref, q1_ref)[h // 2]
        g_ref = (g0_ref, g1_ref)[h // 2]
        ls = slice((h % 2) * HEAD_DIM, (h % 2 + 1) * HEAD_DIM)
        q = (q_ref[:, ls].astype(F32) * scale).astype(BF16)
        k = kv_ref[0, :, h * HEAD_DIM:(h + 1) * HEAD_DIM]
        v = kv_ref[0, :, D_MEMG + h * HEAD_DIM:D_MEMG + (h + 1) * HEAD_DIM]
        s = lax.dot_general(q, k, (((1,), (1,)), ((), ())), preferred_element_type=F32)
        m = jnp.max(s, axis=1, keepdims=True)
        p = jnp.exp(s - m)
        l = jnp.sum(p, axis=1, keepdims=True)
        o = jnp.dot(p.astype(BF16), v, preferred_element_type=F32) / l
        o_ref[:, h * HEAD_DIM:(h + 1) * HEAD_DIM] = (o * _silu(g_ref[:, ls].astype(F32))).astype(o_ref.dtype)


def _mem_attention(h_plain, kv, batch, seq, scale, q_col):
    tq = 2 * SEQ_TILE
    nb = seq // tq
    m = batch * seq
    n_mem = kv.shape[1]
    cb = q_col // 256
    spec = lambda c: pl.BlockSpec((tq, 256), lambda i: (i, c))
    return pl.pallas_call(
        functools.partial(_mem_kernel, scale=scale),
        out_shape=jax.ShapeDtypeStruct((m, D_MEMG), BF16),
        grid=(m // tq,),
        in_specs=[spec(cb), spec(cb + 1), spec(cb + 2), spec(cb + 3),
                  pl.BlockSpec((1, n_mem, 2 * D_MEMG), lambda i: (i // nb, 0, 0))],
        out_specs=pl.BlockSpec((tq, D_MEMG), lambda i: (i, 0)),
        compiler_params=_cparams(("parallel",)),
        name="mem_attention",
    )(h_plain, h_plain, h_plain, h_plain, kv)


def _out_ln_kernel(yf_ref, yd_ref, ym_ref, w_ref, x_ref, g_ref, b_ref, xo_ref, xb_ref, *, alpha):
    y = jnp.dot(yf_ref[...], w_ref[0:D_FOX, :], preferred_element_type=F32)
    y = y + jnp.dot(yd_ref[...], w_ref[D_FOX:D_FOX + D_DSA, :], preferred_element_type=F32)
    y = y + jnp.dot(ym_ref[...], w_ref[D_FOX + D_DSA:, :], preferred_element_type=F32)
    z = alpha * x_ref[...] + y
    mu = jnp.mean(z, axis=1, keepdims=True)
    zc = z - mu
    var = jnp.mean(zc * zc, axis=1, keepdims=True)
    out = zc * lax.rsqrt(var + LN_EPS) * g_ref[...] + b_ref[...]
    xo_ref[...] = out
    xb_ref[...] = out.astype(xb_ref.dtype)


def _out_ln(yf, yd, ym, w_out, x, gain, bias, alpha):
    m, d = x.shape
    tm = SEQ_TILE
    row = lambda n: pl.BlockSpec((tm, n), lambda i: (i, 0))
    const = lambda r, c: pl.BlockSpec((r, c), lambda i: (0, 0))
    return pl.pallas_call(
        functools.partial(_out_ln_kernel, alpha=alpha),
        out_shape=(jax.ShapeDtypeStruct((m, d), F32), jax.ShapeDtypeStruct((m, d), BF16)),
        grid=(m // tm,),
        in_specs=[row(D_FOX), row(D_DSA), row(D_MEMG), const(w_out.shape[0], d), row(d), const(1, d), const(1, d)],
        out_specs=(row(d), row(d)),
        compiler_params=_cparams(("parallel",), VMEM_LIMIT_BYTES),
        name="out_proj_layernorm",
    )(yf, yd, ym, w_out, x, gain, bias)


def _rope_tables(seq):
    pos = jnp.arange(seq).astype(F32)

    def tables(half):
        inv_freq = ROPE_THETA ** (-jnp.arange(half, dtype=F32) / half)
        ang = pos[:, None] * inv_freq[None, :]
        return jnp.cos(ang), jnp.sin(ang)

    c64, s64 = tables(HEAD_DIM // 2)
    c32, s32 = tables(IDX_DIM // 2)
    cos128 = jnp.concatenate([c64, c64], axis=1)
    sin128 = jnp.concatenate([-s64, s64], axis=1)
    zeros = jnp.zeros((seq, 128 - IDX_DIM), F32)
    cos_ik = jnp.concatenate([c32, c32, zeros], axis=1)
    sin_ik = jnp.concatenate([s32, s32, zeros], axis=1)
    return cos128, sin128, (c64.T, s64.T, c32.T, s32.T), cos_ik, sin_ik


def _prepare(seq, d_model, w_in, b_forget, w_mem_kv, w_out):
    depth = w_in.shape[0]
    splits = (D_FOX, D_FOX, D_FOX, D_FOX, H_FOX, D_DSA, D_DSA, D_DSA, D_DSA,
              IDX_HEADS * IDX_DIM, IDX_DIM, IDX_HEADS, D_MEMG, D_MEMG)
    offs = np.concatenate([[0], np.cumsum(splits)]).tolist()
    seg = lambda i: w_in[:, :, offs[i]:offs[i + 1]]
    (w_fq, w_fk, w_fv, w_fg, w_fl, w_dq, w_dk, w_dv, w_dg, w_iq, w_ik, w_iw, w_mq, w_mg) = [seg(i) for i in range(14)]
    pad = jnp.zeros((depth, d_model, 128 - IDX_DIM - IDX_HEADS - H_FOX), w_in.dtype)
    transposed = lambda ws: jnp.swapaxes(jnp.concatenate([w.astype(BF16) for w in ws], axis=2), 1, 2)
    cos128, sin128, t_tables, cos_ik, sin_ik = _rope_tables(seq)
    ts = SEQ_TILE
    return dict(
        w_plain=jnp.concatenate([w_fk, w_fg, w_dg, w_mq, w_mg], axis=2).astype(BF16),
        q_col=2 * D_FOX + D_DSA,
        w_dk=w_dk.astype(BF16),
        w_t_dsa=transposed([w_dq, w_dv, w_iq]),
        w_t_fox=transposed([w_fq, w_fv]),
        w_small=jnp.concatenate([w_ik, w_iw, w_fl, pad], axis=2).astype(BF16),
        w_out=w_out.astype(BF16),
        w_mem=w_mem_kv.astype(BF16),
        bf_rows=jnp.zeros((depth, 1, 128), F32).at[:, 0, OFF_FL:OFF_FL + H_FOX].set(b_forget.astype(F32)),
        cos128=cos128, sin128=sin128, t_tables=t_tables, cos_ik=cos_ik, sin_ik=sin_ik,
        tri=(jnp.arange(ts)[:, None] >= jnp.arange(ts)[None, :]).astype(BF16),
    )


def _mixers(xb, mem_b, l, p, batch, seq, n_mem):
    scale = HEAD_DIM ** -0.5
    topk = min(INDEX_TOPK, seq // 4)
    w_scale = (IDX_HEADS ** -0.5) * (IDX_DIM ** -0.5)
    kv = _proj_plain(mem_b, p["w_mem"][l], batch * n_mem, 2 * D_MEMG, "proj_mem_kv")
    kv = kv.reshape(batch, n_mem, 2 * D_MEMG)
    h_plain = _proj_plain(xb, p["w_plain"][l], 2 * SEQ_TILE, p["w_plain"].shape[2] // 2, "proj_plain")
    dk = _proj_rope(xb, p["w_dk"][l], p["cos128"], p["sin128"], seq, 2 * SEQ_TILE, "proj_dsa_key")
    dqt, dvt, iqt = _proj_t(xb, p["w_t_dsa"][l], p["t_tables"], batch, seq, scale * LOG2E,
                            (("rope_head_scaled", D_DSA), ("values", D_DSA), ("rope_index", IDX_HEADS * IDX_DIM)),
                            "proj_transposed_dsa")
    fqt, fvt = _proj_t(xb, p["w_t_fox"][l], p["t_tables"], batch, seq, scale * LOG2E,
                       (("scaled", D_FOX), ("values", D_FOX)), "proj_transposed_fox")
    ik, iwt, cb = _prep(xb, p["w_small"][l], p["bf_rows"][l], p["cos_ik"], p["sin_ik"], p["tri"], batch, seq, w_scale)
    y_fox = _fox_attention(h_plain, cb, fqt, fvt, batch, seq)
    y_dsa = _dsa_attention(dk, dvt, ik, dqt, iqt, iwt, h_plain, batch, seq, topk)
    y_mem = _mem_attention(h_plain, kv, batch, seq, scale, p["q_col"])
    return y_fox, y_dsa, y_mem


def kernel(x, mem, w_in, b_forget, w_mem_kv, w_out, ln_gain, ln_bias):
    batch, seq, d_model = x.shape
    depth = w_in.shape[0]
    n_mem = mem.shape[1]
    m = batch * seq
    alpha = (2 * depth) ** 0.25
    assert seq % (2 * SEQ_TILE) == 0 and min(INDEX_TOPK, seq // 4) <= SEQ_TILE

    p = _prepare(seq, d_model, w_in, b_forget, w_mem_kv, w_out)
    mem_b = mem.reshape(batch * n_mem, d_model).astype(BF16)
    xf = x.reshape(m, d_model).astype(F32)
    xb = xf.astype(BF16)
    for l in range(depth):
        y_fox, y_dsa, y_mem = _mixers(xb, mem_b, l, p, batch, seq, n_mem)
        xf, xb = _out_ln(y_fox, y_dsa, y_mem, p["w_out"][l], xf, ln_gain[l][None, :].astype(F32),
                         ln_bias[l][None, :].astype(F32), alpha)
    return xf.reshape(batch, seq, d_model).astype(x.dtype)
```

```python
import functools

import numpy as np
import jax
import jax.numpy as jnp
from jax import lax
from jax.experimental import pallas as pl
from jax.experimental.pallas import tpu as pltpu

HEAD_DIM = 128
H_FOX = 6
H_DSA = 6
H_MEM = 4
D_FOX = H_FOX * HEAD_DIM
D_DSA = H_DSA * HEAD_DIM
D_MEMG = H_MEM * HEAD_DIM
IDX_HEADS = 16
IDX_DIM = 64
INDEX_TOPK = 256
ROPE_THETA = 10000.0
LN_EPS = 1e-5
NEG = -1e30
M_INIT = -1e29
LOG2E = float(np.log2(np.e))

SEQ_TILE = 256
VMEM_LIMIT_BYTES = 56 * 1024 * 1024
OFF_IW = IDX_DIM
OFF_FL = IDX_DIM + IDX_HEADS
BIAS_PIECES = 3
BLOCKS_PER_STEP = 4
ONES_ROWS = 16
V_ROWS = HEAD_DIM + ONES_ROWS

F32 = jnp.float32
BF16 = jnp.bfloat16


def _sortable_key(bits):
    return bits ^ ((bits >> 31) & 0x7FFFFFFF)


_NEG_BITS = int(np.array(NEG, np.float32).view(np.int32))
KEY_NEG = int(np.int32(_NEG_BITS) ^ np.int32(0x7FFFFFFF))
INT_MIN = -(2 ** 31)


def _cparams(sem, vmem=None):
    return pltpu.CompilerParams(dimension_semantics=sem, vmem_limit_bytes=vmem)


def _silu(g):
    return g / (1.0 + jnp.exp(-g))


COUNT_CHAINS = 4


def _partial_sums(x):
    rows, lanes = x.shape
    grouped = x.reshape(rows // (COUNT_CHAINS * 8), COUNT_CHAINS * 8, lanes)
    return jnp.sum(grouped, axis=0)


def _split_bf16(x, pieces):
    out = []
    r = x
    for _ in range(pieces - 1):
        p = r.astype(BF16)
        out.append(p)
        r = r - p.astype(F32)
    out.append(r.astype(BF16))
    return out


def _proj_plain_kernel(x_ref, w_ref, o_ref):
    o_ref[...] = jnp.dot(x_ref[...], w_ref[...], preferred_element_type=F32).astype(o_ref.dtype)


def _proj_plain(xb, w, tm, tn, name):
    m, d = xb.shape
    n = w.shape[1]
    return pl.pallas_call(
        _proj_plain_kernel,
        out_shape=jax.ShapeDtypeStruct((m, n), BF16),
        grid=(n // tn, m // tm),
        in_specs=[pl.BlockSpec((tm, d), lambda j, i: (i, 0)),
                  pl.BlockSpec((d, tn), lambda j, i: (0, j))],
        out_specs=pl.BlockSpec((tm, tn), lambda j, i: (i, j)),
        compiler_params=_cparams(("parallel", "parallel"), VMEM_LIMIT_BYTES),
        name=name,
    )(xb, w)


def _proj_rope_kernel(x_ref, w_ref, cos_ref, sin_ref, o_ref, *, heads):
    y = jnp.dot(x_ref[...], w_ref[...], preferred_element_type=F32)
    c = cos_ref[...]
    s = sin_ref[...]
    for h in range(heads):
        yh = y[:, h * HEAD_DIM:(h + 1) * HEAD_DIM]
        rot = pltpu.roll(yh, HEAD_DIM // 2, axis=1)
        o_ref[:, h * HEAD_DIM:(h + 1) * HEAD_DIM] = (yh * c + rot * s).astype(o_ref.dtype)


def _proj_rope(xb, w, cos128, sin128, seq, tm, name):
    m, d = xb.shape
    n = w.shape[1]
    nseq = seq // tm
    return pl.pallas_call(
        functools.partial(_proj_rope_kernel, heads=n // HEAD_DIM),
        out_shape=jax.ShapeDtypeStruct((m, n), BF16),
        grid=(m // tm,),
        in_specs=[pl.BlockSpec((tm, d), lambda i: (i, 0)),
                  pl.BlockSpec((d, n), lambda i: (0, 0)),
                  pl.BlockSpec((tm, HEAD_DIM), lambda i: (i % nseq, 0)),
                  pl.BlockSpec((tm, HEAD_DIM), lambda i: (i % nseq, 0))],
        out_specs=pl.BlockSpec((tm, n), lambda i: (i, 0)),
        compiler_params=_cparams(("parallel",), VMEM_LIMIT_BYTES),
        name=name,
    )(xb, w, cos128, sin128)


def _proj_t_kernel(x_ref, wt_ref, cos64_ref, sin64_ref, cos32_ref, sin32_ref, *out_refs, groups, scale):
    yt = lax.dot_general(wt_ref[...], x_ref[...], (((1,), (1,)), ((), ())), preferred_element_type=F32)
    base = 0
    for (kind, rows), o_ref in zip(groups, out_refs):
        y = yt[base:base + rows]
        base += rows
        if kind == "values":
            ones = jnp.ones((ONES_ROWS, y.shape[1]), o_ref.dtype)
            for h in range(rows // HEAD_DIM):
                o_ref[0, 0, h * V_ROWS:h * V_ROWS + HEAD_DIM, :] = y[h * HEAD_DIM:(h + 1) * HEAD_DIM].astype(o_ref.dtype)
                o_ref[0, 0, h * V_ROWS + HEAD_DIM:(h + 1) * V_ROWS, :] = ones
        elif kind == "scaled":
            o_ref[0, 0] = (y * scale).astype(o_ref.dtype)
        else:
            dim, cos_t, sin_t, mul = ((HEAD_DIM, cos64_ref[...], sin64_ref[...], scale) if kind == "rope_head_scaled"
                                      else (IDX_DIM, cos32_ref[...], sin32_ref[...], 1.0))
            half = dim // 2
            for h in range(rows // dim):
                x1 = y[h * dim:h * dim + half]
                x2 = y[h * dim + half:(h + 1) * dim]
                o_ref[0, 0, h * dim:h * dim + half, :] = ((x1 * cos_t - x2 * sin_t) * mul).astype(o_ref.dtype)
                o_ref[0, 0, h * dim + half:(h + 1) * dim, :] = ((x2 * cos_t + x1 * sin_t) * mul).astype(o_ref.dtype)


def _proj_t(xb, wt, tables, batch, seq, scale, groups, name):
    m, d = xb.shape
    ts = SEQ_TILE
    nb = seq // ts
    rows = wt.shape[0]
    assert rows == sum(r for _, r in groups)
    out_rows = [r // HEAD_DIM * V_ROWS if kind == "values" else r for kind, r in groups]
    blk = lambda r: pl.BlockSpec((1, 1, r, ts), lambda i: (i // nb, i % nb, 0, 0))
    tab = lambda r: pl.BlockSpec((r, ts), lambda i: (0, i % nb))
    return pl.pallas_call(
        functools.partial(_proj_t_kernel, groups=groups, scale=scale),
        out_shape=tuple(jax.ShapeDtypeStruct((batch, nb, r, ts), BF16) for r in out_rows),
        grid=(m // ts,),
        in_specs=[pl.BlockSpec((ts, d), lambda i: (i, 0)),
                  pl.BlockSpec((rows, d), lambda i: (0, 0)),
                  tab(HEAD_DIM // 2), tab(HEAD_DIM // 2), tab(IDX_DIM // 2), tab(IDX_DIM // 2)],
        out_specs=tuple(blk(r) for r in out_rows),
        compiler_params=_cparams(("parallel",), VMEM_LIMIT_BYTES),
        name=name,
    )(xb, wt, *tables)


def _prep_kernel(x_ref, w_ref, bf_ref, cos_ref, sin_ref, tri_ref, ik_ref, iwt_ref, cb_ref, carry_ref, *, w_scale):
    j = pl.program_id(1)
    raw = jnp.dot(x_ref[...], w_ref[...], preferred_element_type=F32)
    ts = raw.shape[0]

    lane = lax.broadcasted_iota(jnp.int32, raw.shape, 1)
    q = IDX_DIM // 2
    rot = jnp.where(lane < q, -pltpu.roll(raw, 128 - q, axis=1), pltpu.roll(raw, q, axis=1))
    ik = raw * cos_ref[...] + rot * sin_ref[...]
    ik_ref[0] = ik[:, :IDX_DIM].astype(ik_ref.dtype)

    iwt_ref[0, 0] = raw.T[OFF_IW:OFF_IW + IDX_HEADS] * w_scale

    z = raw + bf_ref[...]
    logf = jnp.minimum(z, 0.0) - jnp.log(1.0 + jnp.exp(-jnp.abs(z)))
    tri = tri_ref[...]
    cs = sum(jnp.dot(tri, p, preferred_element_type=F32) for p in _split_bf16(logf, 3))

    @pl.when(j == 0)
    def _():
        carry_ref[...] = jnp.zeros_like(carry_ref)

    cs = cs + carry_ref[...]
    carry_ref[...] = cs[ts - 1:ts, :]

    r = lax.broadcasted_iota(jnp.int32, (128, 128), 0)
    c = lax.broadcasted_iota(jnp.int32, (128, 128), 1)
    head_row = (r >= OFF_FL) & (r < OFF_FL + H_FOX)
    cb = jnp.zeros((ts, 128), F32)
    for k, piece in enumerate(_split_bf16(-LOG2E * cs, BIAS_PIECES)):
        place = (head_row & (c == BIAS_PIECES * (r - OFF_FL) + k)).astype(BF16)
        cb = cb + jnp.dot(piece, place, preferred_element_type=F32)
    cb_ref[0] = cb.astype(cb_ref.dtype)


def _prep(xb, w_small, bf_row, cos_ik, sin_ik, tri, batch, seq, w_scale):
    m, d = xb.shape
    ts = SEQ_TILE
    nb = seq // ts
    out_shapes = (jax.ShapeDtypeStruct((batch, seq, IDX_DIM), BF16),
                  jax.ShapeDtypeStruct((batch, nb, IDX_HEADS, ts), F32),
                  jax.ShapeDtypeStruct((batch, seq, 128), BF16))
    return pl.pallas_call(
        functools.partial(_prep_kernel, w_scale=w_scale),
        out_shape=out_shapes,
        grid=(batch, nb),
        in_specs=[pl.BlockSpec((ts, d), lambda b, j: (b * nb + j, 0)),
                  pl.BlockSpec((d, 128), lambda b, j: (0, 0)),
                  pl.BlockSpec((1, 128), lambda b, j: (0, 0)),
                  pl.BlockSpec((ts, 128), lambda b, j: (j, 0)),
                  pl.BlockSpec((ts, 128), lambda b, j: (j, 0)),
                  pl.BlockSpec((ts, ts), lambda b, j: (0, 0))],
        out_specs=(pl.BlockSpec((1, ts, IDX_DIM), lambda b, j: (b, j, 0)),
                   pl.BlockSpec((1, 1, IDX_HEADS, ts), lambda b, j: (b, j, 0, 0)),
                   pl.BlockSpec((1, ts, 128), lambda b, j: (b, j, 0))),
        scratch_shapes=[pltpu.VMEM((1, 128), F32)],
        compiler_params=_cparams(("parallel", "arbitrary")),
        name="prep_small",
    )(xb, w_small, bf_row, cos_ik, sin_ik, tri)


def _attention_scratch(heads, tq):
    return [pltpu.VMEM((heads, 1, tq), F32),
            pltpu.VMEM((heads, 1, tq), F32),
            pltpu.VMEM((heads, HEAD_DIM, tq), F32),
            pltpu.VMEM((heads, 1, tq), F32),
            pltpu.VMEM((heads, tq, tq), BF16)]


def _attention_core(heads, qi, score_fn, values_fn, diag_mask, g_ref, o_ref,
                    m_ref, l_ref, acc_ref, a_ref, p_ref):
    m_ref[...] = jnp.full(m_ref.shape, M_INIT, F32)
    l_ref[...] = jnp.zeros(l_ref.shape, F32)
    acc_ref[...] = jnp.zeros(acc_ref.shape, F32)
    a_ref[...] = jnp.ones(a_ref.shape, F32)
    p_ref[...] = jnp.zeros(p_ref.shape, BF16)

    def values(kb, h):
        pv = jnp.dot(values_fn(kb, h), p_ref[h], preferred_element_type=F32)
        a = a_ref[h]
        acc_ref[h] = a * acc_ref[h] + pv[:HEAD_DIM]
        l_ref[h] = a * l_ref[h] + pv[HEAD_DIM:HEAD_DIM + 1]

    def score_softmax(kb, h, last):
        s = score_fn(kb, h)
        if last and diag_mask is not None:
            s = diag_mask(s)
        m = m_ref[h]
        m_new = jnp.maximum(m, jnp.max(s, axis=0, keepdims=True))
        a = jnp.exp2(m - m_new)
        p = jnp.exp2(s - m_new)
        m_ref[h] = m_new
        a_ref[h] = a
        p_ref[h] = p.astype(BF16)

    def block(kb, last):
        prev = jnp.maximum(kb - 1, 0)
        for h in range(heads):
            values(prev, h)
            score_softmax(kb, h, last)

    def group_step(width):
        def step(g, carry):
            for j in range(width):
                block(g * width + j, False)
            return carry
        return step

    lax.fori_loop(0, qi // BLOCKS_PER_STEP, group_step(BLOCKS_PER_STEP), 0)
    width = BLOCKS_PER_STEP // 2
    while width >= 1:
        first = (qi // (2 * width)) * 2

        @pl.when((qi & width) != 0)
        def _(width=width, first=first):
            group_step(width)(first, 0)

        width //= 2

    block(qi, True)
    for h in range(heads):
        values(qi, h)
    for h in range(heads):
        hs = slice(h * HEAD_DIM, (h + 1) * HEAD_DIM)
        o = (acc_ref[h] / l_ref[h]).T
        o_ref[:, hs] = (o * _silu(g_ref[:, hs].astype(F32))).astype(o_ref.dtype)


def _fox_kernel(k_ref, cb_ref, qt_ref, vt_ref, g_ref, o_ref, qa_ref, *scratch):
    qi = pl.program_id(1)
    tq = o_ref.shape[0]
    tk = tq
    row = lax.broadcasted_iota(jnp.int32, (tk, tq), 0)
    col = lax.broadcasted_iota(jnp.int32, (tk, tq), 1)
    sel_row = lax.broadcasted_iota(jnp.int32, (HEAD_DIM, tq), 0)
    for h in range(H_FOX):
        qa_ref[h, 0:HEAD_DIM, :] = qt_ref[0, 0, h * HEAD_DIM:(h + 1) * HEAD_DIM, :]
        pick = (sel_row >= BIAS_PIECES * h) & (sel_row < BIAS_PIECES * (h + 1))
        qa_ref[h, HEAD_DIM:2 * HEAD_DIM, :] = pick.astype(BF16)

    def score_fn(kb, h):
        start = pl.multiple_of(kb * tk, tk)
        keys = jnp.concatenate([k_ref[pl.ds(start, tk), h * HEAD_DIM:(h + 1) * HEAD_DIM],
                                cb_ref[0, pl.ds(start, tk), :]], axis=1)
        return jnp.dot(keys, qa_ref[h], preferred_element_type=F32)

    def values_fn(kb, h):
        return vt_ref[0, kb, h * V_ROWS:(h + 1) * V_ROWS, :]

    def diag_mask(s):
        return jnp.where(row <= col, s, NEG)

    _attention_core(H_FOX, qi, score_fn, values_fn, diag_mask, g_ref, o_ref, *scratch)


def _fox_attention(h_plain, cb, fqt, fvt, batch, seq):
    tq = SEQ_TILE
    nb = seq // tq
    m = batch * seq
    return pl.pallas_call(
        _fox_kernel,
        out_shape=jax.ShapeDtypeStruct((m, D_FOX), BF16),
        grid=(batch, nb),
        in_specs=[pl.BlockSpec((seq, D_FOX), lambda b, i: (b, 0)),
                  pl.BlockSpec((1, seq, 128), lambda b, i: (b, 0, 0)),
                  pl.BlockSpec((1, 1, D_FOX, tq), lambda b, i: (b, i, 0, 0)),
                  pl.BlockSpec((1, nb, H_FOX * V_ROWS, tq), lambda b, i: (b, 0, 0, 0)),
                  pl.BlockSpec((tq, D_FOX), lambda b, i: (b * nb + i, 1))],
        out_specs=pl.BlockSpec((tq, D_FOX), lambda b, i: (b * nb + i, 0)),
        scratch_shapes=[pltpu.VMEM((H_FOX, 2 * HEAD_DIM, tq), BF16)] + _attention_scratch(H_FOX, tq),
        compiler_params=_cparams(("parallel", "parallel"), VMEM_LIMIT_BYTES),
        name="fox_attention",
    )(h_plain, cb, fqt, fvt, h_plain)


def _dsa_kernel(dk_ref, dvt_ref, ik_ref, dqt_ref, iqt_ref, iwt_ref, g_ref, o_ref, keys_ref, *scratch, topk):
    qi = pl.program_id(1)
    tq = o_ref.shape[0]
    tk = tq
    nkb = qi + 1
    row = lax.broadcasted_iota(jnp.int32, (tk, tq), 0)
    col = lax.broadcasted_iota(jnp.int32, (tk, tq), 1)

    def score_block(kb, carry):
        start = pl.multiple_of(kb * tk, tk)
        ikb = ik_ref[0, pl.ds(start, tk), :]
        acc = jnp.zeros((tk, tq), F32)
        for h in range(IDX_HEADS):
            z = jnp.dot(ikb, iqt_ref[0, 0, h * IDX_DIM:(h + 1) * IDX_DIM, :], preferred_element_type=F32)
            acc = acc + jnp.maximum(z, 0.0) * iwt_ref[0, 0, h:h + 1, :]
        key = _sortable_key(lax.bitcast_convert_type(acc, jnp.int32))
        keys_ref[kb] = jnp.where((kb == qi) & (row > col), KEY_NEG, key)
        return carry

    lax.fori_loop(0, nkb, score_block, 0)

    def count(pred):
        def body(kb, acc):
            hit = pred(keys_ref[kb], row + kb * tk).astype(jnp.int32)
            return acc + _partial_sums(hit)
        part = lax.fori_loop(0, nkb, body, jnp.zeros((COUNT_CHAINS * 8, tq), jnp.int32))
        return jnp.sum(part, axis=0, keepdims=True)

    n_nonneg = count(lambda k, _: k >= 0)
    nonneg = n_nonneg >= topk
    prefix0 = jnp.where(nonneg, 0, INT_MIN).astype(jnp.int32)
    n_all = jnp.full((1, tq), tk, jnp.int32) * nkb

    def bit_step(i, carry):
        prefix, n_ge = carry
        cand = prefix | (jnp.int32(1) << (30 - i))
        n = count(lambda k, _: k >= cand)
        keep = n >= topk
        return jnp.where(keep, cand, prefix), jnp.where(keep, n, n_ge)

    vstar, n_ge = lax.fori_loop(0, 31, bit_step, (prefix0, jnp.where(nonneg, n_nonneg, n_all)))

    tie = jnp.max(((n_ge > topk) & (vstar > KEY_NEG)).astype(jnp.int32)) > 0

    def tie_index(_):
        n_gt = count(lambda k, _: k > vstar)

        def idx_step(i, jx):
            cand = jx | (jnp.int32(1) << (29 - i))
            n = n_gt + count(lambda k, idx: (k == vstar) & (idx < cand))
            return jnp.where(n <= topk, cand, jx)
        return lax.fori_loop(0, 30, idx_step, jnp.zeros((1, tq), jnp.int32))

    def store_bias(kb, sel):
        keys_ref[kb] = lax.bitcast_convert_type(jnp.where(sel, 0.0, NEG).astype(F32), jnp.int32)

    def bias_with_ties(_):
        jx = tie_index(0)

        def body(kb, carry):
            k = keys_ref[kb]
            store_bias(kb, ((k > vstar) | ((k == vstar) & ((row + kb * tk) < jx))) & (k > KEY_NEG))
            return carry

        lax.fori_loop(0, nkb, body, 0)
        return 0

    def bias_without_ties(_):
        threshold = jnp.maximum(vstar, KEY_NEG + 1)

        def body(kb, carry):
            store_bias(kb, keys_ref[kb] >= threshold)
            return carry

        lax.fori_loop(0, nkb, body, 0)
        return 0

    lax.cond(tie, bias_with_ties, bias_without_ties, 0)

    def score_fn(kb, h):
        start = pl.multiple_of(kb * tk, tk)
        hs = slice(h * HEAD_DIM, (h + 1) * HEAD_DIM)
        s = jnp.dot(dk_ref[pl.ds(start, tk), hs], dqt_ref[0, 0, hs, :], preferred_element_type=F32)
        return s + lax.bitcast_convert_type(keys_ref[kb], F32)

    def values_fn(kb, h):
        return dvt_ref[0, kb, h * V_ROWS:(h + 1) * V_ROWS, :]

    _attention_core(H_DSA, qi, score_fn, values_fn, None, g_ref, o_ref, *scratch)


def _dsa_attention(dk, dvt, ik, dqt, iqt, iwt, h_plain, batch, seq, topk):
    tq = SEQ_TILE
    nb = seq // tq
    m = batch * seq
    return pl.pallas_call(
        functools.partial(_dsa_kernel, topk=topk),
        out_shape=jax.ShapeDtypeStruct((m, D_DSA), BF16),
        grid=(batch, nb),
        in_specs=[pl.BlockSpec((seq, D_DSA), lambda b, i: (b, 0)),
                  pl.BlockSpec((1, nb, H_DSA * V_ROWS, tq), lambda b, i: (b, 0, 0, 0)),
                  pl.BlockSpec((1, seq, IDX_DIM), lambda b, i: (b, 0, 0)),
                  pl.BlockSpec((1, 1, D_DSA, tq), lambda b, i: (b, i, 0, 0)),
                  pl.BlockSpec((1, 1, IDX_HEADS * IDX_DIM, tq), lambda b, i: (b, i, 0, 0)),
                  pl.BlockSpec((1, 1, IDX_HEADS, tq), lambda b, i: (b, i, 0, 0)),
                  pl.BlockSpec((tq, D_DSA), lambda b, i: (b * nb + i, 2))],
        out_specs=pl.BlockSpec((tq, D_DSA), lambda b, i: (b * nb + i, 0)),
        scratch_shapes=[pltpu.VMEM((nb, tq, tq), jnp.int32)] + _attention_scratch(H_DSA, tq),
        compiler_params=_cparams(("parallel", "parallel"), VMEM_LIMIT_BYTES),
        name="dsa_attention",
    )(dk, dvt, ik, dqt, iqt, iwt, h_plain)


def _mem_kernel(q0_ref, q1_ref, g0_ref, g1_ref, kv_ref, o_ref, *, scale):
    for h in range(H_MEM):
        q_ref = (q0_ref, q1_ref)[h // 2]
        g_ref = (g0_ref, g1_ref)[h // 2]
        ls = slice((h % 2) * HEAD_DIM, (h % 2 + 1) * HEAD_DIM)
        q = (q_ref[:, ls].astype(F32) * scale).astype(BF16)
        k = kv_ref[0, :, h * HEAD_DIM:(h + 1) * HEAD_DIM]
        v = kv_ref[0, :, D_MEMG + h * HEAD_DIM:D_MEMG + (h + 1) * HEAD_DIM]
        s = lax.dot_general(q, k, (((1,), (1,)), ((), ())), preferred_element_type=F32)
        m = jnp.max(s, axis=1, keepdims=True)
        p = jnp.exp(s - m)
        l = jnp.sum(p, axis=1, keepdims=True)
        o = jnp.dot(p.astype(BF16), v, preferred_element_type=F32) / l
        o_ref[:, h * HEAD_DIM:(h + 1) * HEAD_DIM] = (o * _silu(g_ref[:, ls].astype(F32))).astype(o_ref.dtype)


def _mem_attention(h_plain, kv, batch, seq, scale, q_col):
    tq = 2 * SEQ_TILE
    nb = seq // tq
    m = batch * seq
    n_mem = kv.shape[1]
    cb = q_col // 256
    spec = lambda c: pl.BlockSpec((tq, 256), lambda i: (i, c))
    return pl.pallas_call(
        functools.partial(_mem_kernel, scale=scale),
        out_shape=jax.ShapeDtypeStruct((m, D_MEMG), BF16),
        grid=(m // tq,),
        in_specs=[spec(cb), spec(cb + 1), spec(cb + 2), spec(cb + 3),
                  pl.BlockSpec((1, n_mem, 2 * D_MEMG), lambda i: (i // nb, 0, 0))],
        out_specs=pl.BlockSpec((tq, D_MEMG), lambda i: (i, 0)),
        compiler_params=_cparams(("parallel",)),
        name="mem_attention",
    )(h_plain, h_plain, h_plain, h_plain, kv)


def _out_ln_kernel(yf_ref, yd_ref, ym_ref, w_ref, x_ref, g_ref, b_ref, xo_ref, xb_ref, *, alpha):
    y = jnp.dot(yf_ref[...], w_ref[0:D_FOX, :], preferred_element_type=F32)
    y = y + jnp.dot(yd_ref[...], w_ref[D_FOX:D_FOX + D_DSA, :], preferred_element_type=F32)
    y = y + jnp.dot(ym_ref[...], w_ref[D_FOX + D_DSA:, :], preferred_element_type=F32)
    z = alpha * x_ref[...] + y
    mu = jnp.mean(z, axis=1, keepdims=True)
    zc = z - mu
    var = jnp.mean(zc * zc, axis=1, keepdims=True)
    out = zc * lax.rsqrt(var + LN_EPS) * g_ref[...] + b_ref[...]
    xo_ref[...] = out
    xb_ref[...] = out.astype(xb_ref.dtype)


def _out_ln(yf, yd, ym, w_out, x, gain, bias, alpha):
    m, d = x.shape
    tm = SEQ_TILE
    row = lambda n: pl.BlockSpec((tm, n), lambda i: (i, 0))
    const = lambda r, c: pl.BlockSpec((r, c), lambda i: (0, 0))
    return pl.pallas_call(
        functools.partial(_out_ln_kernel, alpha=alpha),
        out_shape=(jax.ShapeDtypeStruct((m, d), F32), jax.ShapeDtypeStruct((m, d), BF16)),
        grid=(m // tm,),
        in_specs=[row(D_FOX), row(D_DSA), row(D_MEMG), const(w_out.shape[0], d), row(d), const(1, d), const(1, d)],
        out_specs=(row(d), row(d)),
        compiler_params=_cparams(("parallel",), VMEM_LIMIT_BYTES),
        name="out_proj_layernorm",
    )(yf, yd, ym, w_out, x, gain, bias)


def _rope_tables(seq):
    pos = jnp.arange(seq).astype(F32)

    def tables(half):
        inv_freq = ROPE_THETA ** (-jnp.arange(half, dtype=F32) / half)
        ang = pos[:, None] * inv_freq[None, :]
        return jnp.cos(ang), jnp.sin(ang)

    c64, s64 = tables(HEAD_DIM // 2)
    c32, s32 = tables(IDX_DIM // 2)
    cos128 = jnp.concatenate([c64, c64], axis=1)
    sin128 = jnp.concatenate([-s64, s64], axis=1)
    zeros = jnp.zeros((seq, 128 - IDX_DIM), F32)
    cos_ik = jnp.concatenate([c32, c32, zeros], axis=1)
    sin_ik = jnp.concatenate([s32, s32, zeros], axis=1)
    return cos128, sin128, (c64.T, s64.T, c32.T, s32.T), cos_ik, sin_ik


def _prepare(seq, d_model, w_in, b_forget, w_mem_kv, w_out):
    depth = w_in.shape[0]
    splits = (D_FOX, D_FOX, D_FOX, D_FOX, H_FOX, D_DSA, D_DSA, D_DSA, D_DSA,
              IDX_HEADS * IDX_DIM, IDX_DIM, IDX_HEADS, D_MEMG, D_MEMG)
    offs = np.concatenate([[0], np.cumsum(splits)]).tolist()
    seg = lambda i: w_in[:, :, offs[i]:offs[i + 1]]
    (w_fq, w_fk, w_fv, w_fg, w_fl, w_dq, w_dk, w_dv, w_dg, w_iq, w_ik, w_iw, w_mq, w_mg) = [seg(i) for i in range(14)]
    pad = jnp.zeros((depth, d_model, 128 - IDX_DIM - IDX_HEADS - H_FOX), w_in.dtype)
    transposed = lambda ws: jnp.swapaxes(jnp.concatenate([w.astype(BF16) for w in ws], axis=2), 1, 2)
    cos128, sin128, t_tables, cos_ik, sin_ik = _rope_tables(seq)
    ts = SEQ_TILE
    return dict(
        w_plain=jnp.concatenate([w_fk, w_fg, w_dg, w_mq, w_mg], axis=2).astype(BF16),
        q_col=2 * D_FOX + D_DSA,
        w_dk=w_dk.astype(BF16),
        w_t_dsa=transposed([w_dq, w_dv, w_iq]),
        w_t_fox=transposed([w_fq, w_fv]),
        w_small=jnp.concatenate([w_ik, w_iw, w_fl, pad], axis=2).astype(BF16),
        w_out=w_out.astype(BF16),
        w_mem=w_mem_kv.astype(BF16),
        bf_rows=jnp.zeros((depth, 1, 128), F32).at[:, 0, OFF_FL:OFF_FL + H_FOX].set(b_forget.astype(F32)),
        cos128=cos128, sin128=sin128, t_tables=t_tables, cos_ik=cos_ik, sin_ik=sin_ik,
        tri=(jnp.arange(ts)[:, None] >= jnp.arange(ts)[None, :]).astype(BF16),
    )


def _mixers(xb, mem_b, l, p, batch, seq, n_mem):
    scale = HEAD_DIM ** -0.5
    topk = min(INDEX_TOPK, seq // 4)
    w_scale = (IDX_HEADS ** -0.5) * (IDX_DIM ** -0.5)
    kv = _proj_plain(mem_b, p["w_mem"][l], batch * n_mem, 2 * D_MEMG, "proj_mem_kv")
    kv = kv.reshape(batch, n_mem, 2 * D_MEMG)
    h_plain = _proj_plain(xb, p["w_plain"][l], 2 * SEQ_TILE, p["w_plain"].shape[2] // 2, "proj_plain")
    dk = _proj_rope(xb, p["w_dk"][l], p["cos128"], p["sin128"], seq, 2 * SEQ_TILE, "proj_dsa_key")
    dqt, dvt, iqt = _proj_t(xb, p["w_t_dsa"][l], p["t_tables"], batch, seq, scale * LOG2E,
                            (("rope_head_scaled", D_DSA), ("values", D_DSA), ("rope_index", IDX_HEADS * IDX_DIM)),
                            "proj_transposed_dsa")
    fqt, fvt = _proj_t(xb, p["w_t_fox"][l], p["t_tables"], batch, seq, scale * LOG2E,
                       (("scaled", D_FOX), ("values", D_FOX)), "proj_transposed_fox")
    ik, iwt, cb = _prep(xb, p["w_small"][l], p["bf_rows"][l], p["cos_ik"], p["sin_ik"], p["tri"], batch, seq, w_scale)
    y_fox = _fox_attention(h_plain, cb, fqt, fvt, batch, seq)
    y_dsa = _dsa_attention(dk, dvt, ik, dqt, iqt, iwt, h_plain, batch, seq, topk)
    y_mem = _mem_attention(h_plain, kv, batch, seq, scale, p["q_col"])
    return y_fox, y_dsa, y_mem


def kernel(x, mem, w_in, b_forget, w_mem_kv, w_out, ln_gain, ln_bias):
    batch, seq, d_model = x.shape
    depth = w_in.shape[0]
    n_mem = mem.shape[1]
    m = batch * seq
    alpha = (2 * depth) ** 0.25
    assert seq % (2 * SEQ_TILE) == 0 and min(INDEX_TOPK, seq // 4) <= SEQ_TILE

    p = _prepare(seq, d_model, w_in, b_forget, w_mem_kv, w_out)
    mem_b = mem.reshape(batch * n_mem, d_model).astype(BF16)
    xf = x.reshape(m, d_model).astype(F32)
    xb = xf.astype(BF16)
    for l in range(depth):
        y_fox, y_dsa, y_mem = _mixers(xb, mem_b, l, p, batch, seq, n_mem)
        xf, xb = _out_ln(y_fox, y_dsa, y_mem, p["w_out"][l], xf, ln_gain[l][None, :].astype(F32),
                         ln_bias[l][None, :].astype(F32), alpha)
    return xf.reshape(batch, seq, d_model).astype(x.dtype)
```

```python
import functools

import numpy as np
import jax
import jax.numpy as jnp
from jax import lax
from jax.experimental import pallas as pl
from jax.experimental.pallas import tpu as pltpu

HEAD_DIM = 128
H_FOX = 6
H_DSA = 6
H_MEM = 4
D_FOX = H_FOX * HEAD_DIM
D_DSA = H_DSA * HEAD_DIM
D_MEMG = H_MEM * HEAD_DIM
IDX_HEADS = 16
IDX_DIM = 64
INDEX_TOPK = 256
ROPE_THETA = 10000.0
LN_EPS = 1e-5
NEG = -1e30
M_INIT = -1e29
LOG2E = float(np.log2(np.e))

SEQ_TILE = 256
VMEM_LIMIT_BYTES = 56 * 1024 * 1024
OFF_IW = IDX_DIM
OFF_FL = IDX_DIM + IDX_HEADS
BIAS_PIECES = 3
BLOCKS_PER_STEP = 4
ONES_ROWS = 16
V_ROWS = HEAD_DIM + ONES_ROWS

F32 = jnp.float32
BF16 = jnp.bfloat16


def _sortable_key(bits):
    return bits ^ ((bits >> 31) & 0x7FFFFFFF)


_NEG_BITS = int(np.array(NEG, np.float32).view(np.int32))
KEY_NEG = int(np.int32(_NEG_BITS) ^ np.int32(0x7FFFFFFF))
INT_MIN = -(2 ** 31)


def _cparams(sem, vmem=None):
    return pltpu.CompilerParams(dimension_semantics=sem, vmem_limit_bytes=vmem)


def _silu(g):
    return g / (1.0 + jnp.exp(-g))


COUNT_CHAINS = 4


def _partial_sums(x):
    rows, lanes = x.shape
    grouped = x.reshape(rows // (COUNT_CHAINS * 8), COUNT_CHAINS * 8, lanes)
    return jnp.sum(grouped, axis=0)


def _grouped_loop(n, width, body):
    def group_step(w):
        def step(g, carry):
            for j in range(w):
                body(g * w + j)
            return carry
        return step

    lax.fori_loop(0, n // width, group_step(width), 0)
    w = width // 2
    while w >= 1:
        @pl.when((n & w) != 0)
        def _(w=w):
            group_step(w)((n // (2 * w)) * 2, 0)
        w //= 2


def _split_bf16(x, pieces):
    out = []
    r = x
    for _ in range(pieces - 1):
        p = r.astype(BF16)
        out.append(p)
        r = r - p.astype(F32)
    out.append(r.astype(BF16))
    return out


def _proj_plain_kernel(x_ref, w_ref, o_ref):
    o_ref[...] = jnp.dot(x_ref[...], w_ref[...], preferred_element_type=F32).astype(o_ref.dtype)


def _proj_plain(xb, w, tm, tn, name):
    m, d = xb.shape
    n = w.shape[1]
    return pl.pallas_call(
        _proj_plain_kernel,
        out_shape=jax.ShapeDtypeStruct((m, n), BF16),
        grid=(n // tn, m // tm),
        in_specs=[pl.BlockSpec((tm, d), lambda j, i: (i, 0)),
                  pl.BlockSpec((d, tn), lambda j, i: (0, j))],
        out_specs=pl.BlockSpec((tm, tn), lambda j, i: (i, j)),
        compiler_params=_cparams(("parallel", "parallel"), VMEM_LIMIT_BYTES),
        name=name,
    )(xb, w)


def _proj_rope_kernel(x_ref, w_ref, cos_ref, sin_ref, o_ref, *, heads):
    y = jnp.dot(x_ref[...], w_ref[...], preferred_element_type=F32)
    c = cos_ref[...]
    s = sin_ref[...]
    for h in range(heads):
        yh = y[:, h * HEAD_DIM:(h + 1) * HEAD_DIM]
        rot = pltpu.roll(yh, HEAD_DIM // 2, axis=1)
        o_ref[:, h * HEAD_DIM:(h + 1) * HEAD_DIM] = (yh * c + rot * s).astype(o_ref.dtype)


def _proj_rope(xb, w, cos128, sin128, seq, tm, name):
    m, d = xb.shape
    n = w.shape[1]
    nseq = seq // tm
    return pl.pallas_call(
        functools.partial(_proj_rope_kernel, heads=n // HEAD_DIM),
        out_shape=jax.ShapeDtypeStruct((m, n), BF16),
        grid=(m // tm,),
        in_specs=[pl.BlockSpec((tm, d), lambda i: (i, 0)),
                  pl.BlockSpec((d, n), lambda i: (0, 0)),
                  pl.BlockSpec((tm, HEAD_DIM), lambda i: (i % nseq, 0)),
                  pl.BlockSpec((tm, HEAD_DIM), lambda i: (i % nseq, 0))],
        out_specs=pl.BlockSpec((tm, n), lambda i: (i, 0)),
        compiler_params=_cparams(("parallel",), VMEM_LIMIT_BYTES),
        name=name,
    )(xb, w, cos128, sin128)


def _proj_t_kernel(x_ref, wt_ref, cos64_ref, sin64_ref, cos32_ref, sin32_ref, *out_refs, groups, scale):
    yt = lax.dot_general(wt_ref[...], x_ref[...], (((1,), (1,)), ((), ())), preferred_element_type=F32)
    base = 0
    for (kind, rows), o_ref in zip(groups, out_refs):
        y = yt[base:base + rows]
        base += rows
        if kind == "values":
            ones = jnp.ones((ONES_ROWS, y.shape[1]), o_ref.dtype)
            for h in range(rows // HEAD_DIM):
                o_ref[0, 0, h * V_ROWS:h * V_ROWS + HEAD_DIM, :] = y[h * HEAD_DIM:(h + 1) * HEAD_DIM].astype(o_ref.dtype)
                o_ref[0, 0, h * V_ROWS + HEAD_DIM:(h + 1) * V_ROWS, :] = ones
        elif kind == "scaled":
            o_ref[0, 0] = (y * scale).astype(o_ref.dtype)
        else:
            dim, cos_t, sin_t, mul = ((HEAD_DIM, cos64_ref[...], sin64_ref[...], scale) if kind == "rope_head_scaled"
                                      else (IDX_DIM, cos32_ref[...], sin32_ref[...], 1.0))
            half = dim // 2
            for h in range(rows // dim):
                x1 = y[h * dim:h * dim + half]
                x2 = y[h * dim + half:(h + 1) * dim]
                o_ref[0, 0, h * dim:h * dim + half, :] = ((x1 * cos_t - x2 * sin_t) * mul).astype(o_ref.dtype)
                o_ref[0, 0, h * dim + half:(h + 1) * dim, :] = ((x2 * cos_t + x1 * sin_t) * mul).astype(o_ref.dtype)


def _proj_t(xb, wt, tables, batch, seq, scale, groups, name):
    m, d = xb.shape
    ts = SEQ_TILE
    nb = seq // ts
    rows = wt.shape[0]
    assert rows == sum(r for _, r in groups)
    out_rows = [r // HEAD_DIM * V_ROWS if kind == "values" else r for kind, r in groups]
    blk = lambda r: pl.BlockSpec((1, 1, r, ts), lambda i: (i // nb, i % nb, 0, 0))
    tab = lambda r: pl.BlockSpec((r, ts), lambda i: (0, i % nb))
    return pl.pallas_call(
        functools.partial(_proj_t_kernel, groups=groups, scale=scale),
        out_shape=tuple(jax.ShapeDtypeStruct((batch, nb, r, ts), BF16) for r in out_rows),
        grid=(m // ts,),
        in_specs=[pl.BlockSpec((ts, d), lambda i: (i, 0)),
                  pl.BlockSpec((rows, d), lambda i: (0, 0)),
                  tab(HEAD_DIM // 2), tab(HEAD_DIM // 2), tab(IDX_DIM // 2), tab(IDX_DIM // 2)],
        out_specs=tuple(blk(r) for r in out_rows),
        compiler_params=_cparams(("parallel",), VMEM_LIMIT_BYTES),
        name=name,
    )(xb, wt, *tables)


def _prep_kernel(x_ref, w_ref, bf_ref, cos_ref, sin_ref, tri_ref, ik_ref, iwt_ref, cb_ref, carry_ref, *, w_scale):
    j = pl.program_id(1)
    raw = jnp.dot(x_ref[...], w_ref[...], preferred_element_type=F32)
    ts = raw.shape[0]

    lane = lax.broadcasted_iota(jnp.int32, raw.shape, 1)
    q = IDX_DIM // 2
    rot = jnp.where(lane < q, -pltpu.roll(raw, 128 - q, axis=1), pltpu.roll(raw, q, axis=1))
    ik = raw * cos_ref[...] + rot * sin_ref[...]
    ik_ref[0] = ik[:, :IDX_DIM].astype(ik_ref.dtype)

    iwt_ref[0, 0] = raw.T[OFF_IW:OFF_IW + IDX_HEADS] * w_scale

    z = raw + bf_ref[...]
    logf = jnp.minimum(z, 0.0) - jnp.log(1.0 + jnp.exp(-jnp.abs(z)))
    tri = tri_ref[...]
    cs = sum(jnp.dot(tri, p, preferred_element_type=F32) for p in _split_bf16(logf, 3))

    @pl.when(j == 0)
    def _():
        carry_ref[...] = jnp.zeros_like(carry_ref)

    cs = cs + carry_ref[...]
    carry_ref[...] = cs[ts - 1:ts, :]

    r = lax.broadcasted_iota(jnp.int32, (128, 128), 0)
    c = lax.broadcasted_iota(jnp.int32, (128, 128), 1)
    head_row = (r >= OFF_FL) & (r < OFF_FL + H_FOX)
    cb = jnp.zeros((ts, 128), F32)
    for k, piece in enumerate(_split_bf16(-LOG2E * cs, BIAS_PIECES)):
        place = (head_row & (c == BIAS_PIECES * (r - OFF_FL) + k)).astype(BF16)
        cb = cb + jnp.dot(piece, place, preferred_element_type=F32)
    cb_ref[0] = cb.astype(cb_ref.dtype)


def _prep(xb, w_small, bf_row, cos_ik, sin_ik, tri, batch, seq, w_scale):
    m, d = xb.shape
    ts = SEQ_TILE
    nb = seq // ts
    out_shapes = (jax.ShapeDtypeStruct((batch, seq, IDX_DIM), BF16),
                  jax.ShapeDtypeStruct((batch, nb, IDX_HEADS, ts), F32),
                  jax.ShapeDtypeStruct((batch, seq, 128), BF16))
    return pl.pallas_call(
        functools.partial(_prep_kernel, w_scale=w_scale),
        out_shape=out_shapes,
        grid=(batch, nb),
        in_specs=[pl.BlockSpec((ts, d), lambda b, j: (b * nb + j, 0)),
                  pl.BlockSpec((d, 128), lambda b, j: (0, 0)),
                  pl.BlockSpec((1, 128), lambda b, j: (0, 0)),
                  pl.BlockSpec((ts, 128), lambda b, j: (j, 0)),
                  pl.BlockSpec((ts, 128), lambda b, j: (j, 0)),
                  pl.BlockSpec((ts, ts), lambda b, j: (0, 0))],
        out_specs=(pl.BlockSpec((1, ts, IDX_DIM), lambda b, j: (b, j, 0)),
                   pl.BlockSpec((1, 1, IDX_HEADS, ts), lambda b, j: (b, j, 0, 0)),
                   pl.BlockSpec((1, ts, 128), lambda b, j: (b, j, 0))),
        scratch_shapes=[pltpu.VMEM((1, 128), F32)],
        compiler_params=_cparams(("parallel", "arbitrary")),
        name="prep_small",
    )(xb, w_small, bf_row, cos_ik, sin_ik, tri)


def _attention_scratch(heads, tq):
    return [pltpu.VMEM((heads, 1, tq), F32),
            pltpu.VMEM((heads, 1, tq), F32),
            pltpu.VMEM((heads, HEAD_DIM, tq), F32),
            pltpu.VMEM((heads, 1, tq), F32),
            pltpu.VMEM((heads, tq, tq), BF16)]


def _attention_core(heads, qi, score_fn, values_fn, diag_mask, g_ref, o_ref,
                    m_ref, l_ref, acc_ref, a_ref, p_ref):
    m_ref[...] = jnp.full(m_ref.shape, M_INIT, F32)
    l_ref[...] = jnp.zeros(l_ref.shape, F32)
    acc_ref[...] = jnp.zeros(acc_ref.shape, F32)
    a_ref[...] = jnp.ones(a_ref.shape, F32)
    p_ref[...] = jnp.zeros(p_ref.shape, BF16)

    def values(kb, h):
        pv = jnp.dot(values_fn(kb, h), p_ref[h], preferred_element_type=F32)
        a = a_ref[h]
        acc_ref[h] = a * acc_ref[h] + pv[:HEAD_DIM]
        l_ref[h] = a * l_ref[h] + pv[HEAD_DIM:HEAD_DIM + 1]

    def score_softmax(kb, h, last):
        s = score_fn(kb, h)
        if last and diag_mask is not None:
            s = diag_mask(s)
        m = m_ref[h]
        m_new = jnp.maximum(m, jnp.max(s, axis=0, keepdims=True))
        a = jnp.exp2(m - m_new)
        p = jnp.exp2(s - m_new)
        m_ref[h] = m_new
        a_ref[h] = a
        p_ref[h] = p.astype(BF16)

    def block(kb, last):
        prev = jnp.maximum(kb - 1, 0)
        for h in range(heads):
            values(prev, h)
            score_softmax(kb, h, last)

    _grouped_loop(qi, BLOCKS_PER_STEP, lambda kb: block(kb, False))
    block(qi, True)
    for h in range(heads):
        values(qi, h)
    for h in range(heads):
        hs = slice(h * HEAD_DIM, (h + 1) * HEAD_DIM)
        o = (acc_ref[h] / l_ref[h]).T
        o_ref[:, hs] = (o * _silu(g_ref[:, hs].astype(F32))).astype(o_ref.dtype)


def _fox_kernel(k_ref, cb_ref, qt_ref, vt_ref, g_ref, o_ref, qa_ref, *scratch):
    qi = pl.program_id(1)
    tq = o_ref.shape[0]
    tk = tq
    row = lax.broadcasted_iota(jnp.int32, (tk, tq), 0)
    col = lax.broadcasted_iota(jnp.int32, (tk, tq), 1)
    sel_row = lax.broadcasted_iota(jnp.int32, (HEAD_DIM, tq), 0)
    for h in range(H_FOX):
        qa_ref[h, 0:HEAD_DIM, :] = qt_ref[0, 0, h * HEAD_DIM:(h + 1) * HEAD_DIM, :]
        pick = (sel_row >= BIAS_PIECES * h) & (sel_row < BIAS_PIECES * (h + 1))
        qa_ref[h, HEAD_DIM:2 * HEAD_DIM, :] = pick.astype(BF16)

    def score_fn(kb, h):
        start = pl.multiple_of(kb * tk, tk)
        keys = jnp.concatenate([k_ref[pl.ds(start, tk), h * HEAD_DIM:(h + 1) * HEAD_DIM],
                                cb_ref[0, pl.ds(start, tk), :]], axis=1)
        return jnp.dot(keys, qa_ref[h], preferred_element_type=F32)

    def values_fn(kb, h):
        return vt_ref[0, kb, h * V_ROWS:(h + 1) * V_ROWS, :]

    def diag_mask(s):
        return jnp.where(row <= col, s, NEG)

    _attention_core(H_FOX, qi, score_fn, values_fn, diag_mask, g_ref, o_ref, *scratch)


def _fox_attention(h_plain, cb, fqt, fvt, batch, seq):
    tq = SEQ_TILE
    nb = seq // tq
    m = batch * seq
    return pl.pallas_call(
        _fox_kernel,
        out_shape=jax.ShapeDtypeStruct((m, D_FOX), BF16),
        grid=(batch, nb),
        in_specs=[pl.BlockSpec((seq, D_FOX), lambda b, i: (b, 0)),
                  pl.BlockSpec((1, seq, 128), lambda b, i: (b, 0, 0)),
                  pl.BlockSpec((1, 1, D_FOX, tq), lambda b, i: (b, i, 0, 0)),
                  pl.BlockSpec((1, nb, H_FOX * V_ROWS, tq), lambda b, i: (b, 0, 0, 0)),
                  pl.BlockSpec((tq, D_FOX), lambda b, i: (b * nb + i, 1))],
        out_specs=pl.BlockSpec((tq, D_FOX), lambda b, i: (b * nb + i, 0)),
        scratch_shapes=[pltpu.VMEM((H_FOX, 2 * HEAD_DIM, tq), BF16)] + _attention_scratch(H_FOX, tq),
        compiler_params=_cparams(("parallel", "parallel"), VMEM_LIMIT_BYTES),
        name="fox_attention",
    )(h_plain, cb, fqt, fvt, h_plain)


def _dsa_kernel(dk_ref, dvt_ref, ik_ref, dqt_ref, iqt_ref, iwt_ref, g_ref, o_ref, keys_ref, *scratch, topk):
    qi = pl.program_id(1)
    tq = o_ref.shape[0]
    tk = tq
    nkb = qi + 1
    row = lax.broadcasted_iota(jnp.int32, (tk, tq), 0)
    col = lax.broadcasted_iota(jnp.int32, (tk, tq), 1)

    def score_block(kb):
        start = pl.multiple_of(kb * tk, tk)
        ikb = ik_ref[0, pl.ds(start, tk), :]
        acc = jnp.zeros((tk, tq), F32)
        for h in range(IDX_HEADS):
            z = jnp.dot(ikb, iqt_ref[0, 0, h * IDX_DIM:(h + 1) * IDX_DIM, :], preferred_element_type=F32)
            acc = acc + jnp.maximum(z, 0.0) * iwt_ref[0, 0, h:h + 1, :]
        key = _sortable_key(lax.bitcast_convert_type(acc, jnp.int32))
        keys_ref[kb] = jnp.where((kb == qi) & (row > col), KEY_NEG, key)

    _grouped_loop(nkb, BLOCKS_PER_STEP, score_block)

    def count(pred):
        def body(kb, acc):
            hit = pred(keys_ref[kb], row + kb * tk).astype(jnp.int32)
            return acc + _partial_sums(hit)
        part = lax.fori_loop(0, nkb, body, jnp.zeros((COUNT_CHAINS * 8, tq), jnp.int32))
        return jnp.sum(part, axis=0, keepdims=True)

    n_nonneg = count(lambda k, _: k >= 0)
    nonneg = n_nonneg >= topk
    prefix0 = jnp.where(nonneg, 0, INT_MIN).astype(jnp.int32)
    n_all = jnp.full((1, tq), tk, jnp.int32) * nkb

    def bit_step(i, carry):
        prefix, n_ge = carry
        cand = prefix | (jnp.int32(1) << (30 - i))
        n = count(lambda k, _: k >= cand)
        keep = n >= topk
        return jnp.where(keep, cand, prefix), jnp.where(keep, n, n_ge)

    vstar, n_ge = lax.fori_loop(0, 31, bit_step, (prefix0, jnp.where(nonneg, n_nonneg, n_all)))

    tie = jnp.max(((n_ge > topk) & (vstar > KEY_NEG)).astype(jnp.int32)) > 0

    def tie_index(_):
        n_gt = count(lambda k, _: k > vstar)

        def idx_step(i, jx):
            cand = jx | (jnp.int32(1) << (29 - i))
            n = n_gt + count(lambda k, idx: (k == vstar) & (idx < cand))
            return jnp.where(n <= topk, cand, jx)
        return lax.fori_loop(0, 30, idx_step, jnp.zeros((1, tq), jnp.int32))

    def store_bias(kb, sel):
        keys_ref[kb] = lax.bitcast_convert_type(jnp.where(sel, 0.0, NEG).astype(F32), jnp.int32)

    def bias_with_ties(_):
        jx = tie_index(0)

        def body(kb, carry):
            k = keys_ref[kb]
            store_bias(kb, ((k > vstar) | ((k == vstar) & ((row + kb * tk) < jx))) & (k > KEY_NEG))
            return carry

        lax.fori_loop(0, nkb, body, 0)
        return 0

    def bias_without_ties(_):
        threshold = jnp.maximum(vstar, KEY_NEG + 1)

        def body(kb, carry):
            store_bias(kb, keys_ref[kb] >= threshold)
            return carry

        lax.fori_loop(0, nkb, body, 0)
        return 0

    lax.cond(tie, bias_with_ties, bias_without_ties, 0)

    def score_fn(kb, h):
        start = pl.multiple_of(kb * tk, tk)
        hs = slice(h * HEAD_DIM, (h + 1) * HEAD_DIM)
        s = jnp.dot(dk_ref[pl.ds(start, tk), hs], dqt_ref[0, 0, hs, :], preferred_element_type=F32)
        return s + lax.bitcast_convert_type(keys_ref[kb], F32)

    def values_fn(kb, h):
        return dvt_ref[0, kb, h * V_ROWS:(h + 1) * V_ROWS, :]

    _attention_core(H_DSA, qi, score_fn, values_fn, None, g_ref, o_ref, *scratch)


def _dsa_attention(dk, dvt, ik, dqt, iqt, iwt, h_plain, batch, seq, topk):
    tq = SEQ_TILE
    nb = seq // tq
    m = batch * seq
    return pl.pallas_call(
        functools.partial(_dsa_kernel, topk=topk),
        out_shape=jax.ShapeDtypeStruct((m, D_DSA), BF16),
        grid=(batch, nb),
        in_specs=[pl.BlockSpec((seq, D_DSA), lambda b, i: (b, 0)),
                  pl.BlockSpec((1, nb, H_DSA * V_ROWS, tq), lambda b, i: (b, 0, 0, 0)),
                  pl.BlockSpec((1, seq, IDX_DIM), lambda b, i: (b, 0, 0)),
                  pl.BlockSpec((1, 1, D_DSA, tq), lambda b, i: (b, i, 0, 0)),
                  pl.BlockSpec((1, 1, IDX_HEADS * IDX_DIM, tq), lambda b, i: (b, i, 0, 0)),
                  pl.BlockSpec((1, 1, IDX_HEADS, tq), lambda b, i: (b, i, 0, 0)),
                  pl.BlockSpec((tq, D_DSA), lambda b, i: (b * nb + i, 2))],
        out_specs=pl.BlockSpec((tq, D_DSA), lambda b, i: (b * nb + i, 0)),
        scratch_shapes=[pltpu.VMEM((nb, tq, tq), jnp.int32)] + _attention_scratch(H_DSA, tq),
        compiler_params=_cparams(("parallel", "parallel"), VMEM_LIMIT_BYTES),
        name="dsa_attention",
    )(dk, dvt, ik, dqt, iqt, iwt, h_plain)


def _mem_kernel(q0_ref, q1_ref, g0_ref, g1_ref, kv_ref, o_ref, *, scale):
    for h in range(H_MEM):
        q_ref = (q0_ref, q1_ref)[h // 2]
        g_ref = (g0_ref, g1_ref)[h // 2]
        ls = slice((h % 2) * HEAD_DIM, (h % 2 + 1) * HEAD_DIM)
        q = (q_ref[:, ls].astype(F32) * scale).astype(BF16)
        k = kv_ref[0, :, h * HEAD_DIM:(h + 1) * HEAD_DIM]
        v = kv_ref[0, :, D_MEMG + h * HEAD_DIM:D_MEMG + (h + 1) * HEAD_DIM]
        s = lax.dot_general(q, k, (((1,), (1,)), ((), ())), preferred_element_type=F32)
        m = jnp.max(s, axis=1, keepdims=True)
        p = jnp.exp(s - m)
        l = jnp.sum(p, axis=1, keepdims=True)
        o = jnp.dot(p.astype(BF16), v, preferred_element_type=F32) / l
        o_ref[:, h * HEAD_DIM:(h + 1) * HEAD_DIM] = (o * _silu(g_ref[:, ls].astype(F32))).astype(o_ref.dtype)


def _mem_attention(h_plain, kv, batch, seq, scale, q_col):
    tq = 2 * SEQ_TILE
    nb = seq // tq
    m = batch * seq
    n_mem = kv.shape[1]
    cb = q_col // 256
    spec = lambda c: pl.BlockSpec((tq, 256), lambda i: (i, c))
    return pl.pallas_call(
        functools.partial(_mem_kernel, scale=scale),
        out_shape=jax.ShapeDtypeStruct((m, D_MEMG), BF16),
        grid=(m // tq,),
        in_specs=[spec(cb), spec(cb + 1), spec(cb + 2), spec(cb + 3),
                  pl.BlockSpec((1, n_mem, 2 * D_MEMG), lambda i: (i // nb, 0, 0))],
        out_specs=pl.BlockSpec((tq, D_MEMG), lambda i: (i, 0)),
        compiler_params=_cparams(("parallel",)),
        name="mem_attention",
    )(h_plain, h_plain, h_plain, h_plain, kv)


def _out_ln_kernel(yf_ref, yd_ref, ym_ref, w_ref, x_ref, g_ref, b_ref, xo_ref, xb_ref, *, alpha):
    y = jnp.dot(yf_ref[...], w_ref[0:D_FOX, :], preferred_element_type=F32)
    y = y + jnp.dot(yd_ref[...], w_ref[D_FOX:D_FOX + D_DSA, :], preferred_element_type=F32)
    y = y + jnp.dot(ym_ref[...], w_ref[D_FOX + D_DSA:, :], preferred_element_type=F32)
    z = alpha * x_ref[...] + y
    mu = jnp.mean(z, axis=1, keepdims=True)
    zc = z - mu
    var = jnp.mean(zc * zc, axis=1, keepdims=True)
    out = zc * lax.rsqrt(var + LN_EPS) * g_ref[...] + b_ref[...]
    xo_ref[...] = out
    xb_ref[...] = out.astype(xb_ref.dtype)


def _out_ln(yf, yd, ym, w_out, x, gain, bias, alpha):
    m, d = x.shape
    tm = SEQ_TILE
    row = lambda n: pl.BlockSpec((tm, n), lambda i: (i, 0))
    const = lambda r, c: pl.BlockSpec((r, c), lambda i: (0, 0))
    return pl.pallas_call(
        functools.partial(_out_ln_kernel, alpha=alpha),
        out_shape=(jax.ShapeDtypeStruct((m, d), F32), jax.ShapeDtypeStruct((m, d), BF16)),
        grid=(m // tm,),
        in_specs=[row(D_FOX), row(D_DSA), row(D_MEMG), const(w_out.shape[0], d), row(d), const(1, d), const(1, d)],
        out_specs=(row(d), row(d)),
        compiler_params=_cparams(("parallel",), VMEM_LIMIT_BYTES),
        name="out_proj_layernorm",
    )(yf, yd, ym, w_out, x, gain, bias)


def _rope_tables(seq):
    pos = jnp.arange(seq).astype(F32)

    def tables(half):
        inv_freq = ROPE_THETA ** (-jnp.arange(half, dtype=F32) / half)
        ang = pos[:, None] * inv_freq[None, :]
        return jnp.cos(ang), jnp.sin(ang)

    c64, s64 = tables(HEAD_DIM // 2)
    c32, s32 = tables(IDX_DIM // 2)
    cos128 = jnp.concatenate([c64, c64], axis=1)
    sin128 = jnp.concatenate([-s64, s64], axis=1)
    zeros = jnp.zeros((seq, 128 - IDX_DIM), F32)
    cos_ik = jnp.concatenate([c32, c32, zeros], axis=1)
    sin_ik = jnp.concatenate([s32, s32, zeros], axis=1)
    return cos128, sin128, (c64.T, s64.T, c32.T, s32.T), cos_ik, sin_ik


def _prepare(seq, d_model, w_in, b_forget, w_mem_kv, w_out):
    depth = w_in.shape[0]
    splits = (D_FOX, D_FOX, D_FOX, D_FOX, H_FOX, D_DSA, D_DSA, D_DSA, D_DSA,
              IDX_HEADS * IDX_DIM, IDX_DIM, IDX_HEADS, D_MEMG, D_MEMG)
    offs = np.concatenate([[0], np.cumsum(splits)]).tolist()
    seg = lambda i: w_in[:, :, offs[i]:offs[i + 1]]
    (w_fq, w_fk, w_fv, w_fg, w_fl, w_dq, w_dk, w_dv, w_dg, w_iq, w_ik, w_iw, w_mq, w_mg) = [seg(i) for i in range(14)]
    pad = jnp.zeros((depth, d_model, 128 - IDX_DIM - IDX_HEADS - H_FOX), w_in.dtype)
    transposed = lambda ws: jnp.swapaxes(jnp.concatenate([w.astype(BF16) for w in ws], axis=2), 1, 2)
    cos128, sin128, t_tables, cos_ik, sin_ik = _rope_tables(seq)
    ts = SEQ_TILE
    return dict(
        w_plain=jnp.concatenate([w_fk, w_fg, w_dg, w_mq, w_mg], axis=2).astype(BF16),
        q_col=2 * D_FOX + D_DSA,
        w_dk=w_dk.astype(BF16),
        w_t_dsa=transposed([w_dq, w_dv, w_iq]),
        w_t_fox=transposed([w_fq, w_fv]),
        w_small=jnp.concatenate([w_ik, w_iw, w_fl, pad], axis=2).astype(BF16),
        w_out=w_out.astype(BF16),
        w_mem=w_mem_kv.astype(BF16),
        bf_rows=jnp.zeros((depth, 1, 128), F32).at[:, 0, OFF_FL:OFF_FL + H_FOX].set(b_forget.astype(F32)),
        cos128=cos128, sin128=sin128, t_tables=t_tables, cos_ik=cos_ik, sin_ik=sin_ik,
        tri=(jnp.arange(ts)[:, None] >= jnp.arange(ts)[None, :]).astype(BF16),
    )


def _mixers(xb, mem_b, l, p, batch, seq, n_mem):
    scale = HEAD_DIM ** -0.5
    topk = min(INDEX_TOPK, seq // 4)
    w_scale = (IDX_HEADS ** -0.5) * (IDX_DIM ** -0.5)
    kv = _proj_plain(mem_b, p["w_mem"][l], batch * n_mem, 2 * D_MEMG, "proj_mem_kv")
    kv = kv.reshape(batch, n_mem, 2 * D_MEMG)
    h_plain = _proj_plain(xb, p["w_plain"][l], 2 * SEQ_TILE, p["w_plain"].shape[2] // 2, "proj_plain")
    dk = _proj_rope(xb, p["w_dk"][l], p["cos128"], p["sin128"], seq, 2 * SEQ_TILE, "proj_dsa_key")
    dqt, dvt, iqt = _proj_t(xb, p["w_t_dsa"][l], p["t_tables"], batch, seq, scale * LOG2E,
                            (("rope_head_scaled", D_DSA), ("values", D_DSA), ("rope_index", IDX_HEADS * IDX_DIM)),
                            "proj_transposed_dsa")
    fqt, fvt = _proj_t(xb, p["w_t_fox"][l], p["t_tables"], batch, seq, scale * LOG2E,
                       (("scaled", D_FOX), ("values", D_FOX)), "proj_transposed_fox")
    ik, iwt, cb = _prep(xb, p["w_small"][l], p["bf_rows"][l], p["cos_ik"], p["sin_ik"], p["tri"], batch, seq, w_scale)
    y_fox = _fox_attention(h_plain, cb, fqt, fvt, batch, seq)
    y_dsa = _dsa_attention(dk, dvt, ik, dqt, iqt, iwt, h_plain, batch, seq, topk)
    y_mem = _mem_attention(h_plain, kv, batch, seq, scale, p["q_col"])
    return y_fox, y_dsa, y_mem


def kernel(x, mem, w_in, b_forget, w_mem_kv, w_out, ln_gain, ln_bias):
    batch, seq, d_model = x.shape
    depth = w_in.shape[0]
    n_mem = mem.shape[1]
    m = batch * seq
    alpha = (2 * depth) ** 0.25
    assert seq % (2 * SEQ_TILE) == 0 and min(INDEX_TOPK, seq // 4) <= SEQ_TILE

    p = _prepare(seq, d_model, w_in, b_forget, w_mem_kv, w_out)
    mem_b = mem.reshape(batch * n_mem, d_model).astype(BF16)
    xf = x.reshape(m, d_model).astype(F32)
    xb = xf.astype(BF16)
    for l in range(depth):
        y_fox, y_dsa, y_mem = _mixers(xb, mem_b, l, p, batch, seq, n_mem)
        xf, xb = _out_ln(y_fox, y_dsa, y_mem, p["w_out"][l], xf, ln_gain[l][None, :].astype(F32),
                         ln_bias[l][None, :].astype(F32), alpha)
    return xf.reshape(batch, seq, d_model).astype(x.dtype)
```

```python
import functools

import numpy as np
import jax
import jax.numpy as jnp
from jax import lax
from jax.experimental import pallas as pl
from jax.experimental.pallas import tpu as pltpu

HEAD_DIM = 128
H_FOX = 6
H_DSA = 6
H_MEM = 4
D_FOX = H_FOX * HEAD_DIM
D_DSA = H_DSA * HEAD_DIM
D_MEMG = H_MEM * HEAD_DIM
IDX_HEADS = 16
IDX_DIM = 64
INDEX_TOPK = 256
ROPE_THETA = 10000.0
LN_EPS = 1e-5
NEG = -1e30
M_INIT = -1e29
LOG2E = float(np.log2(np.e))

SEQ_TILE = 256
PREP_TILE = 2 * SEQ_TILE
VMEM_LIMIT_BYTES = 56 * 1024 * 1024
OFF_IW = IDX_DIM
OFF_FL = IDX_DIM + IDX_HEADS
BIAS_PIECES = 3
BLOCKS_PER_STEP = 4
ONES_ROWS = 16
V_ROWS = HEAD_DIM + ONES_ROWS

F32 = jnp.float32
BF16 = jnp.bfloat16


def _sortable_key(bits):
    return bits ^ ((bits >> 31) & 0x7FFFFFFF)


_NEG_BITS = int(np.array(NEG, np.float32).view(np.int32))
KEY_NEG = int(np.int32(_NEG_BITS) ^ np.int32(0x7FFFFFFF))
INT_MIN = -(2 ** 31)


def _cparams(sem, vmem=None):
    return pltpu.CompilerParams(dimension_semantics=sem, vmem_limit_bytes=vmem)


def _silu(g):
    return g / (1.0 + jnp.exp(-g))


COUNT_CHAINS = 4


def _partial_sums(x):
    rows, lanes = x.shape
    grouped = x.reshape(rows // (COUNT_CHAINS * 8), COUNT_CHAINS * 8, lanes)
    return jnp.sum(grouped, axis=0)


def _grouped_loop(n, width, body):
    def group_step(w):
        def step(g, carry):
            for j in range(w):
                body(g * w + j)
            return carry
        return step

    lax.fori_loop(0, n // width, group_step(width), 0)
    w = width // 2
    while w >= 1:
        @pl.when((n & w) != 0)
        def _(w=w):
            group_step(w)((n // (2 * w)) * 2, 0)
        w //= 2


def _split_bf16(x, pieces):
    out = []
    r = x
    for _ in range(pieces - 1):
        p = r.astype(BF16)
        out.append(p)
        r = r - p.astype(F32)
    out.append(r.astype(BF16))
    return out


def _proj_plain_kernel(x_ref, w_ref, o_ref):
    o_ref[...] = jnp.dot(x_ref[...], w_ref[...], preferred_element_type=F32).astype(o_ref.dtype)


def _proj_plain(xb, w, tm, tn, name):
    m, d = xb.shape
    n = w.shape[1]
    return pl.pallas_call(
        _proj_plain_kernel,
        out_shape=jax.ShapeDtypeStruct((m, n), BF16),
        grid=(n // tn, m // tm),
        in_specs=[pl.BlockSpec((tm, d), lambda j, i: (i, 0)),
                  pl.BlockSpec((d, tn), lambda j, i: (0, j))],
        out_specs=pl.BlockSpec((tm, tn), lambda j, i: (i, j)),
        compiler_params=_cparams(("parallel", "parallel"), VMEM_LIMIT_BYTES),
        name=name,
    )(xb, w)


def _proj_rope_kernel(x_ref, w_ref, cos_ref, sin_ref, o_ref, *, heads):
    y = jnp.dot(x_ref[...], w_ref[...], preferred_element_type=F32)
    c = cos_ref[...]
    s = sin_ref[...]
    for h in range(heads):
        yh = y[:, h * HEAD_DIM:(h + 1) * HEAD_DIM]
        rot = pltpu.roll(yh, HEAD_DIM // 2, axis=1)
        o_ref[:, h * HEAD_DIM:(h + 1) * HEAD_DIM] = (yh * c + rot * s).astype(o_ref.dtype)


def _proj_rope(xb, w, cos128, sin128, seq, tm, name):
    m, d = xb.shape
    n = w.shape[1]
    nseq = seq // tm
    return pl.pallas_call(
        functools.partial(_proj_rope_kernel, heads=n // HEAD_DIM),
        out_shape=jax.ShapeDtypeStruct((m, n), BF16),
        grid=(m // tm,),
        in_specs=[pl.BlockSpec((tm, d), lambda i: (i, 0)),
                  pl.BlockSpec((d, n), lambda i: (0, 0)),
                  pl.BlockSpec((tm, HEAD_DIM), lambda i: (i % nseq, 0)),
                  pl.BlockSpec((tm, HEAD_DIM), lambda i: (i % nseq, 0))],
        out_specs=pl.BlockSpec((tm, n), lambda i: (i, 0)),
        compiler_params=_cparams(("parallel",), VMEM_LIMIT_BYTES),
        name=name,
    )(xb, w, cos128, sin128)


def _proj_t_kernel(x_ref, w_ref, cos64_ref, sin64_ref, cos32_ref, sin32_ref, *out_refs, groups, scale):
    yt = jnp.dot(x_ref[...], w_ref[...], preferred_element_type=F32).T
    base = 0
    for (kind, rows), o_ref in zip(groups, out_refs):
        y = yt[base:base + rows]
        base += rows
        if kind == "values":
            ones = jnp.ones((ONES_ROWS, y.shape[1]), o_ref.dtype)
            for h in range(rows // HEAD_DIM):
                o_ref[0, 0, h * V_ROWS:h * V_ROWS + HEAD_DIM, :] = y[h * HEAD_DIM:(h + 1) * HEAD_DIM].astype(o_ref.dtype)
                o_ref[0, 0, h * V_ROWS + HEAD_DIM:(h + 1) * V_ROWS, :] = ones
        elif kind == "scaled":
            o_ref[0, 0] = (y * scale).astype(o_ref.dtype)
        else:
            dim, cos_t, sin_t, mul = ((HEAD_DIM, cos64_ref[...], sin64_ref[...], scale) if kind == "rope_head_scaled"
                                      else (IDX_DIM, cos32_ref[...], sin32_ref[...], 1.0))
            half = dim // 2
            for h in range(rows // dim):
                x1 = y[h * dim:h * dim + half]
                x2 = y[h * dim + half:(h + 1) * dim]
                o_ref[0, 0, h * dim:h * dim + half, :] = ((x1 * cos_t - x2 * sin_t) * mul).astype(o_ref.dtype)
                o_ref[0, 0, h * dim + half:(h + 1) * dim, :] = ((x2 * cos_t + x1 * sin_t) * mul).astype(o_ref.dtype)


def _proj_t(xb, w, tables, batch, seq, scale, groups, name):
    m, d = xb.shape
    ts = SEQ_TILE
    nb = seq // ts
    rows = w.shape[1]
    assert rows == sum(r for _, r in groups)
    out_rows = [r // HEAD_DIM * V_ROWS if kind == "values" else r for kind, r in groups]
    blk = lambda r: pl.BlockSpec((1, 1, r, ts), lambda i: (i // nb, i % nb, 0, 0))
    tab = lambda r: pl.BlockSpec((r, ts), lambda i: (0, i % nb))
    return pl.pallas_call(
        functools.partial(_proj_t_kernel, groups=groups, scale=scale),
        out_shape=tuple(jax.ShapeDtypeStruct((batch, nb, r, ts), BF16) for r in out_rows),
        grid=(m // ts,),
        in_specs=[pl.BlockSpec((ts, d), lambda i: (i, 0)),
                  pl.BlockSpec((d, rows), lambda i: (0, 0)),
                  tab(HEAD_DIM // 2), tab(HEAD_DIM // 2), tab(IDX_DIM // 2), tab(IDX_DIM // 2)],
        out_specs=tuple(blk(r) for r in out_rows),
        compiler_params=_cparams(("parallel",), VMEM_LIMIT_BYTES),
        name=name,
    )(xb, w, *tables)


def _prep_kernel(x_ref, w_ref, bf_ref, cos_ref, sin_ref, tri_ref, ik_ref, iwt_ref, cb_ref, carry_ref, *, w_scale):
    j = pl.program_id(1)
    raw = jnp.dot(x_ref[...], w_ref[...], preferred_element_type=F32)
    ts = raw.shape[0]

    lane = lax.broadcasted_iota(jnp.int32, raw.shape, 1)
    q = IDX_DIM // 2
    rot = jnp.where(lane < q, -pltpu.roll(raw, 128 - q, axis=1), pltpu.roll(raw, q, axis=1))
    ik = raw * cos_ref[...] + rot * sin_ref[...]
    ik_ref[0] = ik[:, :IDX_DIM].astype(ik_ref.dtype)

    iw_t = raw.T[OFF_IW:OFF_IW + IDX_HEADS] * w_scale
    for t in range(ts // SEQ_TILE):
        iwt_ref[0, t] = iw_t[:, t * SEQ_TILE:(t + 1) * SEQ_TILE]

    z = raw + bf_ref[...]
    logf = jnp.minimum(z, 0.0) - jnp.log(1.0 + jnp.exp(-jnp.abs(z)))
    tri = tri_ref[...]
    cs = sum(jnp.dot(tri, p, preferred_element_type=F32) for p in _split_bf16(logf, 3))

    @pl.when(j == 0)
    def _():
        carry_ref[...] = jnp.zeros_like(carry_ref)

    cs = cs + carry_ref[...]
    carry_ref[...] = cs[ts - 1:ts, :]

    r = lax.broadcasted_iota(jnp.int32, (128, 128), 0)
    c = lax.broadcasted_iota(jnp.int32, (128, 128), 1)
    head_row = (r >= OFF_FL) & (r < OFF_FL + H_FOX)
    cb = jnp.zeros((ts, 128), F32)
    for k, piece in enumerate(_split_bf16(-LOG2E * cs, BIAS_PIECES)):
        place = (head_row & (c == BIAS_PIECES * (r - OFF_FL) + k)).astype(BF16)
        cb = cb + jnp.dot(piece, place, preferred_element_type=F32)
    cb_ref[0] = cb.astype(cb_ref.dtype)


def _prep(xb, w_small, bf_row, cos_ik, sin_ik, tri, batch, seq, w_scale):
    m, d = xb.shape
    ts = PREP_TILE
    nb = seq // ts
    sub = ts // SEQ_TILE
    out_shapes = (jax.ShapeDtypeStruct((batch, seq, IDX_DIM), BF16),
                  jax.ShapeDtypeStruct((batch, seq // SEQ_TILE, IDX_HEADS, SEQ_TILE), F32),
                  jax.ShapeDtypeStruct((batch, seq, 128), BF16))
    return pl.pallas_call(
        functools.partial(_prep_kernel, w_scale=w_scale),
        out_shape=out_shapes,
        grid=(batch, nb),
        in_specs=[pl.BlockSpec((ts, d), lambda b, j: (b * nb + j, 0)),
                  pl.BlockSpec((d, 128), lambda b, j: (0, 0)),
                  pl.BlockSpec((1, 128), lambda b, j: (0, 0)),
                  pl.BlockSpec((ts, 128), lambda b, j: (j, 0)),
                  pl.BlockSpec((ts, 128), lambda b, j: (j, 0)),
                  pl.BlockSpec((ts, ts), lambda b, j: (0, 0))],
        out_specs=(pl.BlockSpec((1, ts, IDX_DIM), lambda b, j: (b, j, 0)),
                   pl.BlockSpec((1, sub, IDX_HEADS, SEQ_TILE), lambda b, j: (b, j, 0, 0)),
                   pl.BlockSpec((1, ts, 128), lambda b, j: (b, j, 0))),
        scratch_shapes=[pltpu.VMEM((1, 128), F32)],
        compiler_params=_cparams(("parallel", "arbitrary")),
        name="prep_small",
    )(xb, w_small, bf_row, cos_ik, sin_ik, tri)


def _attention_scratch(heads, tq):
    return [pltpu.VMEM((heads, 1, tq), F32),
            pltpu.VMEM((heads, 1, tq), F32),
            pltpu.VMEM((heads, HEAD_DIM, tq), F32),
            pltpu.VMEM((heads, 1, tq), F32),
            pltpu.VMEM((heads, tq, tq), BF16)]


def _attention_core(heads, qi, score_fn, values_fn, diag_mask, g_ref, o_ref,
                    m_ref, l_ref, acc_ref, a_ref, p_ref):
    m_ref[...] = jnp.full(m_ref.shape, M_INIT, F32)
    l_ref[...] = jnp.zeros(l_ref.shape, F32)
    acc_ref[...] = jnp.zeros(acc_ref.shape, F32)
    a_ref[...] = jnp.ones(a_ref.shape, F32)
    p_ref[...] = jnp.zeros(p_ref.shape, BF16)

    def values(kb, h):
        pv = jnp.dot(values_fn(kb, h), p_ref[h], preferred_element_type=F32)
        a = a_ref[h]
        acc_ref[h] = a * acc_ref[h] + pv[:HEAD_DIM]
        l_ref[h] = a * l_ref[h] + pv[HEAD_DIM:HEAD_DIM + 1]

    def score_softmax(kb, h, last):
        s = score_fn(kb, h)
        if last and diag_mask is not None:
            s = diag_mask(s)
        m = m_ref[h]
        m_new = jnp.maximum(m, jnp.max(s, axis=0, keepdims=True))
        a = jnp.exp2(m - m_new)
        p = jnp.exp2(s - m_new)
        m_ref[h] = m_new
        a_ref[h] = a
        p_ref[h] = p.astype(BF16)

    def block(kb, last):
        prev = jnp.maximum(kb - 1, 0)
        for h in range(heads):
            values(prev, h)
            score_softmax(kb, h, last)

    _grouped_loop(qi, BLOCKS_PER_STEP, lambda kb: block(kb, False))
    block(qi, True)
    for h in range(heads):
        values(qi, h)
    for h in range(heads):
        hs = slice(h * HEAD_DIM, (h + 1) * HEAD_DIM)
        o = (acc_ref[h] / l_ref[h]).T
        o_ref[:, hs] = (o * _silu(g_ref[:, hs].astype(F32))).astype(o_ref.dtype)


def _fox_kernel(k_ref, cb_ref, qt_ref, vt_ref, g_ref, o_ref, qa_ref, *scratch):
    qi = pl.program_id(1)
    tq = o_ref.shape[0]
    tk = tq
    row = lax.broadcasted_iota(jnp.int32, (tk, tq), 0)
    col = lax.broadcasted_iota(jnp.int32, (tk, tq), 1)
    sel_row = lax.broadcasted_iota(jnp.int32, (HEAD_DIM, tq), 0)
    for h in range(H_FOX):
        qa_ref[h, 0:HEAD_DIM, :] = qt_ref[0, 0, h * HEAD_DIM:(h + 1) * HEAD_DIM, :]
        pick = (sel_row >= BIAS_PIECES * h) & (sel_row < BIAS_PIECES * (h + 1))
        qa_ref[h, HEAD_DIM:2 * HEAD_DIM, :] = pick.astype(BF16)

    def score_fn(kb, h):
        start = pl.multiple_of(kb * tk, tk)
        keys = jnp.concatenate([k_ref[pl.ds(start, tk), h * HEAD_DIM:(h + 1) * HEAD_DIM],
                                cb_ref[0, pl.ds(start, tk), :]], axis=1)
        return jnp.dot(keys, qa_ref[h], preferred_element_type=F32)

    def values_fn(kb, h):
        return vt_ref[0, kb, h * V_ROWS:(h + 1) * V_ROWS, :]

    def diag_mask(s):
        return jnp.where(row <= col, s, NEG)

    _attention_core(H_FOX, qi, score_fn, values_fn, diag_mask, g_ref, o_ref, *scratch)


def _fox_attention(h_plain, cb, fqt, fvt, batch, seq):
    tq = SEQ_TILE
    nb = seq // tq
    m = batch * seq
    return pl.pallas_call(
        _fox_kernel,
        out_shape=jax.ShapeDtypeStruct((m, D_FOX), BF16),
        grid=(batch, nb),
        in_specs=[pl.BlockSpec((seq, D_FOX), lambda b, i: (b, 0)),
                  pl.BlockSpec((1, seq, 128), lambda b, i: (b, 0, 0)),
                  pl.BlockSpec((1, 1, D_FOX, tq), lambda b, i: (b, i, 0, 0)),
                  pl.BlockSpec((1, nb, H_FOX * V_ROWS, tq), lambda b, i: (b, 0, 0, 0)),
                  pl.BlockSpec((tq, D_FOX), lambda b, i: (b * nb + i, 1))],
        out_specs=pl.BlockSpec((tq, D_FOX), lambda b, i: (b * nb + i, 0)),
        scratch_shapes=[pltpu.VMEM((H_FOX, 2 * HEAD_DIM, tq), BF16)] + _attention_scratch(H_FOX, tq),
        compiler_params=_cparams(("parallel", "parallel"), VMEM_LIMIT_BYTES),
        name="fox_attention",
    )(h_plain, cb, fqt, fvt, h_plain)


def _dsa_kernel(dk_ref, dvt_ref, ik_ref, dqt_ref, iqt_ref, iwt_ref, g_ref, o_ref, keys_ref, *scratch, topk):
    qi = pl.program_id(1)
    tq = o_ref.shape[0]
    tk = tq
    nkb = qi + 1
    row = lax.broadcasted_iota(jnp.int32, (tk, tq), 0)
    col = lax.broadcasted_iota(jnp.int32, (tk, tq), 1)

    def score_block(kb):
        start = pl.multiple_of(kb * tk, tk)
        ikb = ik_ref[0, pl.ds(start, tk), :]
        acc = jnp.zeros((tk, tq), F32)
        for h in range(IDX_HEADS):
            z = jnp.dot(ikb, iqt_ref[0, 0, h * IDX_DIM:(h + 1) * IDX_DIM, :], preferred_element_type=F32)
            acc = acc + jnp.maximum(z, 0.0) * iwt_ref[0, 0, h:h + 1, :]
        key = _sortable_key(lax.bitcast_convert_type(acc, jnp.int32))
        keys_ref[kb] = jnp.where((kb == qi) & (row > col), KEY_NEG, key)

    _grouped_loop(nkb, BLOCKS_PER_STEP, score_block)

    def count(pred):
        def body(kb, acc):
            hit = pred(keys_ref[kb], row + kb * tk).astype(jnp.int32)
            return acc + _partial_sums(hit)
        part = lax.fori_loop(0, nkb, body, jnp.zeros((COUNT_CHAINS * 8, tq), jnp.int32))
        return jnp.sum(part, axis=0, keepdims=True)

    n_nonneg = count(lambda k, _: k >= 0)
    nonneg = n_nonneg >= topk
    prefix0 = jnp.where(nonneg, 0, INT_MIN).astype(jnp.int32)
    n_all = jnp.full((1, tq), tk, jnp.int32) * nkb

    def bit_step(i, carry):
        prefix, n_ge = carry
        cand = prefix | (jnp.int32(1) << (30 - i))
        n = count(lambda k, _: k >= cand)
        keep = n >= topk
        return jnp.where(keep, cand, prefix), jnp.where(keep, n, n_ge)

    vstar, n_ge = lax.fori_loop(0, 31, bit_step, (prefix0, jnp.where(nonneg, n_nonneg, n_all)))

    tie = jnp.max(((n_ge > topk) & (vstar > KEY_NEG)).astype(jnp.int32)) > 0

    def tie_index(_):
        n_gt = count(lambda k, _: k > vstar)

        def idx_step(i, jx):
            cand = jx | (jnp.int32(1) << (29 - i))
            n = n_gt + count(lambda k, idx: (k == vstar) & (idx < cand))
            return jnp.where(n <= topk, cand, jx)
        return lax.fori_loop(0, 30, idx_step, jnp.zeros((1, tq), jnp.int32))

    def store_bias(kb, sel):
        keys_ref[kb] = lax.bitcast_convert_type(jnp.where(sel, 0.0, NEG).astype(F32), jnp.int32)

    def bias_with_ties(_):
        jx = tie_index(0)

        def body(kb, carry):
            k = keys_ref[kb]
            store_bias(kb, ((k > vstar) | ((k == vstar) & ((row + kb * tk) < jx))) & (k > KEY_NEG))
            return carry

        lax.fori_loop(0, nkb, body, 0)
        return 0

    def bias_without_ties(_):
        threshold = jnp.maximum(vstar, KEY_NEG + 1)

        def body(kb, carry):
            store_bias(kb, keys_ref[kb] >= threshold)
            return carry

        lax.fori_loop(0, nkb, body, 0)
        return 0

    lax.cond(tie, bias_with_ties, bias_without_ties, 0)

    def score_fn(kb, h):
        start = pl.multiple_of(kb * tk, tk)
        hs = slice(h * HEAD_DIM, (h + 1) * HEAD_DIM)
        s = jnp.dot(dk_ref[pl.ds(start, tk), hs], dqt_ref[0, 0, hs, :], preferred_element_type=F32)
        return s + lax.bitcast_convert_type(keys_ref[kb], F32)

    def values_fn(kb, h):
        return dvt_ref[0, kb, h * V_ROWS:(h + 1) * V_ROWS, :]

    _attention_core(H_DSA, qi, score_fn, values_fn, None, g_ref, o_ref, *scratch)


def _dsa_attention(dk, dvt, ik, dqt, iqt, iwt, h_plain, batch, seq, topk):
    tq = SEQ_TILE
    nb = seq // tq
    m = batch * seq
    return pl.pallas_call(
        functools.partial(_dsa_kernel, topk=topk),
        out_shape=jax.ShapeDtypeStruct((m, D_DSA), BF16),
        grid=(batch, nb),
        in_specs=[pl.BlockSpec((seq, D_DSA), lambda b, i: (b, 0)),
                  pl.BlockSpec((1, nb, H_DSA * V_ROWS, tq), lambda b, i: (b, 0, 0, 0)),
                  pl.BlockSpec((1, seq, IDX_DIM), lambda b, i: (b, 0, 0)),
                  pl.BlockSpec((1, 1, D_DSA, tq), lambda b, i: (b, i, 0, 0)),
                  pl.BlockSpec((1, 1, IDX_HEADS * IDX_DIM, tq), lambda b, i: (b, i, 0, 0)),
                  pl.BlockSpec((1, 1, IDX_HEADS, tq), lambda b, i: (b, i, 0, 0)),
                  pl.BlockSpec((tq, D_DSA), lambda b, i: (b * nb + i, 2))],
        out_specs=pl.BlockSpec((tq, D_DSA), lambda b, i: (b * nb + i, 0)),
        scratch_shapes=[pltpu.VMEM((nb, tq, tq), jnp.int32)] + _attention_scratch(H_DSA, tq),
        compiler_params=_cparams(("parallel", "parallel"), VMEM_LIMIT_BYTES),
        name="dsa_attention",
    )(dk, dvt, ik, dqt, iqt, iwt, h_plain)


def _mem_kernel(q0_ref, q1_ref, g0_ref, g1_ref, kv_ref, o_ref, *, scale):
    for h in range(H_MEM):
        q_ref = (q0_ref, q1_ref)[h // 2]
        g_ref = (g0_ref, g1_ref)[h // 2]
        ls = slice((h % 2) * HEAD_DIM, (h % 2 + 1) * HEAD_DIM)
        q = (q_ref[:, ls].astype(F32) * scale).astype(BF16)
        k = kv_ref[0, :, h * HEAD_DIM:(h + 1) * HEAD_DIM]
        v = kv_ref[0, :, D_MEMG + h * HEAD_DIM:D_MEMG + (h + 1) * HEAD_DIM]
        s = lax.dot_general(q, k, (((1,), (1,)), ((), ())), preferred_element_type=F32)
        m = jnp.max(s, axis=1, keepdims=True)
        p = jnp.exp(s - m)
        l = jnp.sum(p, axis=1, keepdims=True)
        o = jnp.dot(p.astype(BF16), v, preferred_element_type=F32) / l
        o_ref[:, h * HEAD_DIM:(h + 1) * HEAD_DIM] = (o * _silu(g_ref[:, ls].astype(F32))).astype(o_ref.dtype)


def _mem_attention(h_plain, kv, batch, seq, scale, q_col):
    tq = 2 * SEQ_TILE
    nb = seq // tq
    m = batch * seq
    n_mem = kv.shape[1]
    cb = q_col // 256
    spec = lambda c: pl.BlockSpec((tq, 256), lambda i: (i, c))
    return pl.pallas_call(
        functools.partial(_mem_kernel, scale=scale),
        out_shape=jax.ShapeDtypeStruct((m, D_MEMG), BF16),
        grid=(m // tq,),
        in_specs=[spec(cb), spec(cb + 1), spec(cb + 2), spec(cb + 3),
                  pl.BlockSpec((1, n_mem, 2 * D_MEMG), lambda i: (i // nb, 0, 0))],
        out_specs=pl.BlockSpec((tq, D_MEMG), lambda i: (i, 0)),
        compiler_params=_cparams(("parallel",)),
        name="mem_attention",
    )(h_plain, h_plain, h_plain, h_plain, kv)


def _out_ln_kernel(yf_ref, yd_ref, ym_ref, w_ref, x_ref, g_ref, b_ref, xo_ref, xb_ref, *, alpha):
    y = jnp.dot(yf_ref[...], w_ref[0:D_FOX, :], preferred_element_type=F32)
    y = y + jnp.dot(yd_ref[...], w_ref[D_FOX:D_FOX + D_DSA, :], preferred_element_type=F32)
    y = y + jnp.dot(ym_ref[...], w_ref[D_FOX + D_DSA:, :], preferred_element_type=F32)
    z = alpha * x_ref[...] + y
    mu = jnp.mean(z, axis=1, keepdims=True)
    zc = z - mu
    var = jnp.mean(zc * zc, axis=1, keepdims=True)
    out = zc * lax.rsqrt(var + LN_EPS) * g_ref[...] + b_ref[...]
    xo_ref[...] = out
    xb_ref[...] = out.astype(xb_ref.dtype)


def _out_ln(yf, yd, ym, w_out, x, gain, bias, alpha):
    m, d = x.shape
    tm = SEQ_TILE
    row = lambda n: pl.BlockSpec((tm, n), lambda i: (i, 0))
    const = lambda r, c: pl.BlockSpec((r, c), lambda i: (0, 0))
    return pl.pallas_call(
        functools.partial(_out_ln_kernel, alpha=alpha),
        out_shape=(jax.ShapeDtypeStruct((m, d), F32), jax.ShapeDtypeStruct((m, d), BF16)),
        grid=(m // tm,),
        in_specs=[row(D_FOX), row(D_DSA), row(D_MEMG), const(w_out.shape[0], d), row(d), const(1, d), const(1, d)],
        out_specs=(row(d), row(d)),
        compiler_params=_cparams(("parallel",), VMEM_LIMIT_BYTES),
        name="out_proj_layernorm",
    )(yf, yd, ym, w_out, x, gain, bias)


def _rope_tables(seq):
    pos = np.arange(seq, dtype=np.float32)

    def tables(half):
        inv_freq = (np.float32(ROPE_THETA) ** (-np.arange(half, dtype=np.float32) / np.float32(half))).astype(np.float32)
        ang = pos[:, None] * inv_freq[None, :]
        return np.cos(ang).astype(np.float32), np.sin(ang).astype(np.float32)

    c64, s64 = tables(HEAD_DIM // 2)
    c32, s32 = tables(IDX_DIM // 2)
    cos128 = np.concatenate([c64, c64], axis=1)
    sin128 = np.concatenate([-s64, s64], axis=1)
    zeros = np.zeros((seq, 128 - IDX_DIM), np.float32)
    cos_ik = np.concatenate([c32, c32, zeros], axis=1)
    sin_ik = np.concatenate([s32, s32, zeros], axis=1)
    transposed = tuple(np.ascontiguousarray(t.T) for t in (c64, s64, c32, s32))
    return cos128, sin128, transposed, cos_ik, sin_ik


def _prepare(seq, d_model, w_in, b_forget, w_mem_kv, w_out):
    depth = w_in.shape[0]
    splits = (D_FOX, D_FOX, D_FOX, D_FOX, H_FOX, D_DSA, D_DSA, D_DSA, D_DSA,
              IDX_HEADS * IDX_DIM, IDX_DIM, IDX_HEADS, D_MEMG, D_MEMG)
    offs = np.concatenate([[0], np.cumsum(splits)]).tolist()
    seg = lambda i: w_in[:, :, offs[i]:offs[i + 1]]
    (w_fq, w_fk, w_fv, w_fg, w_fl, w_dq, w_dk, w_dv, w_dg, w_iq, w_ik, w_iw, w_mq, w_mg) = [seg(i) for i in range(14)]
    pad = jnp.zeros((depth, d_model, 128 - IDX_DIM - IDX_HEADS - H_FOX), w_in.dtype)
    joined = lambda ws: jnp.concatenate(ws, axis=2).astype(BF16)
    cos128, sin128, t_tables, cos_ik, sin_ik = _rope_tables(seq)
    ts = PREP_TILE
    return dict(
        w_plain=jnp.concatenate([w_fk, w_fg, w_dg, w_mq, w_mg], axis=2).astype(BF16),
        q_col=2 * D_FOX + D_DSA,
        w_dk=w_dk.astype(BF16),
        w_t_dsa=joined([w_dq, w_dv, w_iq]),
        w_t_fox=joined([w_fq, w_fv]),
        w_small=jnp.concatenate([w_ik, w_iw, w_fl, pad], axis=2).astype(BF16),
        w_out=w_out.astype(BF16),
        w_mem=w_mem_kv.astype(BF16),
        bf_rows=jnp.zeros((depth, 1, 128), F32).at[:, 0, OFF_FL:OFF_FL + H_FOX].set(b_forget.astype(F32)),
        cos128=cos128, sin128=sin128, t_tables=t_tables, cos_ik=cos_ik, sin_ik=sin_ik,
        tri=jnp.asarray(np.tril(np.ones((ts, ts), np.float32)), BF16),
    )


def _mixers(xb, mem_b, l, p, batch, seq, n_mem):
    scale = HEAD_DIM ** -0.5
    topk = min(INDEX_TOPK, seq // 4)
    w_scale = (IDX_HEADS ** -0.5) * (IDX_DIM ** -0.5)
    kv = _proj_plain(mem_b, p["w_mem"][l], batch * n_mem, 2 * D_MEMG, "proj_mem_kv")
    kv = kv.reshape(batch, n_mem, 2 * D_MEMG)
    h_plain = _proj_plain(xb, p["w_plain"][l], 2 * SEQ_TILE, p["w_plain"].shape[2] // 2, "proj_plain")
    dk = _proj_rope(xb, p["w_dk"][l], p["cos128"], p["sin128"], seq, 2 * SEQ_TILE, "proj_dsa_key")
    dqt, dvt, iqt = _proj_t(xb, p["w_t_dsa"][l], p["t_tables"], batch, seq, scale * LOG2E,
                            (("rope_head_scaled", D_DSA), ("values", D_DSA), ("rope_index", IDX_HEADS * IDX_DIM)),
                            "proj_transposed_dsa")
    fqt, fvt = _proj_t(xb, p["w_t_fox"][l], p["t_tables"], batch, seq, scale * LOG2E,
                       (("scaled", D_FOX), ("values", D_FOX)), "proj_transposed_fox")
    ik, iwt, cb = _prep(xb, p["w_small"][l], p["bf_rows"][l], p["cos_ik"], p["sin_ik"], p["tri"], batch, seq, w_scale)
    y_fox = _fox_attention(h_plain, cb, fqt, fvt, batch, seq)
    y_dsa = _dsa_attention(dk, dvt, ik, dqt, iqt, iwt, h_plain, batch, seq, topk)
    y_mem = _mem_attention(h_plain, kv, batch, seq, scale, p["q_col"])
    return y_fox, y_dsa, y_mem


def kernel(x, mem, w_in, b_forget, w_mem_kv, w_out, ln_gain, ln_bias):
    batch, seq, d_model = x.shape
    depth = w_in.shape[0]
    n_mem = mem.shape[1]
    m = batch * seq
    alpha = (2 * depth) ** 0.25
    assert seq % (2 * SEQ_TILE) == 0 and min(INDEX_TOPK, seq // 4) <= SEQ_TILE

    p = _prepare(seq, d_model, w_in, b_forget, w_mem_kv, w_out)
    mem_b = mem.reshape(batch * n_mem, d_model).astype(BF16)
    xf = x.reshape(m, d_model).astype(F32)
    xb = xf.astype(BF16)
    for l in range(depth):
        y_fox, y_dsa, y_mem = _mixers(xb, mem_b, l, p, batch, seq, n_mem)
        xf, xb = _out_ln(y_fox, y_dsa, y_mem, p["w_out"][l], xf, ln_gain[l][None, :].astype(F32),
                         ln_bias[l][None, :].astype(F32), alpha)
    return xf.reshape(batch, seq, d_model).astype(x.dtype)
```

```python
import functools

import numpy as np
import jax
import jax.numpy as jnp
from jax import lax
from jax.experimental import pallas as pl
from jax.experimental.pallas import tpu as pltpu

HEAD_DIM = 128
H_FOX = 6
H_DSA = 6
H_MEM = 4
D_FOX = H_FOX * HEAD_DIM
D_DSA = H_DSA * HEAD_DIM
D_MEMG = H_MEM * HEAD_DIM
IDX_HEADS = 16
IDX_DIM = 64
INDEX_TOPK = 256
ROPE_THETA = 10000.0
LN_EPS = 1e-5
NEG = -1e30
M_INIT = -1e29
LOG2E = float(np.log2(np.e))

SEQ_TILE = 256
PREP_TILE = 2 * SEQ_TILE
VMEM_LIMIT_BYTES = 56 * 1024 * 1024
OFF_IW = IDX_DIM
OFF_FL = IDX_DIM + IDX_HEADS
BIAS_PIECES = 3
BLOCKS_PER_STEP = 4
ONES_ROWS = 16
V_ROWS = HEAD_DIM + ONES_ROWS

F32 = jnp.float32
BF16 = jnp.bfloat16


def _sortable_key(bits):
    return bits ^ ((bits >> 31) & 0x7FFFFFFF)


_NEG_BITS = int(np.array(NEG, np.float32).view(np.int32))
KEY_NEG = int(np.int32(_NEG_BITS) ^ np.int32(0x7FFFFFFF))
INT_MIN = -(2 ** 31)


def _cparams(sem, vmem=None):
    return pltpu.CompilerParams(dimension_semantics=sem, vmem_limit_bytes=vmem)


def _silu(g):
    return g / (1.0 + jnp.exp(-g))


COUNT_CHAINS = 4


def _partial_sums(x):
    rows, lanes = x.shape
    grouped = x.reshape(rows // (COUNT_CHAINS * 8), COUNT_CHAINS * 8, lanes)
    return jnp.sum(grouped, axis=0)


def _grouped_loop(n, width, body):
    def group_step(w):
        def step(g, carry):
            for j in range(w):
                body(g * w + j)
            return carry
        return step

    lax.fori_loop(0, n // width, group_step(width), 0)
    w = width // 2
    while w >= 1:
        @pl.when((n & w) != 0)
        def _(w=w):
            group_step(w)((n // (2 * w)) * 2, 0)
        w //= 2


def _split_bf16(x, pieces):
    out = []
    r = x
    for _ in range(pieces - 1):
        p = r.astype(BF16)
        out.append(p)
        r = r - p.astype(F32)
    out.append(r.astype(BF16))
    return out


def _proj_plain_kernel(x_ref, w_ref, o_ref):
    o_ref[...] = jnp.dot(x_ref[...], w_ref[...], preferred_element_type=F32).astype(o_ref.dtype)


def _proj_plain(xb, w, tm, tn, name):
    m, d = xb.shape
    n = w.shape[1]
    return pl.pallas_call(
        _proj_plain_kernel,
        out_shape=jax.ShapeDtypeStruct((m, n), BF16),
        grid=(n // tn, m // tm),
        in_specs=[pl.BlockSpec((tm, d), lambda j, i: (i, 0)),
                  pl.BlockSpec((d, tn), lambda j, i: (0, j))],
        out_specs=pl.BlockSpec((tm, tn), lambda j, i: (i, j)),
        compiler_params=_cparams(("parallel", "parallel"), VMEM_LIMIT_BYTES),
        name=name,
    )(xb, w)


def _proj_rope_kernel(x_ref, w_ref, cos_ref, sin_ref, o_ref, *, heads):
    y = jnp.dot(x_ref[...], w_ref[...], preferred_element_type=F32)
    c = cos_ref[...]
    s = sin_ref[...]
    for h in range(heads):
        yh = y[:, h * HEAD_DIM:(h + 1) * HEAD_DIM]
        rot = pltpu.roll(yh, HEAD_DIM // 2, axis=1)
        o_ref[:, h * HEAD_DIM:(h + 1) * HEAD_DIM] = (yh * c + rot * s).astype(o_ref.dtype)


def _proj_rope(xb, w, cos128, sin128, seq, tm, name):
    m, d = xb.shape
    n = w.shape[1]
    nseq = seq // tm
    return pl.pallas_call(
        functools.partial(_proj_rope_kernel, heads=n // HEAD_DIM),
        out_shape=jax.ShapeDtypeStruct((m, n), BF16),
        grid=(m // tm,),
        in_specs=[pl.BlockSpec((tm, d), lambda i: (i, 0)),
                  pl.BlockSpec((d, n), lambda i: (0, 0)),
                  pl.BlockSpec((tm, HEAD_DIM), lambda i: (i % nseq, 0)),
                  pl.BlockSpec((tm, HEAD_DIM), lambda i: (i % nseq, 0))],
        out_specs=pl.BlockSpec((tm, n), lambda i: (i, 0)),
        compiler_params=_cparams(("parallel",), VMEM_LIMIT_BYTES),
        name=name,
    )(xb, w, cos128, sin128)


def _proj_t_kernel(x_ref, w_ref, cos64_ref, sin64_ref, cos32_ref, sin32_ref, *out_refs, groups, scale):
    yt = jnp.dot(x_ref[...], w_ref[...], preferred_element_type=F32).T
    base = 0
    for (kind, rows), o_ref in zip(groups, out_refs):
        y = yt[base:base + rows]
        base += rows
        if kind == "values":
            ones = jnp.ones((ONES_ROWS, y.shape[1]), o_ref.dtype)
            for h in range(rows // HEAD_DIM):
                o_ref[0, 0, h * V_ROWS:h * V_ROWS + HEAD_DIM, :] = y[h * HEAD_DIM:(h + 1) * HEAD_DIM].astype(o_ref.dtype)
                o_ref[0, 0, h * V_ROWS + HEAD_DIM:(h + 1) * V_ROWS, :] = ones
        elif kind == "scaled":
            o_ref[0, 0] = (y * scale).astype(o_ref.dtype)
        else:
            dim, cos_t, sin_t, mul = ((HEAD_DIM, cos64_ref[...], sin64_ref[...], scale) if kind == "rope_head_scaled"
                                      else (IDX_DIM, cos32_ref[...], sin32_ref[...], 1.0))
            half = dim // 2
            for h in range(rows // dim):
                x1 = y[h * dim:h * dim + half]
                x2 = y[h * dim + half:(h + 1) * dim]
                o_ref[0, 0, h * dim:h * dim + half, :] = ((x1 * cos_t - x2 * sin_t) * mul).astype(o_ref.dtype)
                o_ref[0, 0, h * dim + half:(h + 1) * dim, :] = ((x2 * cos_t + x1 * sin_t) * mul).astype(o_ref.dtype)


def _proj_t(xb, w, tables, batch, seq, scale, groups, name):
    m, d = xb.shape
    ts = SEQ_TILE
    nb = seq // ts
    rows = w.shape[1]
    assert rows == sum(r for _, r in groups)
    out_rows = [r // HEAD_DIM * V_ROWS if kind == "values" else r for kind, r in groups]
    blk = lambda r: pl.BlockSpec((1, 1, r, ts), lambda i: (i // nb, i % nb, 0, 0))
    tab = lambda r: pl.BlockSpec((r, ts), lambda i: (0, i % nb))
    return pl.pallas_call(
        functools.partial(_proj_t_kernel, groups=groups, scale=scale),
        out_shape=tuple(jax.ShapeDtypeStruct((batch, nb, r, ts), BF16) for r in out_rows),
        grid=(m // ts,),
        in_specs=[pl.BlockSpec((ts, d), lambda i: (i, 0)),
                  pl.BlockSpec((d, rows), lambda i: (0, 0)),
                  tab(HEAD_DIM // 2), tab(HEAD_DIM // 2), tab(IDX_DIM // 2), tab(IDX_DIM // 2)],
        out_specs=tuple(blk(r) for r in out_rows),
        compiler_params=_cparams(("parallel",), VMEM_LIMIT_BYTES),
        name=name,
    )(xb, w, *tables)


def _prep_kernel(x_ref, w_ref, bf_ref, cos_ref, sin_ref, tri_ref, ik_ref, iwt_ref, cb_ref, carry_ref, *, w_scale):
    j = pl.program_id(1)
    raw = jnp.dot(x_ref[...], w_ref[...], preferred_element_type=F32)
    ts = raw.shape[0]

    lane = lax.broadcasted_iota(jnp.int32, raw.shape, 1)
    q = IDX_DIM // 2
    rot = jnp.where(lane < q, -pltpu.roll(raw, 128 - q, axis=1), pltpu.roll(raw, q, axis=1))
    ik = raw * cos_ref[...] + rot * sin_ref[...]
    ik_ref[0] = ik[:, :IDX_DIM].astype(ik_ref.dtype)

    iw_t = raw.T[OFF_IW:OFF_IW + IDX_HEADS] * w_scale
    for t in range(ts // SEQ_TILE):
        iwt_ref[0, t] = iw_t[:, t * SEQ_TILE:(t + 1) * SEQ_TILE]

    z = raw + bf_ref[...]
    logf = jnp.minimum(z, 0.0) - jnp.log(1.0 + jnp.exp(-jnp.abs(z)))
    tri = tri_ref[...]
    cs = sum(jnp.dot(tri, p, preferred_element_type=F32) for p in _split_bf16(logf, 3))

    @pl.when(j == 0)
    def _():
        carry_ref[...] = jnp.zeros_like(carry_ref)

    cs = cs + carry_ref[...]
    carry_ref[...] = cs[ts - 1:ts, :]

    r = lax.broadcasted_iota(jnp.int32, (128, 128), 0)
    c = lax.broadcasted_iota(jnp.int32, (128, 128), 1)
    head_row = (r >= OFF_FL) & (r < OFF_FL + H_FOX)
    cb = jnp.zeros((ts, 128), F32)
    for k, piece in enumerate(_split_bf16(-LOG2E * cs, BIAS_PIECES)):
        place = (head_row & (c == BIAS_PIECES * (r - OFF_FL) + k)).astype(BF16)
        cb = cb + jnp.dot(piece, place, preferred_element_type=F32)
    cb_ref[0] = cb.astype(cb_ref.dtype)


def _prep(xb, w_small, bf_row, cos_ik, sin_ik, tri, batch, seq, w_scale):
    m, d = xb.shape
    ts = PREP_TILE
    nb = seq // ts
    sub = ts // SEQ_TILE
    out_shapes = (jax.ShapeDtypeStruct((batch, seq, IDX_DIM), BF16),
                  jax.ShapeDtypeStruct((batch, seq // SEQ_TILE, IDX_HEADS, SEQ_TILE), F32),
                  jax.ShapeDtypeStruct((batch, seq, 128), BF16))
    return pl.pallas_call(
        functools.partial(_prep_kernel, w_scale=w_scale),
        out_shape=out_shapes,
        grid=(batch, nb),
        in_specs=[pl.BlockSpec((ts, d), lambda b, j: (b * nb + j, 0)),
                  pl.BlockSpec((d, 128), lambda b, j: (0, 0)),
                  pl.BlockSpec((1, 128), lambda b, j: (0, 0)),
                  pl.BlockSpec((ts, 128), lambda b, j: (j, 0)),
                  pl.BlockSpec((ts, 128), lambda b, j: (j, 0)),
                  pl.BlockSpec((ts, ts), lambda b, j: (0, 0))],
        out_specs=(pl.BlockSpec((1, ts, IDX_DIM), lambda b, j: (b, j, 0)),
                   pl.BlockSpec((1, sub, IDX_HEADS, SEQ_TILE), lambda b, j: (b, j, 0, 0)),
                   pl.BlockSpec((1, ts, 128), lambda b, j: (b, j, 0))),
        scratch_shapes=[pltpu.VMEM((1, 128), F32)],
        compiler_params=_cparams(("parallel", "arbitrary")),
        name="prep_small",
    )(xb, w_small, bf_row, cos_ik, sin_ik, tri)


def _attention_scratch(heads, tq):
    return [pltpu.VMEM((heads, 1, tq), F32),
            pltpu.VMEM((heads, 1, tq), F32),
            pltpu.VMEM((heads, HEAD_DIM, tq), F32),
            pltpu.VMEM((heads, 1, tq), F32),
            pltpu.VMEM((heads, tq, tq), BF16)]


def _attention_core(heads, qi, score_fn, values_fn, diag_mask, g_ref, o_ref,
                    m_ref, l_ref, acc_ref, a_ref, p_ref):
    m_ref[...] = jnp.full(m_ref.shape, M_INIT, F32)
    l_ref[...] = jnp.zeros(l_ref.shape, F32)
    acc_ref[...] = jnp.zeros(acc_ref.shape, F32)
    a_ref[...] = jnp.ones(a_ref.shape, F32)
    p_ref[...] = jnp.zeros(p_ref.shape, BF16)

    def values(kb, h):
        pv = jnp.dot(values_fn(kb, h), p_ref[h], preferred_element_type=F32)
        a = a_ref[h]
        acc_ref[h] = a * acc_ref[h] + pv[:HEAD_DIM]
        l_ref[h] = a * l_ref[h] + pv[HEAD_DIM:HEAD_DIM + 1]

    def score_softmax(kb, h, last):
        s = score_fn(kb, h)
        if last and diag_mask is not None:
            s = diag_mask(s)
        m = m_ref[h]
        m_new = jnp.maximum(m, jnp.max(s, axis=0, keepdims=True))
        a = jnp.exp2(m - m_new)
        p = jnp.exp2(s - m_new)
        m_ref[h] = m_new
        a_ref[h] = a
        p_ref[h] = p.astype(BF16)

    def block(kb, last):
        prev = jnp.maximum(kb - 1, 0)
        for h in range(heads):
            values(prev, h)
            score_softmax(kb, h, last)

    _grouped_loop(qi, BLOCKS_PER_STEP, lambda kb: block(kb, False))
    block(qi, True)
    for h in range(heads):
        values(qi, h)
    for h in range(heads):
        hs = slice(h * HEAD_DIM, (h + 1) * HEAD_DIM)
        o = (acc_ref[h] / l_ref[h]).T
        o_ref[:, hs] = (o * _silu(g_ref[:, hs].astype(F32))).astype(o_ref.dtype)


def _fox_kernel(k_ref, cb_ref, qt_ref, vt_ref, g_ref, o_ref, qa_ref, *scratch):
    qi = pl.program_id(1)
    tq = o_ref.shape[0]
    tk = tq
    row = lax.broadcasted_iota(jnp.int32, (tk, tq), 0)
    col = lax.broadcasted_iota(jnp.int32, (tk, tq), 1)
    sel_row = lax.broadcasted_iota(jnp.int32, (HEAD_DIM, tq), 0)
    for h in range(H_FOX):
        qa_ref[h, 0:HEAD_DIM, :] = qt_ref[0, 0, h * HEAD_DIM:(h + 1) * HEAD_DIM, :]
        pick = (sel_row >= BIAS_PIECES * h) & (sel_row < BIAS_PIECES * (h + 1))
        qa_ref[h, HEAD_DIM:2 * HEAD_DIM, :] = pick.astype(BF16)

    def score_fn(kb, h):
        start = pl.multiple_of(kb * tk, tk)
        keys = jnp.concatenate([k_ref[pl.ds(start, tk), h * HEAD_DIM:(h + 1) * HEAD_DIM],
                                cb_ref[0, pl.ds(start, tk), :]], axis=1)
        return jnp.dot(keys, qa_ref[h], preferred_element_type=F32)

    def values_fn(kb, h):
        return vt_ref[0, kb, h * V_ROWS:(h + 1) * V_ROWS, :]

    def diag_mask(s):
        return jnp.where(row <= col, s, NEG)

    _attention_core(H_FOX, qi, score_fn, values_fn, diag_mask, g_ref, o_ref, *scratch)


def _fox_attention(h_plain, cb, fqt, fvt, batch, seq):
    tq = SEQ_TILE
    nb = seq // tq
    m = batch * seq
    return pl.pallas_call(
        _fox_kernel,
        out_shape=jax.ShapeDtypeStruct((m, D_FOX), BF16),
        grid=(batch, nb),
        in_specs=[pl.BlockSpec((seq, D_FOX), lambda b, i: (b, 0)),
                  pl.BlockSpec((1, seq, 128), lambda b, i: (b, 0, 0)),
                  pl.BlockSpec((1, 1, D_FOX, tq), lambda b, i: (b, i, 0, 0)),
                  pl.BlockSpec((1, nb, H_FOX * V_ROWS, tq), lambda b, i: (b, 0, 0, 0)),
                  pl.BlockSpec((tq, D_FOX), lambda b, i: (b * nb + i, 1))],
        out_specs=pl.BlockSpec((tq, D_FOX), lambda b, i: (b * nb + i, 0)),
        scratch_shapes=[pltpu.VMEM((H_FOX, 2 * HEAD_DIM, tq), BF16)] + _attention_scratch(H_FOX, tq),
        compiler_params=_cparams(("parallel", "parallel"), VMEM_LIMIT_BYTES),
        name="fox_attention",
    )(h_plain, cb, fqt, fvt, h_plain)


def _dsa_kernel(dk_ref, dvt_ref, ik_ref, dqt_ref, iqt_ref, iwt_ref, g_ref, o_ref, keys_ref, *scratch, topk):
    qi = pl.program_id(1)
    tq = o_ref.shape[0]
    tk = tq
    nkb = qi + 1
    row = lax.broadcasted_iota(jnp.int32, (tk, tq), 0)
    col = lax.broadcasted_iota(jnp.int32, (tk, tq), 1)

    def score_block(kb):
        start = pl.multiple_of(kb * tk, tk)
        ikb = ik_ref[0, pl.ds(start, tk), :]
        acc = jnp.zeros((tk, tq), F32)
        for h in range(IDX_HEADS):
            z = jnp.dot(ikb, iqt_ref[0, 0, h * IDX_DIM:(h + 1) * IDX_DIM, :], preferred_element_type=F32)
            acc = acc + jnp.maximum(z, 0.0) * iwt_ref[0, 0, h:h + 1, :]
        key = _sortable_key(lax.bitcast_convert_type(acc, jnp.int32))
        keys_ref[kb] = jnp.where((kb == qi) & (row > col), KEY_NEG, key)

    _grouped_loop(nkb, BLOCKS_PER_STEP, score_block)

    def count(pred):
        def body(kb, acc):
            hit = pred(keys_ref[kb], row + kb * tk).astype(jnp.int32)
            return acc + _partial_sums(hit)
        part = lax.fori_loop(0, nkb, body, jnp.zeros((COUNT_CHAINS * 8, tq), jnp.int32))
        return jnp.sum(part, axis=0, keepdims=True)

    n_nonneg = count(lambda k, _: k >= 0)
    nonneg = n_nonneg >= topk
    prefix0 = jnp.where(nonneg, 0, INT_MIN).astype(jnp.int32)
    n_all = jnp.full((1, tq), tk, jnp.int32) * nkb

    def bit_step(i, carry):
        prefix, n_ge = carry
        cand = prefix | (jnp.int32(1) << (30 - i))
        n = count(lambda k, _: k >= cand)
        keep = n >= topk
        return jnp.where(keep, cand, prefix), jnp.where(keep, n, n_ge)

    vstar, n_ge = lax.fori_loop(0, 31, bit_step, (prefix0, jnp.where(nonneg, n_nonneg, n_all)))

    tie = jnp.max(((n_ge > topk) & (vstar > KEY_NEG)).astype(jnp.int32)) > 0

    def tie_index(_):
        n_gt = count(lambda k, _: k > vstar)

        def idx_step(i, jx):
            cand = jx | (jnp.int32(1) << (29 - i))
            n = n_gt + count(lambda k, idx: (k == vstar) & (idx < cand))
            return jnp.where(n <= topk, cand, jx)
        return lax.fori_loop(0, 30, idx_step, jnp.zeros((1, tq), jnp.int32))

    def store_bias(kb, sel):
        keys_ref[kb] = lax.bitcast_convert_type(jnp.where(sel, 0.0, NEG).astype(F32), jnp.int32)

    def bias_with_ties(_):
        jx = tie_index(0)

        def body(kb, carry):
            k = keys_ref[kb]
            store_bias(kb, ((k > vstar) | ((k == vstar) & ((row + kb * tk) < jx))) & (k > KEY_NEG))
            return carry

        lax.fori_loop(0, nkb, body, 0)
        return 0

    def bias_without_ties(_):
        threshold = jnp.maximum(vstar, KEY_NEG + 1)

        def body(kb, carry):
            store_bias(kb, keys_ref[kb] >= threshold)
            return carry

        lax.fori_loop(0, nkb, body, 0)
        return 0

    lax.cond(tie, bias_with_ties, bias_without_ties, 0)

    def score_fn(kb, h):
        start = pl.multiple_of(kb * tk, tk)
        hs = slice(h * HEAD_DIM, (h + 1) * HEAD_DIM)
        s = jnp.dot(dk_ref[pl.ds(start, tk), hs], dqt_ref[0, 0, hs, :], preferred_element_type=F32)
        return s + lax.bitcast_convert_type(keys_ref[kb], F32)

    def values_fn(kb, h):
        return dvt_ref[0, kb, h * V_ROWS:(h + 1) * V_ROWS, :]

    _attention_core(H_DSA, qi, score_fn, values_fn, None, g_ref, o_ref, *scratch)


def _dsa_attention(dk, dvt, ik, dqt, iqt, iwt, h_plain, batch, seq, topk):
    tq = SEQ_TILE
    nb = seq // tq
    m = batch * seq
    return pl.pallas_call(
        functools.partial(_dsa_kernel, topk=topk),
        out_shape=jax.ShapeDtypeStruct((m, D_DSA), BF16),
        grid=(batch, nb),
        in_specs=[pl.BlockSpec((seq, D_DSA), lambda b, i: (b, 0)),
                  pl.BlockSpec((1, nb, H_DSA * V_ROWS, tq), lambda b, i: (b, 0, 0, 0)),
                  pl.BlockSpec((1, seq, IDX_DIM), lambda b, i: (b, 0, 0)),
                  pl.BlockSpec((1, 1, D_DSA, tq), lambda b, i: (b, i, 0, 0)),
                  pl.BlockSpec((1, 1, IDX_HEADS * IDX_DIM, tq), lambda b, i: (b, i, 0, 0)),
                  pl.BlockSpec((1, 1, IDX_HEADS, tq), lambda b, i: (b, i, 0, 0)),
                  pl.BlockSpec((tq, D_DSA), lambda b, i: (b * nb + i, 2))],
        out_specs=pl.BlockSpec((tq, D_DSA), lambda b, i: (b * nb + i, 0)),
        scratch_shapes=[pltpu.VMEM((nb, tq, tq), jnp.int32)] + _attention_scratch(H_DSA, tq),
        compiler_params=_cparams(("parallel", "parallel"), VMEM_LIMIT_BYTES),
        name="dsa_attention",
    )(dk, dvt, ik, dqt, iqt, iwt, h_plain)


def _mem_kernel(q0_ref, q1_ref, g0_ref, g1_ref, kv_ref, o_ref, *, scale):
    for h in range(H_MEM):
        q_ref = (q0_ref, q1_ref)[h // 2]
        g_ref = (g0_ref, g1_ref)[h // 2]
        ls = slice((h % 2) * HEAD_DIM, (h % 2 + 1) * HEAD_DIM)
        q = (q_ref[:, ls].astype(F32) * scale).astype(BF16)
        k = kv_ref[0, :, h * HEAD_DIM:(h + 1) * HEAD_DIM]
        v = kv_ref[0, :, D_MEMG + h * HEAD_DIM:D_MEMG + (h + 1) * HEAD_DIM]
        s = lax.dot_general(q, k, (((1,), (1,)), ((), ())), preferred_element_type=F32)
        m = jnp.max(s, axis=1, keepdims=True)
        p = jnp.exp(s - m)
        l = jnp.sum(p, axis=1, keepdims=True)
        o = jnp.dot(p.astype(BF16), v, preferred_element_type=F32) / l
        o_ref[:, h * HEAD_DIM:(h + 1) * HEAD_DIM] = (o * _silu(g_ref[:, ls].astype(F32))).astype(o_ref.dtype)


def _mem_attention(h_plain, kv, batch, seq, scale, q_col):
    tq = 2 * SEQ_TILE
    nb = seq // tq
    m = batch * seq
    n_mem = kv.shape[1]
    cb = q_col // 256
    spec = lambda c: pl.BlockSpec((tq, 256), lambda i: (i, c))
    return pl.pallas_call(
        functools.partial(_mem_kernel, scale=scale),
        out_shape=jax.ShapeDtypeStruct((m, D_MEMG), BF16),
        grid=(m // tq,),
        in_specs=[spec(cb), spec(cb + 1), spec(cb + 2), spec(cb + 3),
                  pl.BlockSpec((1, n_mem, 2 * D_MEMG), lambda i: (i // nb, 0, 0))],
        out_specs=pl.BlockSpec((tq, D_MEMG), lambda i: (i, 0)),
        compiler_params=_cparams(("parallel",)),
        name="mem_attention",
    )(h_plain, h_plain, h_plain, h_plain, kv)


def _out_ln_kernel(yf_ref, yd_ref, ym_ref, w_ref, x_ref, g_ref, b_ref, xo_ref, xb_ref, *, alpha):
    for t in range(x_ref.shape[0] // SEQ_TILE):
        rows = slice(t * SEQ_TILE, (t + 1) * SEQ_TILE)
        y = jnp.dot(yf_ref[rows, :], w_ref[0:D_FOX, :], preferred_element_type=F32)
        y = y + jnp.dot(yd_ref[rows, :], w_ref[D_FOX:D_FOX + D_DSA, :], preferred_element_type=F32)
        y = y + jnp.dot(ym_ref[rows, :], w_ref[D_FOX + D_DSA:, :], preferred_element_type=F32)
        z = alpha * x_ref[rows, :] + y
        mu = jnp.mean(z, axis=1, keepdims=True)
        zc = z - mu
        var = jnp.mean(zc * zc, axis=1, keepdims=True)
        out = zc * lax.rsqrt(var + LN_EPS) * g_ref[...] + b_ref[...]
        xo_ref[rows, :] = out
        xb_ref[rows, :] = out.astype(xb_ref.dtype)


def _out_ln(yf, yd, ym, w_out, x, gain, bias, alpha):
    m, d = x.shape
    tm = 2 * SEQ_TILE
    row = lambda n: pl.BlockSpec((tm, n), lambda i: (i, 0))
    const = lambda r, c: pl.BlockSpec((r, c), lambda i: (0, 0))
    return pl.pallas_call(
        functools.partial(_out_ln_kernel, alpha=alpha),
        out_shape=(jax.ShapeDtypeStruct((m, d), F32), jax.ShapeDtypeStruct((m, d), BF16)),
        grid=(m // tm,),
        in_specs=[row(D_FOX), row(D_DSA), row(D_MEMG), const(w_out.shape[0], d), row(d), const(1, d), const(1, d)],
        out_specs=(row(d), row(d)),
        compiler_params=_cparams(("parallel",), VMEM_LIMIT_BYTES),
        name="out_proj_layernorm",
    )(yf, yd, ym, w_out, x, gain, bias)


def _rope_tables(seq):
    pos = np.arange(seq, dtype=np.float32)

    def tables(half):
        inv_freq = (np.float32(ROPE_THETA) ** (-np.arange(half, dtype=np.float32) / np.float32(half))).astype(np.float32)
        ang = pos[:, None] * inv_freq[None, :]
        return np.cos(ang).astype(np.float32), np.sin(ang).astype(np.float32)

    c64, s64 = tables(HEAD_DIM // 2)
    c32, s32 = tables(IDX_DIM // 2)
    cos128 = np.concatenate([c64, c64], axis=1)
    sin128 = np.concatenate([-s64, s64], axis=1)
    zeros = np.zeros((seq, 128 - IDX_DIM), np.float32)
    cos_ik = np.concatenate([c32, c32, zeros], axis=1)
    sin_ik = np.concatenate([s32, s32, zeros], axis=1)
    transposed = tuple(np.ascontiguousarray(t.T) for t in (c64, s64, c32, s32))
    return cos128, sin128, transposed, cos_ik, sin_ik


def _prepare(seq, d_model, w_in, b_forget, w_mem_kv, w_out):
    depth = w_in.shape[0]
    splits = (D_FOX, D_FOX, D_FOX, D_FOX, H_FOX, D_DSA, D_DSA, D_DSA, D_DSA,
              IDX_HEADS * IDX_DIM, IDX_DIM, IDX_HEADS, D_MEMG, D_MEMG)
    offs = np.concatenate([[0], np.cumsum(splits)]).tolist()
    seg = lambda i: w_in[:, :, offs[i]:offs[i + 1]]
    (w_fq, w_fk, w_fv, w_fg, w_fl, w_dq, w_dk, w_dv, w_dg, w_iq, w_ik, w_iw, w_mq, w_mg) = [seg(i) for i in range(14)]
    pad = jnp.zeros((depth, d_model, 128 - IDX_DIM - IDX_HEADS - H_FOX), w_in.dtype)
    joined = lambda ws: jnp.concatenate(ws, axis=2).astype(BF16)
    cos128, sin128, t_tables, cos_ik, sin_ik = _rope_tables(seq)
    ts = PREP_TILE
    return dict(
        w_plain=jnp.concatenate([w_fk, w_fg, w_dg, w_mq, w_mg], axis=2).astype(BF16),
        q_col=2 * D_FOX + D_DSA,
        w_dk=w_dk.astype(BF16),
        w_t_dsa=joined([w_dq, w_dv, w_iq]),
        w_t_fox=joined([w_fq, w_fv]),
        w_small=jnp.concatenate([w_ik, w_iw, w_fl, pad], axis=2).astype(BF16),
        w_out=w_out.astype(BF16),
        w_mem=w_mem_kv.astype(BF16),
        bf_rows=jnp.zeros((depth, 1, 128), F32).at[:, 0, OFF_FL:OFF_FL + H_FOX].set(b_forget.astype(F32)),
        cos128=cos128, sin128=sin128, t_tables=t_tables, cos_ik=cos_ik, sin_ik=sin_ik,
        tri=jnp.asarray(np.tril(np.ones((ts, ts), np.float32)), BF16),
    )


def _mixers(xb, mem_b, l, p, batch, seq, n_mem):
    scale = HEAD_DIM ** -0.5
    topk = min(INDEX_TOPK, seq // 4)
    w_scale = (IDX_HEADS ** -0.5) * (IDX_DIM ** -0.5)
    kv = _proj_plain(mem_b, p["w_mem"][l], batch * n_mem, 2 * D_MEMG, "proj_mem_kv")
    kv = kv.reshape(batch, n_mem, 2 * D_MEMG)
    h_plain = _proj_plain(xb, p["w_plain"][l], 2 * SEQ_TILE, p["w_plain"].shape[2] // 2, "proj_plain")
    dk = _proj_rope(xb, p["w_dk"][l], p["cos128"], p["sin128"], seq, 2 * SEQ_TILE, "proj_dsa_key")
    dqt, dvt, iqt = _proj_t(xb, p["w_t_dsa"][l], p["t_tables"], batch, seq, scale * LOG2E,
                            (("rope_head_scaled", D_DSA), ("values", D_DSA), ("rope_index", IDX_HEADS * IDX_DIM)),
                            "proj_transposed_dsa")
    fqt, fvt = _proj_t(xb, p["w_t_fox"][l], p["t_tables"], batch, seq, scale * LOG2E,
                       (("scaled", D_FOX), ("values", D_FOX)), "proj_transposed_fox")
    ik, iwt, cb = _prep(xb, p["w_small"][l], p["bf_rows"][l], p["cos_ik"], p["sin_ik"], p["tri"], batch, seq, w_scale)
    y_fox = _fox_attention(h_plain, cb, fqt, fvt, batch, seq)
    y_dsa = _dsa_attention(dk, dvt, ik, dqt, iqt, iwt, h_plain, batch, seq, topk)
    y_mem = _mem_attention(h_plain, kv, batch, seq, scale, p["q_col"])
    return y_fox, y_dsa, y_mem


def kernel(x, mem, w_in, b_forget, w_mem_kv, w_out, ln_gain, ln_bias):
    batch, seq, d_model = x.shape
    depth = w_in.shape[0]
    n_mem = mem.shape[1]
    m = batch * seq
    alpha = (2 * depth) ** 0.25
    assert seq % (2 * SEQ_TILE) == 0 and min(INDEX_TOPK, seq // 4) <= SEQ_TILE

    p = _prepare(seq, d_model, w_in, b_forget, w_mem_kv, w_out)
    mem_b = mem.reshape(batch * n_mem, d_model).astype(BF16)
    xf = x.reshape(m, d_model).astype(F32)
    xb = xf.astype(BF16)
    for l in range(depth):
        y_fox, y_dsa, y_mem = _mixers(xb, mem_b, l, p, batch, seq, n_mem)
        xf, xb = _out_ln(y_fox, y_dsa, y_mem, p["w_out"][l], xf, ln_gain[l][None, :].astype(F32),
                         ln_bias[l][None, :].astype(F32), alpha)
    return xf.reshape(batch, seq, d_model).astype(x.dtype)
```

```python
import functools

import numpy as np
import jax
import jax.numpy as jnp
from jax import lax
from jax.experimental import pallas as pl
from jax.experimental.pallas import tpu as pltpu

HEAD_DIM = 128
H_FOX = 6
H_DSA = 6
H_MEM = 4
D_FOX = H_FOX * HEAD_DIM
D_DSA = H_DSA * HEAD_DIM
D_MEMG = H_MEM * HEAD_DIM
IDX_HEADS = 16
IDX_DIM = 64
INDEX_TOPK = 256
ROPE_THETA = 10000.0
LN_EPS = 1e-5
NEG = -1e30
M_INIT = -1e29
LOG2E = float(np.log2(np.e))

SEQ_TILE = 256
PREP_TILE = 2 * SEQ_TILE
VMEM_LIMIT_BYTES = 56 * 1024 * 1024
OFF_IW = IDX_DIM
OFF_FL = IDX_DIM + IDX_HEADS
BIAS_PIECES = 3
RADIX_FIRST_CHECK = 23
RADIX_CHECK_EVERY = 4
BLOCKS_PER_STEP = 4
ONES_ROWS = 16
V_ROWS = HEAD_DIM + ONES_ROWS

F32 = jnp.float32
BF16 = jnp.bfloat16


def _sortable_key(bits):
    return bits ^ ((bits >> 31) & 0x7FFFFFFF)


_NEG_BITS = int(np.array(NEG, np.float32).view(np.int32))
KEY_NEG = int(np.int32(_NEG_BITS) ^ np.int32(0x7FFFFFFF))
INT_MIN = -(2 ** 31)


def _cparams(sem, vmem=None):
    return pltpu.CompilerParams(dimension_semantics=sem, vmem_limit_bytes=vmem)


def _silu(g):
    return g / (1.0 + jnp.exp(-g))


COUNT_CHAINS = 4


def _partial_sums(x):
    rows, lanes = x.shape
    grouped = x.reshape(rows // (COUNT_CHAINS * 8), COUNT_CHAINS * 8, lanes)
    return jnp.sum(grouped, axis=0)


def _grouped_loop(n, width, body):
    def group_step(w):
        def step(g, carry):
            for j in range(w):
                body(g * w + j)
            return carry
        return step

    lax.fori_loop(0, n // width, group_step(width), 0)
    w = width // 2
    while w >= 1:
        @pl.when((n & w) != 0)
        def _(w=w):
            group_step(w)((n // (2 * w)) * 2, 0)
        w //= 2


def _split_bf16(x, pieces):
    out = []
    r = x
    for _ in range(pieces - 1):
        p = r.astype(BF16)
        out.append(p)
        r = r - p.astype(F32)
    out.append(r.astype(BF16))
    return out


def _proj_plain_kernel(x_ref, w_ref, o_ref):
    o_ref[...] = jnp.dot(x_ref[...], w_ref[...], preferred_element_type=F32).astype(o_ref.dtype)


def _proj_plain(xb, w, tm, tn, name):
    m, d = xb.shape
    n = w.shape[1]
    return pl.pallas_call(
        _proj_plain_kernel,
        out_shape=jax.ShapeDtypeStruct((m, n), BF16),
        grid=(n // tn, m // tm),
        in_specs=[pl.BlockSpec((tm, d), lambda j, i: (i, 0)),
                  pl.BlockSpec((d, tn), lambda j, i: (0, j))],
        out_specs=pl.BlockSpec((tm, tn), lambda j, i: (i, j)),
        compiler_params=_cparams(("parallel", "parallel"), VMEM_LIMIT_BYTES),
        name=name,
    )(xb, w)


def _proj_rope_kernel(x_ref, w_ref, cos_ref, sin_ref, o_ref, *, heads):
    y = jnp.dot(x_ref[...], w_ref[...], preferred_element_type=F32)
    c = cos_ref[...]
    s = sin_ref[...]
    for h in range(heads):
        yh = y[:, h * HEAD_DIM:(h + 1) * HEAD_DIM]
        rot = pltpu.roll(yh, HEAD_DIM // 2, axis=1)
        o_ref[:, h * HEAD_DIM:(h + 1) * HEAD_DIM] = (yh * c + rot * s).astype(o_ref.dtype)


def _proj_rope(xb, w, cos128, sin128, seq, tm, name):
    m, d = xb.shape
    n = w.shape[1]
    nseq = seq // tm
    return pl.pallas_call(
        functools.partial(_proj_rope_kernel, heads=n // HEAD_DIM),
        out_shape=jax.ShapeDtypeStruct((m, n), BF16),
        grid=(m // tm,),
        in_specs=[pl.BlockSpec((tm, d), lambda i: (i, 0)),
                  pl.BlockSpec((d, n), lambda i: (0, 0)),
                  pl.BlockSpec((tm, HEAD_DIM), lambda i: (i % nseq, 0)),
                  pl.BlockSpec((tm, HEAD_DIM), lambda i: (i % nseq, 0))],
        out_specs=pl.BlockSpec((tm, n), lambda i: (i, 0)),
        compiler_params=_cparams(("parallel",), VMEM_LIMIT_BYTES),
        name=name,
    )(xb, w, cos128, sin128)


def _proj_t_kernel(x_ref, w_ref, cos64_ref, sin64_ref, cos32_ref, sin32_ref, *out_refs, groups, scale):
    yt = jnp.dot(x_ref[...], w_ref[...], preferred_element_type=F32).T
    base = 0
    for (kind, rows), o_ref in zip(groups, out_refs):
        y = yt[base:base + rows]
        base += rows
        if kind == "values":
            ones = jnp.ones((ONES_ROWS, y.shape[1]), o_ref.dtype)
            for h in range(rows // HEAD_DIM):
                o_ref[0, 0, h * V_ROWS:h * V_ROWS + HEAD_DIM, :] = y[h * HEAD_DIM:(h + 1) * HEAD_DIM].astype(o_ref.dtype)
                o_ref[0, 0, h * V_ROWS + HEAD_DIM:(h + 1) * V_ROWS, :] = ones
        elif kind == "scaled":
            o_ref[0, 0] = (y * scale).astype(o_ref.dtype)
        else:
            dim, cos_t, sin_t, mul = ((HEAD_DIM, cos64_ref[...], sin64_ref[...], scale) if kind == "rope_head_scaled"
                                      else (IDX_DIM, cos32_ref[...], sin32_ref[...], 1.0))
            half = dim // 2
            for h in range(rows // dim):
                x1 = y[h * dim:h * dim + half]
                x2 = y[h * dim + half:(h + 1) * dim]
                o_ref[0, 0, h * dim:h * dim + half, :] = ((x1 * cos_t - x2 * sin_t) * mul).astype(o_ref.dtype)
                o_ref[0, 0, h * dim + half:(h + 1) * dim, :] = ((x2 * cos_t + x1 * sin_t) * mul).astype(o_ref.dtype)


def _proj_t(xb, w, tables, batch, seq, scale, groups, name):
    m, d = xb.shape
    ts = SEQ_TILE
    nb = seq // ts
    rows = w.shape[1]
    assert rows == sum(r for _, r in groups)
    out_rows = [r // HEAD_DIM * V_ROWS if kind == "values" else r for kind, r in groups]
    blk = lambda r: pl.BlockSpec((1, 1, r, ts), lambda i: (i // nb, i % nb, 0, 0))
    tab = lambda r: pl.BlockSpec((r, ts), lambda i: (0, i % nb))
    return pl.pallas_call(
        functools.partial(_proj_t_kernel, groups=groups, scale=scale),
        out_shape=tuple(jax.ShapeDtypeStruct((batch, nb, r, ts), BF16) for r in out_rows),
        grid=(m // ts,),
        in_specs=[pl.BlockSpec((ts, d), lambda i: (i, 0)),
                  pl.BlockSpec((d, rows), lambda i: (0, 0)),
                  tab(HEAD_DIM // 2), tab(HEAD_DIM // 2), tab(IDX_DIM // 2), tab(IDX_DIM // 2)],
        out_specs=tuple(blk(r) for r in out_rows),
        compiler_params=_cparams(("parallel",), VMEM_LIMIT_BYTES),
        name=name,
    )(xb, w, *tables)


def _prep_kernel(x_ref, w_ref, bf_ref, cos_ref, sin_ref, tri_ref, ik_ref, iwt_ref, cb_ref, carry_ref, *, w_scale):
    j = pl.program_id(1)
    raw = jnp.dot(x_ref[...], w_ref[...], preferred_element_type=F32)
    ts = raw.shape[0]

    lane = lax.broadcasted_iota(jnp.int32, raw.shape, 1)
    q = IDX_DIM // 2
    rot = jnp.where(lane < q, -pltpu.roll(raw, 128 - q, axis=1), pltpu.roll(raw, q, axis=1))
    ik = raw * cos_ref[...] + rot * sin_ref[...]
    ik_ref[0] = ik[:, :IDX_DIM].astype(ik_ref.dtype)

    iw_t = raw.T[OFF_IW:OFF_IW + IDX_HEADS] * w_scale
    for t in range(ts // SEQ_TILE):
        iwt_ref[0, t] = iw_t[:, t * SEQ_TILE:(t + 1) * SEQ_TILE]

    z = raw + bf_ref[...]
    logf = jnp.minimum(z, 0.0) - jnp.log(1.0 + jnp.exp(-jnp.abs(z)))
    tri = tri_ref[...]
    cs = sum(jnp.dot(tri, p, preferred_element_type=F32) for p in _split_bf16(logf, 3))

    @pl.when(j == 0)
    def _():
        carry_ref[...] = jnp.zeros_like(carry_ref)

    cs = cs + carry_ref[...]
    carry_ref[...] = cs[ts - 1:ts, :]

    r = lax.broadcasted_iota(jnp.int32, (128, 128), 0)
    c = lax.broadcasted_iota(jnp.int32, (128, 128), 1)
    head_row = (r >= OFF_FL) & (r < OFF_FL + H_FOX)
    cb = jnp.zeros((ts, 128), F32)
    for k, piece in enumerate(_split_bf16(-LOG2E * cs, BIAS_PIECES)):
        place = (head_row & (c == BIAS_PIECES * (r - OFF_FL) + k)).astype(BF16)
        cb = cb + jnp.dot(piece, place, preferred_element_type=F32)
    cb_ref[0] = cb.astype(cb_ref.dtype)


def _prep(xb, w_small, bf_row, cos_ik, sin_ik, tri, batch, seq, w_scale):
    m, d = xb.shape
    ts = PREP_TILE
    nb = seq // ts
    sub = ts // SEQ_TILE
    out_shapes = (jax.ShapeDtypeStruct((batch, seq, IDX_DIM), BF16),
                  jax.ShapeDtypeStruct((batch, seq // SEQ_TILE, IDX_HEADS, SEQ_TILE), F32),
                  jax.ShapeDtypeStruct((batch, seq, 128), BF16))
    return pl.pallas_call(
        functools.partial(_prep_kernel, w_scale=w_scale),
        out_shape=out_shapes,
        grid=(batch, nb),
        in_specs=[pl.BlockSpec((ts, d), lambda b, j: (b * nb + j, 0)),
                  pl.BlockSpec((d, 128), lambda b, j: (0, 0)),
                  pl.BlockSpec((1, 128), lambda b, j: (0, 0)),
                  pl.BlockSpec((ts, 128), lambda b, j: (j, 0)),
                  pl.BlockSpec((ts, 128), lambda b, j: (j, 0)),
                  pl.BlockSpec((ts, ts), lambda b, j: (0, 0))],
        out_specs=(pl.BlockSpec((1, ts, IDX_DIM), lambda b, j: (b, j, 0)),
                   pl.BlockSpec((1, sub, IDX_HEADS, SEQ_TILE), lambda b, j: (b, j, 0, 0)),
                   pl.BlockSpec((1, ts, 128), lambda b, j: (b, j, 0))),
        scratch_shapes=[pltpu.VMEM((1, 128), F32)],
        compiler_params=_cparams(("parallel", "arbitrary")),
        name="prep_small",
    )(xb, w_small, bf_row, cos_ik, sin_ik, tri)


def _attention_scratch(heads, tq):
    return [pltpu.VMEM((heads, 1, tq), F32),
            pltpu.VMEM((heads, 1, tq), F32),
            pltpu.VMEM((heads, HEAD_DIM, tq), F32),
            pltpu.VMEM((heads, 1, tq), F32),
            pltpu.VMEM((heads, tq, tq), BF16)]


def _attention_core(heads, qi, score_fn, values_fn, diag_mask, g_ref, o_ref,
                    m_ref, l_ref, acc_ref, a_ref, p_ref):
    m_ref[...] = jnp.full(m_ref.shape, M_INIT, F32)
    l_ref[...] = jnp.zeros(l_ref.shape, F32)
    acc_ref[...] = jnp.zeros(acc_ref.shape, F32)
    a_ref[...] = jnp.ones(a_ref.shape, F32)
    p_ref[...] = jnp.zeros(p_ref.shape, BF16)

    def values(kb, h):
        pv = jnp.dot(values_fn(kb, h), p_ref[h], preferred_element_type=F32)
        a = a_ref[h]
        acc_ref[h] = a * acc_ref[h] + pv[:HEAD_DIM]
        l_ref[h] = a * l_ref[h] + pv[HEAD_DIM:HEAD_DIM + 1]

    def score_softmax(kb, h, last):
        s = score_fn(kb, h)
        if last and diag_mask is not None:
            s = diag_mask(s)
        m = m_ref[h]
        m_new = jnp.maximum(m, jnp.max(s, axis=0, keepdims=True))
        a = jnp.exp2(m - m_new)
        p = jnp.exp2(s - m_new)
        m_ref[h] = m_new
        a_ref[h] = a
        p_ref[h] = p.astype(BF16)

    def block(kb, last):
        prev = jnp.maximum(kb - 1, 0)
        for h in range(heads):
            values(prev, h)
            score_softmax(kb, h, last)

    _grouped_loop(qi, BLOCKS_PER_STEP, lambda kb: block(kb, False))
    block(qi, True)
    for h in range(heads):
        values(qi, h)
    for h in range(heads):
        hs = slice(h * HEAD_DIM, (h + 1) * HEAD_DIM)
        o = (acc_ref[h] / l_ref[h]).T
        o_ref[:, hs] = (o * _silu(g_ref[:, hs].astype(F32))).astype(o_ref.dtype)


def _fox_kernel(k_ref, cb_ref, qt_ref, vt_ref, g_ref, o_ref, qa_ref, *scratch):
    qi = pl.program_id(1)
    tq = o_ref.shape[0]
    tk = tq
    row = lax.broadcasted_iota(jnp.int32, (tk, tq), 0)
    col = lax.broadcasted_iota(jnp.int32, (tk, tq), 1)
    sel_row = lax.broadcasted_iota(jnp.int32, (HEAD_DIM, tq), 0)
    for h in range(H_FOX):
        qa_ref[h, 0:HEAD_DIM, :] = qt_ref[0, 0, h * HEAD_DIM:(h + 1) * HEAD_DIM, :]
        pick = (sel_row >= BIAS_PIECES * h) & (sel_row < BIAS_PIECES * (h + 1))
        qa_ref[h, HEAD_DIM:2 * HEAD_DIM, :] = pick.astype(BF16)

    def score_fn(kb, h):
        start = pl.multiple_of(kb * tk, tk)
        keys = jnp.concatenate([k_ref[pl.ds(start, tk), h * HEAD_DIM:(h + 1) * HEAD_DIM],
                                cb_ref[0, pl.ds(start, tk), :]], axis=1)
        return jnp.dot(keys, qa_ref[h], preferred_element_type=F32)

    def values_fn(kb, h):
        return vt_ref[0, kb, h * V_ROWS:(h + 1) * V_ROWS, :]

    def diag_mask(s):
        return jnp.where(row <= col, s, NEG)

    _attention_core(H_FOX, qi, score_fn, values_fn, diag_mask, g_ref, o_ref, *scratch)


def _fox_attention(h_plain, cb, fqt, fvt, batch, seq):
    tq = SEQ_TILE
    nb = seq // tq
    m = batch * seq
    return pl.pallas_call(
        _fox_kernel,
        out_shape=jax.ShapeDtypeStruct((m, D_FOX), BF16),
        grid=(batch, nb),
        in_specs=[pl.BlockSpec((seq, D_FOX), lambda b, i: (b, 0)),
                  pl.BlockSpec((1, seq, 128), lambda b, i: (b, 0, 0)),
                  pl.BlockSpec((1, 1, D_FOX, tq), lambda b, i: (b, i, 0, 0)),
                  pl.BlockSpec((1, nb, H_FOX * V_ROWS, tq), lambda b, i: (b, 0, 0, 0)),
                  pl.BlockSpec((tq, D_FOX), lambda b, i: (b * nb + i, 1))],
        out_specs=pl.BlockSpec((tq, D_FOX), lambda b, i: (b * nb + i, 0)),
        scratch_shapes=[pltpu.VMEM((H_FOX, 2 * HEAD_DIM, tq), BF16)] + _attention_scratch(H_FOX, tq),
        compiler_params=_cparams(("parallel", "parallel"), VMEM_LIMIT_BYTES),
        name="fox_attention",
    )(h_plain, cb, fqt, fvt, h_plain)


def _dsa_kernel(dk_ref, dvt_ref, ik_ref, dqt_ref, iqt_ref, iwt_ref, g_ref, o_ref, keys_ref, *scratch, topk):
    qi = pl.program_id(1)
    tq = o_ref.shape[0]
    tk = tq
    nkb = qi + 1
    row = lax.broadcasted_iota(jnp.int32, (tk, tq), 0)
    col = lax.broadcasted_iota(jnp.int32, (tk, tq), 1)

    def score_block(kb):
        start = pl.multiple_of(kb * tk, tk)
        ikb = ik_ref[0, pl.ds(start, tk), :]
        acc = jnp.zeros((tk, tq), F32)
        for h in range(IDX_HEADS):
            z = jnp.dot(ikb, iqt_ref[0, 0, h * IDX_DIM:(h + 1) * IDX_DIM, :], preferred_element_type=F32)
            acc = acc + jnp.maximum(z, 0.0) * iwt_ref[0, 0, h:h + 1, :]
        key = _sortable_key(lax.bitcast_convert_type(acc, jnp.int32))
        keys_ref[kb] = jnp.where((kb == qi) & (row > col), KEY_NEG, key)

    _grouped_loop(nkb, BLOCKS_PER_STEP, score_block)

    def count(pred):
        def body(kb, acc):
            hit = pred(keys_ref[kb], row + kb * tk).astype(jnp.int32)
            return acc + _partial_sums(hit)
        part = lax.fori_loop(0, nkb, body, jnp.zeros((COUNT_CHAINS * 8, tq), jnp.int32))
        return jnp.sum(part, axis=0, keepdims=True)

    n_nonneg = count(lambda k, _: k >= 0)
    nonneg = n_nonneg >= topk
    prefix0 = jnp.where(nonneg, 0, INT_MIN).astype(jnp.int32)
    n_all = jnp.full((1, tq), tk, jnp.int32) * nkb

    def bit_step(i, carry):
        prefix, n_ge = carry
        cand = prefix | (jnp.int32(1) << (30 - i))
        n = count(lambda k, _: k >= cand)
        keep = n >= topk
        return jnp.where(keep, cand, prefix), jnp.where(keep, n, n_ge)

    few = qi * tq + col[0:1, :] + 1 <= topk

    def settled(carry):
        return jnp.min(((carry[1] == topk) | few).astype(jnp.int32)) > 0

    def remaining_bits(first_bit):
        def run(carry):
            last_bit = min(first_bit + RADIX_CHECK_EVERY, 31)
            carry = lax.fori_loop(first_bit, last_bit, bit_step, carry)
            if last_bit == 31:
                return carry
            return lax.cond(settled(carry), lambda c: c, remaining_bits(last_bit), carry)
        return run

    carry = lax.fori_loop(0, RADIX_FIRST_CHECK, bit_step, (prefix0, jnp.where(nonneg, n_nonneg, n_all)))
    vstar, n_ge = lax.cond(settled(carry), lambda c: c, remaining_bits(RADIX_FIRST_CHECK), carry)

    tie = jnp.max(((n_ge > topk) & (vstar > KEY_NEG)).astype(jnp.int32)) > 0

    def tie_index(_):
        n_gt = count(lambda k, _: k > vstar)

        def idx_step(i, jx):
            cand = jx | (jnp.int32(1) << (29 - i))
            n = n_gt + count(lambda k, idx: (k == vstar) & (idx < cand))
            return jnp.where(n <= topk, cand, jx)
        return lax.fori_loop(0, 30, idx_step, jnp.zeros((1, tq), jnp.int32))

    def store_bias(kb, sel):
        keys_ref[kb] = lax.bitcast_convert_type(jnp.where(sel, 0.0, NEG).astype(F32), jnp.int32)

    def bias_with_ties(_):
        jx = tie_index(0)

        def body(kb, carry):
            k = keys_ref[kb]
            store_bias(kb, ((k > vstar) | ((k == vstar) & ((row + kb * tk) < jx))) & (k > KEY_NEG))
            return carry

        lax.fori_loop(0, nkb, body, 0)
        return 0

    def bias_without_ties(_):
        threshold = jnp.maximum(vstar, KEY_NEG + 1)

        def body(kb, carry):
            store_bias(kb, keys_ref[kb] >= threshold)
            return carry

        lax.fori_loop(0, nkb, body, 0)
        return 0

    lax.cond(tie, bias_with_ties, bias_without_ties, 0)

    def score_fn(kb, h):
        start = pl.multiple_of(kb * tk, tk)
        hs = slice(h * HEAD_DIM, (h + 1) * HEAD_DIM)
        s = jnp.dot(dk_ref[pl.ds(start, tk), hs], dqt_ref[0, 0, hs, :], preferred_element_type=F32)
        return s + lax.bitcast_convert_type(keys_ref[kb], F32)

    def values_fn(kb, h):
        return dvt_ref[0, kb, h * V_ROWS:(h + 1) * V_ROWS, :]

    _attention_core(H_DSA, qi, score_fn, values_fn, None, g_ref, o_ref, *scratch)


def _dsa_attention(dk, dvt, ik, dqt, iqt, iwt, h_plain, batch, seq, topk):
    tq = SEQ_TILE
    nb = seq // tq
    m = batch * seq
    return pl.pallas_call(
        functools.partial(_dsa_kernel, topk=topk),
        out_shape=jax.ShapeDtypeStruct((m, D_DSA), BF16),
        grid=(batch, nb),
        in_specs=[pl.BlockSpec((seq, D_DSA), lambda b, i: (b, 0)),
                  pl.BlockSpec((1, nb, H_DSA * V_ROWS, tq), lambda b, i: (b, 0, 0, 0)),
                  pl.BlockSpec((1, seq, IDX_DIM), lambda b, i: (b, 0, 0)),
                  pl.BlockSpec((1, 1, D_DSA, tq), lambda b, i: (b, i, 0, 0)),
                  pl.BlockSpec((1, 1, IDX_HEADS * IDX_DIM, tq), lambda b, i: (b, i, 0, 0)),
                  pl.BlockSpec((1, 1, IDX_HEADS, tq), lambda b, i: (b, i, 0, 0)),
                  pl.BlockSpec((tq, D_DSA), lambda b, i: (b * nb + i, 2))],
        out_specs=pl.BlockSpec((tq, D_DSA), lambda b, i: (b * nb + i, 0)),
        scratch_shapes=[pltpu.VMEM((nb, tq, tq), jnp.int32)] + _attention_scratch(H_DSA, tq),
        compiler_params=_cparams(("parallel", "parallel"), VMEM_LIMIT_BYTES),
        name="dsa_attention",
    )(dk, dvt, ik, dqt, iqt, iwt, h_plain)


def _mem_kernel(q0_ref, q1_ref, g0_ref, g1_ref, kv_ref, o_ref, *, scale):
    for h in range(H_MEM):
        q_ref = (q0_ref, q1_ref)[h // 2]
        g_ref = (g0_ref, g1_ref)[h // 2]
        ls = slice((h % 2) * HEAD_DIM, (h % 2 + 1) * HEAD_DIM)
        q = (q_ref[:, ls].astype(F32) * scale).astype(BF16)
        k = kv_ref[0, :, h * HEAD_DIM:(h + 1) * HEAD_DIM]
        v = kv_ref[0, :, D_MEMG + h * HEAD_DIM:D_MEMG + (h + 1) * HEAD_DIM]
        s = lax.dot_general(q, k, (((1,), (1,)), ((), ())), preferred_element_type=F32)
        m = jnp.max(s, axis=1, keepdims=True)
        p = jnp.exp(s - m)
        l = jnp.sum(p, axis=1, keepdims=True)
        o = jnp.dot(p.astype(BF16), v, preferred_element_type=F32) / l
        o_ref[:, h * HEAD_DIM:(h + 1) * HEAD_DIM] = (o * _silu(g_ref[:, ls].astype(F32))).astype(o_ref.dtype)


def _mem_attention(h_plain, kv, batch, seq, scale, q_col):
    tq = 2 * SEQ_TILE
    nb = seq // tq
    m = batch * seq
    n_mem = kv.shape[1]
    cb = q_col // 256
    spec = lambda c: pl.BlockSpec((tq, 256), lambda i: (i, c))
    return pl.pallas_call(
        functools.partial(_mem_kernel, scale=scale),
        out_shape=jax.ShapeDtypeStruct((m, D_MEMG), BF16),
        grid=(m // tq,),
        in_specs=[spec(cb), spec(cb + 1), spec(cb + 2), spec(cb + 3),
                  pl.BlockSpec((1, n_mem, 2 * D_MEMG), lambda i: (i // nb, 0, 0))],
        out_specs=pl.BlockSpec((tq, D_MEMG), lambda i: (i, 0)),
        compiler_params=_cparams(("parallel",)),
        name="mem_attention",
    )(h_plain, h_plain, h_plain, h_plain, kv)


def _out_ln_kernel(yf_ref, yd_ref, ym_ref, w_ref, x_ref, g_ref, b_ref, xo_ref, xb_ref, *, alpha):
    for t in range(x_ref.shape[0] // SEQ_TILE):
        rows = slice(t * SEQ_TILE, (t + 1) * SEQ_TILE)
        y = jnp.dot(yf_ref[rows, :], w_ref[0:D_FOX, :], preferred_element_type=F32)
        y = y + jnp.dot(yd_ref[rows, :], w_ref[D_FOX:D_FOX + D_DSA, :], preferred_element_type=F32)
        y = y + jnp.dot(ym_ref[rows, :], w_ref[D_FOX + D_DSA:, :], preferred_element_type=F32)
        z = alpha * x_ref[rows, :] + y
        mu = jnp.mean(z, axis=1, keepdims=True)
        zc = z - mu
        var = jnp.mean(zc * zc, axis=1, keepdims=True)
        out = zc * lax.rsqrt(var + LN_EPS) * g_ref[...] + b_ref[...]
        xo_ref[rows, :] = out
        xb_ref[rows, :] = out.astype(xb_ref.dtype)


def _out_ln(yf, yd, ym, w_out, x, gain, bias, alpha):
    m, d = x.shape
    tm = 2 * SEQ_TILE
    row = lambda n: pl.BlockSpec((tm, n), lambda i: (i, 0))
    const = lambda r, c: pl.BlockSpec((r, c), lambda i: (0, 0))
    return pl.pallas_call(
        functools.partial(_out_ln_kernel, alpha=alpha),
        out_shape=(jax.ShapeDtypeStruct((m, d), F32), jax.ShapeDtypeStruct((m, d), BF16)),
        grid=(m // tm,),
        in_specs=[row(D_FOX), row(D_DSA), row(D_MEMG), const(w_out.shape[0], d), row(d), const(1, d), const(1, d)],
        out_specs=(row(d), row(d)),
        compiler_params=_cparams(("parallel",), VMEM_LIMIT_BYTES),
        name="out_proj_layernorm",
    )(yf, yd, ym, w_out, x, gain, bias)


def _rope_tables(seq):
    pos = np.arange(seq, dtype=np.float32)

    def tables(half):
        inv_freq = (np.float32(ROPE_THETA) ** (-np.arange(half, dtype=np.float32) / np.float32(half))).astype(np.float32)
        ang = pos[:, None] * inv_freq[None, :]
        return np.cos(ang).astype(np.float32), np.sin(ang).astype(np.float32)

    c64, s64 = tables(HEAD_DIM // 2)
    c32, s32 = tables(IDX_DIM // 2)
    cos128 = np.concatenate([c64, c64], axis=1)
    sin128 = np.concatenate([-s64, s64], axis=1)
    zeros = np.zeros((seq, 128 - IDX_DIM), np.float32)
    cos_ik = np.concatenate([c32, c32, zeros], axis=1)
    sin_ik = np.concatenate([s32, s32, zeros], axis=1)
    transposed = tuple(np.ascontiguousarray(t.T) for t in (c64, s64, c32, s32))
    return cos128, sin128, transposed, cos_ik, sin_ik


def _prepare(seq, d_model, w_in, b_forget, w_mem_kv, w_out):
    depth = w_in.shape[0]
    splits = (D_FOX, D_FOX, D_FOX, D_FOX, H_FOX, D_DSA, D_DSA, D_DSA, D_DSA,
              IDX_HEADS * IDX_DIM, IDX_DIM, IDX_HEADS, D_MEMG, D_MEMG)
    offs = np.concatenate([[0], np.cumsum(splits)]).tolist()
    seg = lambda i: w_in[:, :, offs[i]:offs[i + 1]]
    (w_fq, w_fk, w_fv, w_fg, w_fl, w_dq, w_dk, w_dv, w_dg, w_iq, w_ik, w_iw, w_mq, w_mg) = [seg(i) for i in range(14)]
    pad = jnp.zeros((depth, d_model, 128 - IDX_DIM - IDX_HEADS - H_FOX), w_in.dtype)
    joined = lambda ws: jnp.concatenate(ws, axis=2).astype(BF16)
    cos128, sin128, t_tables, cos_ik, sin_ik = _rope_tables(seq)
    ts = PREP_TILE
    return dict(
        w_plain=jnp.concatenate([w_fk, w_fg, w_dg, w_mq, w_mg], axis=2).astype(BF16),
        q_col=2 * D_FOX + D_DSA,
        w_dk=w_dk.astype(BF16),
        w_t_dsa=joined([w_dq, w_dv, w_iq]),
        w_t_fox=joined([w_fq, w_fv]),
        w_small=jnp.concatenate([w_ik, w_iw, w_fl, pad], axis=2).astype(BF16),
        w_out=w_out.astype(BF16),
        w_mem=w_mem_kv.astype(BF16),
        bf_rows=jnp.zeros((depth, 1, 128), F32).at[:, 0, OFF_FL:OFF_FL + H_FOX].set(b_forget.astype(F32)),
        cos128=cos128, sin128=sin128, t_tables=t_tables, cos_ik=cos_ik, sin_ik=sin_ik,
        tri=jnp.asarray(np.tril(np.ones((ts, ts), np.float32)), BF16),
    )


def _mixers(xb, mem_b, l, p, batch, seq, n_mem):
    scale = HEAD_DIM ** -0.5
    topk = min(INDEX_TOPK, seq // 4)
    w_scale = (IDX_HEADS ** -0.5) * (IDX_DIM ** -0.5)
    kv = _proj_plain(mem_b, p["w_mem"][l], batch * n_mem, 2 * D_MEMG, "proj_mem_kv")
    kv = kv.reshape(batch, n_mem, 2 * D_MEMG)
    h_plain = _proj_plain(xb, p["w_plain"][l], 2 * SEQ_TILE, p["w_plain"].shape[2] // 2, "proj_plain")
    dk = _proj_rope(xb, p["w_dk"][l], p["cos128"], p["sin128"], seq, 2 * SEQ_TILE, "proj_dsa_key")
    dqt, dvt, iqt = _proj_t(xb, p["w_t_dsa"][l], p["t_tables"], batch, seq, scale * LOG2E,
                            (("rope_head_scaled", D_DSA), ("values", D_DSA), ("rope_index", IDX_HEADS * IDX_DIM)),
                            "proj_transposed_dsa")
    fqt, fvt = _proj_t(xb, p["w_t_fox"][l], p["t_tables"], batch, seq, scale * LOG2E,
                       (("scaled", D_FOX), ("values", D_FOX)), "proj_transposed_fox")
    ik, iwt, cb = _prep(xb, p["w_small"][l], p["bf_rows"][l], p["cos_ik"], p["sin_ik"], p["tri"], batch, seq, w_scale)
    y_fox = _fox_attention(h_plain, cb, fqt, fvt, batch, seq)
    y_dsa = _dsa_attention(dk, dvt, ik, dqt, iqt, iwt, h_plain, batch, seq, topk)
    y_mem = _mem_attention(h_plain, kv, batch, seq, scale, p["q_col"])
    return y_fox, y_dsa, y_mem


def kernel(x, mem, w_in, b_forget, w_mem_kv, w_out, ln_gain, ln_bias):
    batch, seq, d_model = x.shape
    depth = w_in.shape[0]
    n_mem = mem.shape[1]
    m = batch * seq
    alpha = (2 * depth) ** 0.25
    assert seq % (2 * SEQ_TILE) == 0 and min(INDEX_TOPK, seq // 4) <= SEQ_TILE

    p = _prepare(seq, d_model, w_in, b_forget, w_mem_kv, w_out)
    mem_b = mem.reshape(batch * n_mem, d_model).astype(BF16)
    xf = x.reshape(m, d_model).astype(F32)
    xb = xf.astype(BF16)
    for l in range(depth):
        y_fox, y_dsa, y_mem = _mixers(xb, mem_b, l, p, batch, seq, n_mem)
        xf, xb = _out_ln(y_fox, y_dsa, y_mem, p["w_out"][l], xf, ln_gain[l][None, :].astype(F32),
                         ln_bias[l][None, :].astype(F32), alpha)
    return xf.reshape(batch, seq, d_model).astype(x.dtype)
```

```python
import functools

import numpy as np
import jax
import jax.numpy as jnp
from jax import lax
from jax.experimental import pallas as pl
from jax.experimental.pallas import tpu as pltpu

HEAD_DIM = 128
H_FOX = 6
H_DSA = 6
H_MEM = 4
D_FOX = H_FOX * HEAD_DIM
D_DSA = H_DSA * HEAD_DIM
D_MEMG = H_MEM * HEAD_DIM
IDX_HEADS = 16
IDX_DIM = 64
INDEX_TOPK = 256
ROPE_THETA = 10000.0
LN_EPS = 1e-5
NEG = -1e30
M_INIT = -1e29
LOG2E = float(np.log2(np.e))

SEQ_TILE = 256
PREP_TILE = 2 * SEQ_TILE
PREP_STEP = 4 * PREP_TILE
VMEM_LIMIT_BYTES = 56 * 1024 * 1024
OFF_IW = IDX_DIM
OFF_FL = IDX_DIM + IDX_HEADS
BIAS_PIECES = 3
RADIX_FIRST_CHECK = 23
RADIX_CHECK_EVERY = 4
BLOCKS_PER_STEP = 4
ONES_ROWS = 16
V_ROWS = HEAD_DIM + ONES_ROWS

F32 = jnp.float32
BF16 = jnp.bfloat16


def _sortable_key(bits):
    return bits ^ ((bits >> 31) & 0x7FFFFFFF)


_NEG_BITS = int(np.array(NEG, np.float32).view(np.int32))
KEY_NEG = int(np.int32(_NEG_BITS) ^ np.int32(0x7FFFFFFF))
INT_MIN = -(2 ** 31)


def _cparams(sem, vmem=None):
    return pltpu.CompilerParams(dimension_semantics=sem, vmem_limit_bytes=vmem)


def _silu(g):
    return g / (1.0 + jnp.exp(-g))


COUNT_CHAINS = 4


def _partial_sums(x):
    rows, lanes = x.shape
    grouped = x.reshape(rows // (COUNT_CHAINS * 8), COUNT_CHAINS * 8, lanes)
    return jnp.sum(grouped, axis=0)


def _grouped_loop(n, width, body):
    def group_step(w):
        def step(g, carry):
            for j in range(w):
                body(g * w + j)
            return carry
        return step

    lax.fori_loop(0, n // width, group_step(width), 0)
    w = width // 2
    while w >= 1:
        @pl.when((n & w) != 0)
        def _(w=w):
            group_step(w)((n // (2 * w)) * 2, 0)
        w //= 2


def _split_bf16(x, pieces):
    out = []
    r = x
    for _ in range(pieces - 1):
        p = r.astype(BF16)
        out.append(p)
        r = r - p.astype(F32)
    out.append(r.astype(BF16))
    return out


def _proj_plain_kernel(x_ref, w_ref, o_ref):
    o_ref[...] = jnp.dot(x_ref[...], w_ref[...], preferred_element_type=F32).astype(o_ref.dtype)


def _proj_plain(xb, w, tm, tn, name):
    m, d = xb.shape
    n = w.shape[1]
    return pl.pallas_call(
        _proj_plain_kernel,
        out_shape=jax.ShapeDtypeStruct((m, n), BF16),
        grid=(n // tn, m // tm),
        in_specs=[pl.BlockSpec((tm, d), lambda j, i: (i, 0)),
                  pl.BlockSpec((d, tn), lambda j, i: (0, j))],
        out_specs=pl.BlockSpec((tm, tn), lambda j, i: (i, j)),
        compiler_params=_cparams(("parallel", "parallel"), VMEM_LIMIT_BYTES),
        name=name,
    )(xb, w)


def _proj_rope_kernel(x_ref, w_ref, cos_ref, sin_ref, o_ref, *, heads):
    y = jnp.dot(x_ref[...], w_ref[...], preferred_element_type=F32)
    c = cos_ref[...]
    s = sin_ref[...]
    for h in range(heads):
        yh = y[:, h * HEAD_DIM:(h + 1) * HEAD_DIM]
        rot = pltpu.roll(yh, HEAD_DIM // 2, axis=1)
        o_ref[:, h * HEAD_DIM:(h + 1) * HEAD_DIM] = (yh * c + rot * s).astype(o_ref.dtype)


def _proj_rope(xb, w, cos128, sin128, seq, tm, name):
    m, d = xb.shape
    n = w.shape[1]
    nseq = seq // tm
    return pl.pallas_call(
        functools.partial(_proj_rope_kernel, heads=n // HEAD_DIM),
        out_shape=jax.ShapeDtypeStruct((m, n), BF16),
        grid=(m // tm,),
        in_specs=[pl.BlockSpec((tm, d), lambda i: (i, 0)),
                  pl.BlockSpec((d, n), lambda i: (0, 0)),
                  pl.BlockSpec((tm, HEAD_DIM), lambda i: (i % nseq, 0)),
                  pl.BlockSpec((tm, HEAD_DIM), lambda i: (i % nseq, 0))],
        out_specs=pl.BlockSpec((tm, n), lambda i: (i, 0)),
        compiler_params=_cparams(("parallel",), VMEM_LIMIT_BYTES),
        name=name,
    )(xb, w, cos128, sin128)


def _proj_t_kernel(x_ref, w_ref, cos64_ref, sin64_ref, cos32_ref, sin32_ref, *out_refs, groups, scale):
    yt = jnp.dot(x_ref[...], w_ref[...], preferred_element_type=F32).T
    base = 0
    for (kind, rows), o_ref in zip(groups, out_refs):
        y = yt[base:base + rows]
        base += rows
        if kind == "values":
            ones = jnp.ones((ONES_ROWS, y.shape[1]), o_ref.dtype)
            for h in range(rows // HEAD_DIM):
                o_ref[0, 0, h * V_ROWS:h * V_ROWS + HEAD_DIM, :] = y[h * HEAD_DIM:(h + 1) * HEAD_DIM].astype(o_ref.dtype)
                o_ref[0, 0, h * V_ROWS + HEAD_DIM:(h + 1) * V_ROWS, :] = ones
        elif kind == "scaled":
            o_ref[0, 0] = (y * scale).astype(o_ref.dtype)
        else:
            dim, cos_t, sin_t, mul = ((HEAD_DIM, cos64_ref[...], sin64_ref[...], scale) if kind == "rope_head_scaled"
                                      else (IDX_DIM, cos32_ref[...], sin32_ref[...], 1.0))
            half = dim // 2
            for h in range(rows // dim):
                x1 = y[h * dim:h * dim + half]
                x2 = y[h * dim + half:(h + 1) * dim]
                o_ref[0, 0, h * dim:h * dim + half, :] = ((x1 * cos_t - x2 * sin_t) * mul).astype(o_ref.dtype)
                o_ref[0, 0, h * dim + half:(h + 1) * dim, :] = ((x2 * cos_t + x1 * sin_t) * mul).astype(o_ref.dtype)


def _proj_t(xb, w, tables, batch, seq, scale, groups, name):
    m, d = xb.shape
    ts = SEQ_TILE
    nb = seq // ts
    rows = w.shape[1]
    assert rows == sum(r for _, r in groups)
    out_rows = [r // HEAD_DIM * V_ROWS if kind == "values" else r for kind, r in groups]
    blk = lambda r: pl.BlockSpec((1, 1, r, ts), lambda i: (i // nb, i % nb, 0, 0))
    tab = lambda r: pl.BlockSpec((r, ts), lambda i: (0, i % nb))
    return pl.pallas_call(
        functools.partial(_proj_t_kernel, groups=groups, scale=scale),
        out_shape=tuple(jax.ShapeDtypeStruct((batch, nb, r, ts), BF16) for r in out_rows),
        grid=(m // ts,),
        in_specs=[pl.BlockSpec((ts, d), lambda i: (i, 0)),
                  pl.BlockSpec((d, rows), lambda i: (0, 0)),
                  tab(HEAD_DIM // 2), tab(HEAD_DIM // 2), tab(IDX_DIM // 2), tab(IDX_DIM // 2)],
        out_specs=tuple(blk(r) for r in out_rows),
        compiler_params=_cparams(("parallel",), VMEM_LIMIT_BYTES),
        name=name,
    )(xb, w, *tables)


def _prep_kernel(x_ref, w_ref, bf_ref, cos_ref, sin_ref, tri_ref, ik_ref, iwt_ref, cb_ref, carry_ref, *, w_scale):
    j = pl.program_id(1)
    raw_all = jnp.dot(x_ref[...], w_ref[...], preferred_element_type=F32)
    ts = tri_ref.shape[0]
    tri = tri_ref[...]
    q = IDX_DIM // 2
    lane = lax.broadcasted_iota(jnp.int32, (ts, 128), 1)
    r = lax.broadcasted_iota(jnp.int32, (128, 128), 0)
    c = lax.broadcasted_iota(jnp.int32, (128, 128), 1)
    head_row = (r >= OFF_FL) & (r < OFF_FL + H_FOX)

    @pl.when(j == 0)
    def _():
        carry_ref[...] = jnp.zeros_like(carry_ref)

    carry = carry_ref[...]
    for t in range(raw_all.shape[0] // ts):
        rows = slice(t * ts, (t + 1) * ts)
        raw = raw_all[rows]

        rot = jnp.where(lane < q, -pltpu.roll(raw, 128 - q, axis=1), pltpu.roll(raw, q, axis=1))
        ik = raw * cos_ref[rows, :] + rot * sin_ref[rows, :]
        ik_ref[0, rows, :] = ik[:, :IDX_DIM].astype(ik_ref.dtype)

        iw_t = raw.T[OFF_IW:OFF_IW + IDX_HEADS] * w_scale
        for u in range(ts // SEQ_TILE):
            iwt_ref[0, t * (ts // SEQ_TILE) + u] = iw_t[:, u * SEQ_TILE:(u + 1) * SEQ_TILE]

        z = raw + bf_ref[...]
        logf = jnp.minimum(z, 0.0) - jnp.log(1.0 + jnp.exp(-jnp.abs(z)))
        parts = jnp.dot(tri, jnp.concatenate(_split_bf16(logf, 3), axis=1), preferred_element_type=F32)
        cs = parts[:, 0:128] + parts[:, 128:256] + parts[:, 256:384] + carry
        carry = cs[ts - 1:ts, :]

        pieces = jnp.concatenate(_split_bf16(-LOG2E * cs, BIAS_PIECES), axis=1)
        place = jnp.concatenate([(head_row & (c == BIAS_PIECES * (r - OFF_FL) + k)).astype(BF16)
                                 for k in range(BIAS_PIECES)], axis=0)
        cb_ref[0, rows, :] = jnp.dot(pieces, place, preferred_element_type=F32).astype(cb_ref.dtype)
    carry_ref[...] = carry


def _prep(xb, w_small, bf_row, cos_ik, sin_ik, tri, batch, seq, w_scale):
    m, d = xb.shape
    ts = min(PREP_STEP, seq)
    assert seq % ts == 0 and ts % PREP_TILE == 0
    nb = seq // ts
    sub = ts // SEQ_TILE
    out_shapes = (jax.ShapeDtypeStruct((batch, seq, IDX_DIM), BF16),
                  jax.ShapeDtypeStruct((batch, seq // SEQ_TILE, IDX_HEADS, SEQ_TILE), F32),
                  jax.ShapeDtypeStruct((batch, seq, 128), BF16))
    return pl.pallas_call(
        functools.partial(_prep_kernel, w_scale=w_scale),
        out_shape=out_shapes,
        grid=(batch, nb),
        in_specs=[pl.BlockSpec((ts, d), lambda b, j: (b * nb + j, 0)),
                  pl.BlockSpec((d, 128), lambda b, j: (0, 0)),
                  pl.BlockSpec((1, 128), lambda b, j: (0, 0)),
                  pl.BlockSpec((ts, 128), lambda b, j: (j, 0)),
                  pl.BlockSpec((ts, 128), lambda b, j: (j, 0)),
                  pl.BlockSpec((PREP_TILE, PREP_TILE), lambda b, j: (0, 0))],
        out_specs=(pl.BlockSpec((1, ts, IDX_DIM), lambda b, j: (b, j, 0)),
                   pl.BlockSpec((1, sub, IDX_HEADS, SEQ_TILE), lambda b, j: (b, j, 0, 0)),
                   pl.BlockSpec((1, ts, 128), lambda b, j: (b, j, 0))),
        scratch_shapes=[pltpu.VMEM((1, 128), F32)],
        compiler_params=_cparams(("parallel", "arbitrary")),
        name="prep_small",
    )(xb, w_small, bf_row, cos_ik, sin_ik, tri)


def _attention_scratch(heads, tq):
    return [pltpu.VMEM((heads, 1, tq), F32),
            pltpu.VMEM((heads, 1, tq), F32),
            pltpu.VMEM((heads, HEAD_DIM, tq), F32),
            pltpu.VMEM((heads, 1, tq), F32),
            pltpu.VMEM((heads, tq, tq), BF16)]


def _attention_core(heads, qi, score_fn, values_fn, diag_mask, g_ref, o_ref,
                    m_ref, l_ref, acc_ref, a_ref, p_ref):
    m_ref[...] = jnp.full(m_ref.shape, M_INIT, F32)
    l_ref[...] = jnp.zeros(l_ref.shape, F32)
    acc_ref[...] = jnp.zeros(acc_ref.shape, F32)
    a_ref[...] = jnp.ones(a_ref.shape, F32)
    p_ref[...] = jnp.zeros(p_ref.shape, BF16)

    def values(kb, h):
        pv = jnp.dot(values_fn(kb, h), p_ref[h], preferred_element_type=F32)
        a = a_ref[h]
        acc_ref[h] = a * acc_ref[h] + pv[:HEAD_DIM]
        l_ref[h] = a * l_ref[h] + pv[HEAD_DIM:HEAD_DIM + 1]

    def score_softmax(kb, h, last):
        s = score_fn(kb, h)
        if last and diag_mask is not None:
            s = diag_mask(s)
        m = m_ref[h]
        m_new = jnp.maximum(m, jnp.max(s, axis=0, keepdims=True))
        a = jnp.exp2(m - m_new)
        p = jnp.exp2(s - m_new)
        m_ref[h] = m_new
        a_ref[h] = a
        p_ref[h] = p.astype(BF16)

    def block(kb, last):
        prev = jnp.maximum(kb - 1, 0)
        for h in range(heads):
            values(prev, h)
            score_softmax(kb, h, last)

    _grouped_loop(qi, BLOCKS_PER_STEP, lambda kb: block(kb, False))
    block(qi, True)
    for h in range(heads):
        values(qi, h)
    for h in range(heads):
        hs = slice(h * HEAD_DIM, (h + 1) * HEAD_DIM)
        o = (acc_ref[h] / l_ref[h]).T
        o_ref[:, hs] = (o * _silu(g_ref[:, hs].astype(F32))).astype(o_ref.dtype)


def _fox_kernel(k_ref, cb_ref, qt_ref, vt_ref, g_ref, o_ref, qa_ref, *scratch):
    qi = pl.program_id(1)
    tq = o_ref.shape[0]
    tk = tq
    row = lax.broadcasted_iota(jnp.int32, (tk, tq), 0)
    col = lax.broadcasted_iota(jnp.int32, (tk, tq), 1)
    sel_row = lax.broadcasted_iota(jnp.int32, (HEAD_DIM, tq), 0)
    for h in range(H_FOX):
        qa_ref[h, 0:HEAD_DIM, :] = qt_ref[0, 0, h * HEAD_DIM:(h + 1) * HEAD_DIM, :]
        pick = (sel_row >= BIAS_PIECES * h) & (sel_row < BIAS_PIECES * (h + 1))
        qa_ref[h, HEAD_DIM:2 * HEAD_DIM, :] = pick.astype(BF16)

    def score_fn(kb, h):
        start = pl.multiple_of(kb * tk, tk)
        keys = jnp.concatenate([k_ref[pl.ds(start, tk), h * HEAD_DIM:(h + 1) * HEAD_DIM],
                                cb_ref[0, pl.ds(start, tk), :]], axis=1)
        return jnp.dot(keys, qa_ref[h], preferred_element_type=F32)

    def values_fn(kb, h):
        return vt_ref[0, kb, h * V_ROWS:(h + 1) * V_ROWS, :]

    def diag_mask(s):
        return jnp.where(row <= col, s, NEG)

    _attention_core(H_FOX, qi, score_fn, values_fn, diag_mask, g_ref, o_ref, *scratch)


def _fox_attention(h_plain, cb, fqt, fvt, batch, seq):
    tq = SEQ_TILE
    nb = seq // tq
    m = batch * seq
    return pl.pallas_call(
        _fox_kernel,
        out_shape=jax.ShapeDtypeStruct((m, D_FOX), BF16),
        grid=(batch, nb),
        in_specs=[pl.BlockSpec((seq, D_FOX), lambda b, i: (b, 0)),
                  pl.BlockSpec((1, seq, 128), lambda b, i: (b, 0, 0)),
                  pl.BlockSpec((1, 1, D_FOX, tq), lambda b, i: (b, i, 0, 0)),
                  pl.BlockSpec((1, nb, H_FOX * V_ROWS, tq), lambda b, i: (b, 0, 0, 0)),
                  pl.BlockSpec((tq, D_FOX), lambda b, i: (b * nb + i, 1))],
        out_specs=pl.BlockSpec((tq, D_FOX), lambda b, i: (b * nb + i, 0)),
        scratch_shapes=[pltpu.VMEM((H_FOX, 2 * HEAD_DIM, tq), BF16)] + _attention_scratch(H_FOX, tq),
        compiler_params=_cparams(("parallel", "parallel"), VMEM_LIMIT_BYTES),
        name="fox_attention",
    )(h_plain, cb, fqt, fvt, h_plain)


def _dsa_kernel(dk_ref, dvt_ref, ik_ref, dqt_ref, iqt_ref, iwt_ref, g_ref, o_ref, keys_ref, *scratch, topk):
    qi = pl.program_id(1)
    tq = o_ref.shape[0]
    tk = tq
    nkb = qi + 1
    row = lax.broadcasted_iota(jnp.int32, (tk, tq), 0)
    col = lax.broadcasted_iota(jnp.int32, (tk, tq), 1)

    def score_block(kb):
        start = pl.multiple_of(kb * tk, tk)
        ikb = ik_ref[0, pl.ds(start, tk), :]
        acc = jnp.zeros((tk, tq), F32)
        for h in range(IDX_HEADS):
            z = jnp.dot(ikb, iqt_ref[0, 0, h * IDX_DIM:(h + 1) * IDX_DIM, :], preferred_element_type=F32)
            acc = acc + jnp.maximum(z, 0.0) * iwt_ref[0, 0, h:h + 1, :]
        key = _sortable_key(lax.bitcast_convert_type(acc, jnp.int32))
        keys_ref[kb] = jnp.where((kb == qi) & (row > col), KEY_NEG, key)

    _grouped_loop(nkb, BLOCKS_PER_STEP, score_block)

    def count(pred):
        def body(kb, acc):
            hit = pred(keys_ref[kb], row + kb * tk).astype(jnp.int32)
            return acc + _partial_sums(hit)
        part = lax.fori_loop(0, nkb, body, jnp.zeros((COUNT_CHAINS * 8, tq), jnp.int32))
        return jnp.sum(part, axis=0, keepdims=True)

    n_nonneg = count(lambda k, _: k >= 0)
    nonneg = n_nonneg >= topk
    prefix0 = jnp.where(nonneg, 0, INT_MIN).astype(jnp.int32)
    n_all = jnp.full((1, tq), tk, jnp.int32) * nkb

    def bit_step(i, carry):
        prefix, n_ge = carry
        cand = prefix | (jnp.int32(1) << (30 - i))
        n = count(lambda k, _: k >= cand)
        keep = n >= topk
        return jnp.where(keep, cand, prefix), jnp.where(keep, n, n_ge)

    few = qi * tq + col[0:1, :] + 1 <= topk

    def settled(carry):
        return jnp.min(((carry[1] == topk) | few).astype(jnp.int32)) > 0

    def remaining_bits(first_bit):
        def run(carry):
            last_bit = min(first_bit + RADIX_CHECK_EVERY, 31)
            carry = lax.fori_loop(first_bit, last_bit, bit_step, carry)
            if last_bit == 31:
                return carry
            return lax.cond(settled(carry), lambda c: c, remaining_bits(last_bit), carry)
        return run

    carry = lax.fori_loop(0, RADIX_FIRST_CHECK, bit_step, (prefix0, jnp.where(nonneg, n_nonneg, n_all)))
    vstar, n_ge = lax.cond(settled(carry), lambda c: c, remaining_bits(RADIX_FIRST_CHECK), carry)

    tie = jnp.max(((n_ge > topk) & (vstar > KEY_NEG)).astype(jnp.int32)) > 0

    def tie_index(_):
        n_gt = count(lambda k, _: k > vstar)

        def idx_step(i, jx):
            cand = jx | (jnp.int32(1) << (29 - i))
            n = n_gt + count(lambda k, idx: (k == vstar) & (idx < cand))
            return jnp.where(n <= topk, cand, jx)
        return lax.fori_loop(0, 30, idx_step, jnp.zeros((1, tq), jnp.int32))

    def store_bias(kb, sel):
        keys_ref[kb] = lax.bitcast_convert_type(jnp.where(sel, 0.0, NEG).astype(F32), jnp.int32)

    def bias_with_ties(_):
        jx = tie_index(0)

        def body(kb, carry):
            k = keys_ref[kb]
            store_bias(kb, ((k > vstar) | ((k == vstar) & ((row + kb * tk) < jx))) & (k > KEY_NEG))
            return carry

        lax.fori_loop(0, nkb, body, 0)
        return 0

    def bias_without_ties(_):
        threshold = jnp.maximum(vstar, KEY_NEG + 1)

        def body(kb, carry):
            store_bias(kb, keys_ref[kb] >= threshold)
            return carry

        lax.fori_loop(0, nkb, body, 0)
        return 0

    lax.cond(tie, bias_with_ties, bias_without_ties, 0)

    def score_fn(kb, h):
        start = pl.multiple_of(kb * tk, tk)
        hs = slice(h * HEAD_DIM, (h + 1) * HEAD_DIM)
        s = jnp.dot(dk_ref[pl.ds(start, tk), hs], dqt_ref[0, 0, hs, :], preferred_element_type=F32)
        return s + lax.bitcast_convert_type(keys_ref[kb], F32)

    def values_fn(kb, h):
        return dvt_ref[0, kb, h * V_ROWS:(h + 1) * V_ROWS, :]

    _attention_core(H_DSA, qi, score_fn, values_fn, None, g_ref, o_ref, *scratch)


def _dsa_attention(dk, dvt, ik, dqt, iqt, iwt, h_plain, batch, seq, topk):
    tq = SEQ_TILE
    nb = seq // tq
    m = batch * seq
    return pl.pallas_call(
        functools.partial(_dsa_kernel, topk=topk),
        out_shape=jax.ShapeDtypeStruct((m, D_DSA), BF16),
        grid=(batch, nb),
        in_specs=[pl.BlockSpec((seq, D_DSA), lambda b, i: (b, 0)),
                  pl.BlockSpec((1, nb, H_DSA * V_ROWS, tq), lambda b, i: (b, 0, 0, 0)),
                  pl.BlockSpec((1, seq, IDX_DIM), lambda b, i: (b, 0, 0)),
                  pl.BlockSpec((1, 1, D_DSA, tq), lambda b, i: (b, i, 0, 0)),
                  pl.BlockSpec((1, 1, IDX_HEADS * IDX_DIM, tq), lambda b, i: (b, i, 0, 0)),
                  pl.BlockSpec((1, 1, IDX_HEADS, tq), lambda b, i: (b, i, 0, 0)),
                  pl.BlockSpec((tq, D_DSA), lambda b, i: (b * nb + i, 2))],
        out_specs=pl.BlockSpec((tq, D_DSA), lambda b, i: (b * nb + i, 0)),
        scratch_shapes=[pltpu.VMEM((nb, tq, tq), jnp.int32)] + _attention_scratch(H_DSA, tq),
        compiler_params=_cparams(("parallel", "parallel"), VMEM_LIMIT_BYTES),
        name="dsa_attention",
    )(dk, dvt, ik, dqt, iqt, iwt, h_plain)


def _memory_attention(q, gate, kv_ref, scale):
    heads = []
    for h in range(H_MEM):
        hs = slice(h * HEAD_DIM, (h + 1) * HEAD_DIM)
        qh = (q[:, hs].astype(F32) * scale).astype(BF16)
        k = kv_ref[0, :, hs]
        v = kv_ref[0, :, D_MEMG + h * HEAD_DIM:D_MEMG + (h + 1) * HEAD_DIM]
        s = lax.dot_general(qh, k, (((1,), (1,)), ((), ())), preferred_element_type=F32)
        m = jnp.max(s, axis=1, keepdims=True)
        p = jnp.exp(s - m)
        l = jnp.sum(p, axis=1, keepdims=True)
        o = jnp.dot(p.astype(BF16), v, preferred_element_type=F32) / l
        heads.append((o * _silu(gate[:, hs].astype(F32))).astype(BF16))
    return jnp.concatenate(heads, axis=1)


def _out_ln_kernel(yf_ref, yd_ref, q0_ref, q1_ref, g0_ref, g1_ref, kv_ref, w_ref, x_ref, g_ref, b_ref,
                   xo_ref, xb_ref, *, alpha, scale):
    for t in range(x_ref.shape[0] // SEQ_TILE):
        rows = slice(t * SEQ_TILE, (t + 1) * SEQ_TILE)
        mq = jnp.concatenate([q0_ref[rows, :], q1_ref[rows, :]], axis=1)
        mg = jnp.concatenate([g0_ref[rows, :], g1_ref[rows, :]], axis=1)
        ym = _memory_attention(mq, mg, kv_ref, scale)
        y = jnp.dot(yf_ref[rows, :], w_ref[0:D_FOX, :], preferred_element_type=F32)
        y = y + jnp.dot(yd_ref[rows, :], w_ref[D_FOX:D_FOX + D_DSA, :], preferred_element_type=F32)
        y = y + jnp.dot(ym, w_ref[D_FOX + D_DSA:, :], preferred_element_type=F32)
        z = alpha * x_ref[rows, :] + y
        mu = jnp.mean(z, axis=1, keepdims=True)
        zc = z - mu
        var = jnp.mean(zc * zc, axis=1, keepdims=True)
        out = zc * lax.rsqrt(var + LN_EPS) * g_ref[...] + b_ref[...]
        xo_ref[rows, :] = out
        xb_ref[rows, :] = out.astype(xb_ref.dtype)


def _out_ln(yf, yd, h_plain, q_col, kv, w_out, x, gain, bias, alpha, scale, seq):
    m, d = x.shape
    tm = 2 * SEQ_TILE
    nb = seq // tm
    n_mem = kv.shape[1]
    cq = q_col // 256
    col = lambda c: pl.BlockSpec((tm, 256), lambda i: (i, c))
    row = lambda n: pl.BlockSpec((tm, n), lambda i: (i, 0))
    const = lambda r, c: pl.BlockSpec((r, c), lambda i: (0, 0))
    return pl.pallas_call(
        functools.partial(_out_ln_kernel, alpha=alpha, scale=scale),
        out_shape=(jax.ShapeDtypeStruct((m, d), F32), jax.ShapeDtypeStruct((m, d), BF16)),
        grid=(m // tm,),
        in_specs=[row(D_FOX), row(D_DSA), col(cq), col(cq + 1), col(cq + 2), col(cq + 3),
                  pl.BlockSpec((1, n_mem, 2 * D_MEMG), lambda i: (i // nb, 0, 0)),
                  const(w_out.shape[0], d), row(d), const(1, d), const(1, d)],
        out_specs=(row(d), row(d)),
        compiler_params=_cparams(("parallel",), VMEM_LIMIT_BYTES),
        name="out_proj_layernorm",
    )(yf, yd, h_plain, h_plain, h_plain, h_plain, kv, w_out, x, gain, bias)


def _rope_tables(seq):
    pos = np.arange(seq, dtype=np.float32)

    def tables(half):
        inv_freq = (np.float32(ROPE_THETA) ** (-np.arange(half, dtype=np.float32) / np.float32(half))).astype(np.float32)
        ang = pos[:, None] * inv_freq[None, :]
        return np.cos(ang).astype(np.float32), np.sin(ang).astype(np.float32)

    c64, s64 = tables(HEAD_DIM // 2)
    c32, s32 = tables(IDX_DIM // 2)
    cos128 = np.concatenate([c64, c64], axis=1)
    sin128 = np.concatenate([-s64, s64], axis=1)
    zeros = np.zeros((seq, 128 - IDX_DIM), np.float32)
    cos_ik = np.concatenate([c32, c32, zeros], axis=1)
    sin_ik = np.concatenate([s32, s32, zeros], axis=1)
    transposed = tuple(np.ascontiguousarray(t.T) for t in (c64, s64, c32, s32))
    return cos128, sin128, transposed, cos_ik, sin_ik


def _prepare(seq, d_model, w_in, b_forget, w_mem_kv, w_out):
    depth = w_in.shape[0]
    splits = (D_FOX, D_FOX, D_FOX, D_FOX, H_FOX, D_DSA, D_DSA, D_DSA, D_DSA,
              IDX_HEADS * IDX_DIM, IDX_DIM, IDX_HEADS, D_MEMG, D_MEMG)
    offs = np.concatenate([[0], np.cumsum(splits)]).tolist()
    seg = lambda i: w_in[:, :, offs[i]:offs[i + 1]]
    (w_fq, w_fk, w_fv, w_fg, w_fl, w_dq, w_dk, w_dv, w_dg, w_iq, w_ik, w_iw, w_mq, w_mg) = [seg(i) for i in range(14)]
    pad = jnp.zeros((depth, d_model, 128 - IDX_DIM - IDX_HEADS - H_FOX), w_in.dtype)
    joined = lambda ws: jnp.concatenate(ws, axis=2).astype(BF16)
    cos128, sin128, t_tables, cos_ik, sin_ik = _rope_tables(seq)
    ts = PREP_TILE
    return dict(
        w_plain=jnp.concatenate([w_fk, w_fg, w_dg, w_mq, w_mg], axis=2).astype(BF16),
        q_col=2 * D_FOX + D_DSA,
        w_dk=w_dk.astype(BF16),
        w_t_dsa=joined([w_dq, w_dv, w_iq]),
        w_t_fox=joined([w_fq, w_fv]),
        w_small=jnp.concatenate([w_ik, w_iw, w_fl, pad], axis=2).astype(BF16),
        w_out=w_out.astype(BF16),
        w_mem=w_mem_kv.astype(BF16),
        bf_rows=jnp.zeros((depth, 1, 128), F32).at[:, 0, OFF_FL:OFF_FL + H_FOX].set(b_forget.astype(F32)),
        cos128=cos128, sin128=sin128, t_tables=t_tables, cos_ik=cos_ik, sin_ik=sin_ik,
        tri=jnp.asarray(np.tril(np.ones((ts, ts), np.float32)), BF16),
    )


def _mixers(xb, mem_b, l, p, batch, seq, n_mem):
    scale = HEAD_DIM ** -0.5
    topk = min(INDEX_TOPK, seq // 4)
    w_scale = (IDX_HEADS ** -0.5) * (IDX_DIM ** -0.5)
    kv = _proj_plain(mem_b, p["w_mem"][l], batch * n_mem, 2 * D_MEMG, "proj_mem_kv")
    kv = kv.reshape(batch, n_mem, 2 * D_MEMG)
    h_plain = _proj_plain(xb, p["w_plain"][l], 2 * SEQ_TILE, p["w_plain"].shape[2], "proj_plain")
    dk = _proj_rope(xb, p["w_dk"][l], p["cos128"], p["sin128"], seq, 2 * SEQ_TILE, "proj_dsa_key")
    dqt, dvt, iqt = _proj_t(xb, p["w_t_dsa"][l], p["t_tables"], batch, seq, scale * LOG2E,
                            (("rope_head_scaled", D_DSA), ("values", D_DSA), ("rope_index", IDX_HEADS * IDX_DIM)),
                            "proj_transposed_dsa")
    fqt, fvt = _proj_t(xb, p["w_t_fox"][l], p["t_tables"], batch, seq, scale * LOG2E,
                       (("scaled", D_FOX), ("values", D_FOX)), "proj_transposed_fox")
    ik, iwt, cb = _prep(xb, p["w_small"][l], p["bf_rows"][l], p["cos_ik"], p["sin_ik"], p["tri"], batch, seq, w_scale)
    y_fox = _fox_attention(h_plain, cb, fqt, fvt, batch, seq)
    y_dsa = _dsa_attention(dk, dvt, ik, dqt, iqt, iwt, h_plain, batch, seq, topk)
    return y_fox, y_dsa, h_plain, kv


def kernel(x, mem, w_in, b_forget, w_mem_kv, w_out, ln_gain, ln_bias):
    batch, seq, d_model = x.shape
    depth = w_in.shape[0]
    n_mem = mem.shape[1]
    m = batch * seq
    alpha = (2 * depth) ** 0.25
    assert seq % (2 * SEQ_TILE) == 0 and min(INDEX_TOPK, seq // 4) <= SEQ_TILE

    p = _prepare(seq, d_model, w_in, b_forget, w_mem_kv, w_out)
    mem_b = mem.reshape(batch * n_mem, d_model).astype(BF16)
    xf = x.reshape(m, d_model).astype(F32)
    xb = xf.astype(BF16)
    for l in range(depth):
        y_fox, y_dsa, h_plain, kv = _mixers(xb, mem_b, l, p, batch, seq, n_mem)
        xf, xb = _out_ln(y_fox, y_dsa, h_plain, p["q_col"], kv, p["w_out"][l], xf,
                         ln_gain[l][None, :].astype(F32), ln_bias[l][None, :].astype(F32),
                         alpha, HEAD_DIM ** -0.5, seq)
    return xf.reshape(batch, seq, d_model).astype(x.dtype)
```

```python
import functools

import numpy as np
import jax
import jax.numpy as jnp
from jax import lax
from jax.experimental import pallas as pl
from jax.experimental.pallas import tpu as pltpu

HEAD_DIM = 128
H_FOX = 6
H_DSA = 6
H_MEM = 4
D_FOX = H_FOX * HEAD_DIM
D_DSA = H_DSA * HEAD_DIM
D_MEMG = H_MEM * HEAD_DIM
IDX_HEADS = 16
IDX_DIM = 64
INDEX_TOPK = 256
ROPE_THETA = 10000.0
LN_EPS = 1e-5
NEG = -1e30
M_INIT = -1e29
LOG2E = float(np.log2(np.e))

SEQ_TILE = 256
PREP_TILE = 2 * SEQ_TILE
PREP_STEP = 4 * PREP_TILE
VMEM_LIMIT_BYTES = 56 * 1024 * 1024
OFF_IW = IDX_DIM
OFF_FL = IDX_DIM + IDX_HEADS
BIAS_PIECES = 3
RADIX_FIRST_CHECK = 23
RADIX_CHECK_EVERY = 4
BLOCKS_PER_STEP = 4
ONES_ROWS = 16
V_ROWS = HEAD_DIM + ONES_ROWS

F32 = jnp.float32
BF16 = jnp.bfloat16


def _sortable_key(bits):
    return bits ^ ((bits >> 31) & 0x7FFFFFFF)


_NEG_BITS = int(np.array(NEG, np.float32).view(np.int32))
KEY_NEG = int(np.int32(_NEG_BITS) ^ np.int32(0x7FFFFFFF))
INT_MIN = -(2 ** 31)


def _cparams(sem, vmem=None):
    return pltpu.CompilerParams(dimension_semantics=sem, vmem_limit_bytes=vmem)


def _silu(g):
    return g / (1.0 + jnp.exp(-g))


COUNT_CHAINS = 4


def _partial_sums(x):
    rows, lanes = x.shape
    grouped = x.reshape(rows // (COUNT_CHAINS * 8), COUNT_CHAINS * 8, lanes)
    return jnp.sum(grouped, axis=0)


def _grouped_loop(n, width, body):
    def group_step(w):
        def step(g, carry):
            for j in range(w):
                body(g * w + j)
            return carry
        return step

    lax.fori_loop(0, n // width, group_step(width), 0)
    w = width // 2
    while w >= 1:
        @pl.when((n & w) != 0)
        def _(w=w):
            group_step(w)((n // (2 * w)) * 2, 0)
        w //= 2


def _split_bf16(x, pieces):
    out = []
    r = x
    for _ in range(pieces - 1):
        p = r.astype(BF16)
        out.append(p)
        r = r - p.astype(F32)
    out.append(r.astype(BF16))
    return out


def _proj_plain_kernel(x_ref, w_ref, o_ref):
    o_ref[...] = jnp.dot(x_ref[...], w_ref[...], preferred_element_type=F32).astype(o_ref.dtype)


def _proj_plain(xb, w, tm, tn, name):
    m, d = xb.shape
    n = w.shape[1]
    return pl.pallas_call(
        _proj_plain_kernel,
        out_shape=jax.ShapeDtypeStruct((m, n), BF16),
        grid=(n // tn, m // tm),
        in_specs=[pl.BlockSpec((tm, d), lambda j, i: (i, 0)),
                  pl.BlockSpec((d, tn), lambda j, i: (0, j))],
        out_specs=pl.BlockSpec((tm, tn), lambda j, i: (i, j)),
        compiler_params=_cparams(("parallel", "parallel"), VMEM_LIMIT_BYTES),
        name=name,
    )(xb, w)


def _proj_rope_kernel(x_ref, w_ref, cos_ref, sin_ref, o_ref, *, heads):
    sub = 2 * SEQ_TILE
    for t in range(x_ref.shape[0] // sub):
        rows = slice(t * sub, (t + 1) * sub)
        y = jnp.dot(x_ref[rows, :], w_ref[...], preferred_element_type=F32)
        c = cos_ref[rows, :]
        s = sin_ref[rows, :]
        for h in range(heads):
            yh = y[:, h * HEAD_DIM:(h + 1) * HEAD_DIM]
            rot = pltpu.roll(yh, HEAD_DIM // 2, axis=1)
            o_ref[rows, h * HEAD_DIM:(h + 1) * HEAD_DIM] = (yh * c + rot * s).astype(o_ref.dtype)


def _proj_rope(xb, w, cos128, sin128, seq, tm, name):
    m, d = xb.shape
    n = w.shape[1]
    nseq = seq // tm
    return pl.pallas_call(
        functools.partial(_proj_rope_kernel, heads=n // HEAD_DIM),
        out_shape=jax.ShapeDtypeStruct((m, n), BF16),
        grid=(m // tm,),
        in_specs=[pl.BlockSpec((tm, d), lambda i: (i, 0)),
                  pl.BlockSpec((d, n), lambda i: (0, 0)),
                  pl.BlockSpec((tm, HEAD_DIM), lambda i: (i % nseq, 0)),
                  pl.BlockSpec((tm, HEAD_DIM), lambda i: (i % nseq, 0))],
        out_specs=pl.BlockSpec((tm, n), lambda i: (i, 0)),
        compiler_params=_cparams(("parallel",), VMEM_LIMIT_BYTES),
        name=name,
    )(xb, w, cos128, sin128)


def _proj_t_kernel(x_ref, w_ref, cos64_ref, sin64_ref, cos32_ref, sin32_ref, *out_refs, groups, scale):
    yt = jnp.dot(x_ref[...], w_ref[...], preferred_element_type=F32).T
    base = 0
    for (kind, rows), o_ref in zip(groups, out_refs):
        y = yt[base:base + rows]
        base += rows
        if kind == "values":
            ones = jnp.ones((ONES_ROWS, y.shape[1]), o_ref.dtype)
            for h in range(rows // HEAD_DIM):
                o_ref[0, 0, h * V_ROWS:h * V_ROWS + HEAD_DIM, :] = y[h * HEAD_DIM:(h + 1) * HEAD_DIM].astype(o_ref.dtype)
                o_ref[0, 0, h * V_ROWS + HEAD_DIM:(h + 1) * V_ROWS, :] = ones
        elif kind == "scaled":
            o_ref[0, 0] = (y * scale).astype(o_ref.dtype)
        else:
            dim, cos_t, sin_t, mul = ((HEAD_DIM, cos64_ref[...], sin64_ref[...], scale) if kind == "rope_head_scaled"
                                      else (IDX_DIM, cos32_ref[...], sin32_ref[...], 1.0))
            half = dim // 2
            for h in range(rows // dim):
                x1 = y[h * dim:h * dim + half]
                x2 = y[h * dim + half:(h + 1) * dim]
                o_ref[0, 0, h * dim:h * dim + half, :] = ((x1 * cos_t - x2 * sin_t) * mul).astype(o_ref.dtype)
                o_ref[0, 0, h * dim + half:(h + 1) * dim, :] = ((x2 * cos_t + x1 * sin_t) * mul).astype(o_ref.dtype)


def _proj_t(xb, w, tables, batch, seq, scale, groups, name):
    m, d = xb.shape
    ts = SEQ_TILE
    nb = seq // ts
    rows = w.shape[1]
    assert rows == sum(r for _, r in groups)
    out_rows = [r // HEAD_DIM * V_ROWS if kind == "values" else r for kind, r in groups]
    blk = lambda r: pl.BlockSpec((1, 1, r, ts), lambda i: (i // nb, i % nb, 0, 0))
    tab = lambda r: pl.BlockSpec((r, ts), lambda i: (0, i % nb))
    return pl.pallas_call(
        functools.partial(_proj_t_kernel, groups=groups, scale=scale),
        out_shape=tuple(jax.ShapeDtypeStruct((batch, nb, r, ts), BF16) for r in out_rows),
        grid=(m // ts,),
        in_specs=[pl.BlockSpec((ts, d), lambda i: (i, 0)),
                  pl.BlockSpec((d, rows), lambda i: (0, 0)),
                  tab(HEAD_DIM // 2), tab(HEAD_DIM // 2), tab(IDX_DIM // 2), tab(IDX_DIM // 2)],
        out_specs=tuple(blk(r) for r in out_rows),
        compiler_params=_cparams(("parallel",), VMEM_LIMIT_BYTES),
        name=name,
    )(xb, w, *tables)


def _prep_kernel(x_ref, w_ref, bf_ref, cos_ref, sin_ref, tri_ref, ik_ref, iwt_ref, cb_ref, carry_ref, *, w_scale):
    j = pl.program_id(1)
    raw_all = jnp.dot(x_ref[...], w_ref[...], preferred_element_type=F32)
    ts = tri_ref.shape[0]
    tri = tri_ref[...]
    q = IDX_DIM // 2
    lane = lax.broadcasted_iota(jnp.int32, (ts, 128), 1)
    r = lax.broadcasted_iota(jnp.int32, (128, 128), 0)
    c = lax.broadcasted_iota(jnp.int32, (128, 128), 1)
    head_row = (r >= OFF_FL) & (r < OFF_FL + H_FOX)

    @pl.when(j == 0)
    def _():
        carry_ref[...] = jnp.zeros_like(carry_ref)

    carry = carry_ref[...]
    for t in range(raw_all.shape[0] // ts):
        rows = slice(t * ts, (t + 1) * ts)
        raw = raw_all[rows]

        rot = jnp.where(lane < q, -pltpu.roll(raw, 128 - q, axis=1), pltpu.roll(raw, q, axis=1))
        ik = raw * cos_ref[rows, :] + rot * sin_ref[rows, :]
        ik_ref[0, rows, :] = ik[:, :IDX_DIM].astype(ik_ref.dtype)

        iw_t = raw.T[OFF_IW:OFF_IW + IDX_HEADS] * w_scale
        for u in range(ts // SEQ_TILE):
            iwt_ref[0, t * (ts // SEQ_TILE) + u] = iw_t[:, u * SEQ_TILE:(u + 1) * SEQ_TILE]

        z = raw + bf_ref[...]
        logf = jnp.minimum(z, 0.0) - jnp.log(1.0 + jnp.exp(-jnp.abs(z)))
        parts = jnp.dot(tri, jnp.concatenate(_split_bf16(logf, 3), axis=1), preferred_element_type=F32)
        cs = parts[:, 0:128] + parts[:, 128:256] + parts[:, 256:384] + carry
        carry = cs[ts - 1:ts, :]

        pieces = jnp.concatenate(_split_bf16(-LOG2E * cs, BIAS_PIECES), axis=1)
        place = jnp.concatenate([(head_row & (c == BIAS_PIECES * (r - OFF_FL) + k)).astype(BF16)
                                 for k in range(BIAS_PIECES)], axis=0)
        cb_ref[0, rows, :] = jnp.dot(pieces, place, preferred_element_type=F32).astype(cb_ref.dtype)
    carry_ref[...] = carry


def _prep(xb, w_small, bf_row, cos_ik, sin_ik, tri, batch, seq, w_scale):
    m, d = xb.shape
    ts = min(PREP_STEP, seq)
    assert seq % ts == 0 and ts % PREP_TILE == 0
    nb = seq // ts
    sub = ts // SEQ_TILE
    out_shapes = (jax.ShapeDtypeStruct((batch, seq, IDX_DIM), BF16),
                  jax.ShapeDtypeStruct((batch, seq // SEQ_TILE, IDX_HEADS, SEQ_TILE), F32),
                  jax.ShapeDtypeStruct((batch, seq, 128), BF16))
    return pl.pallas_call(
        functools.partial(_prep_kernel, w_scale=w_scale),
        out_shape=out_shapes,
        grid=(batch, nb),
        in_specs=[pl.BlockSpec((ts, d), lambda b, j: (b * nb + j, 0)),
                  pl.BlockSpec((d, 128), lambda b, j: (0, 0)),
                  pl.BlockSpec((1, 128), lambda b, j: (0, 0)),
                  pl.BlockSpec((ts, 128), lambda b, j: (j, 0)),
                  pl.BlockSpec((ts, 128), lambda b, j: (j, 0)),
                  pl.BlockSpec((PREP_TILE, PREP_TILE), lambda b, j: (0, 0))],
        out_specs=(pl.BlockSpec((1, ts, IDX_DIM), lambda b, j: (b, j, 0)),
                   pl.BlockSpec((1, sub, IDX_HEADS, SEQ_TILE), lambda b, j: (b, j, 0, 0)),
                   pl.BlockSpec((1, ts, 128), lambda b, j: (b, j, 0))),
        scratch_shapes=[pltpu.VMEM((1, 128), F32)],
        compiler_params=_cparams(("parallel", "arbitrary")),
        name="prep_small",
    )(xb, w_small, bf_row, cos_ik, sin_ik, tri)


def _attention_scratch(heads, tq):
    return [pltpu.VMEM((heads, 1, tq), F32),
            pltpu.VMEM((heads, 1, tq), F32),
            pltpu.VMEM((heads, HEAD_DIM, tq), F32),
            pltpu.VMEM((heads, 1, tq), F32),
            pltpu.VMEM((heads, tq, tq), BF16)]


def _attention_core(heads, qi, score_fn, values_fn, diag_mask, g_ref, o_ref,
                    m_ref, l_ref, acc_ref, a_ref, p_ref):
    m_ref[...] = jnp.full(m_ref.shape, M_INIT, F32)
    l_ref[...] = jnp.zeros(l_ref.shape, F32)
    acc_ref[...] = jnp.zeros(acc_ref.shape, F32)
    a_ref[...] = jnp.ones(a_ref.shape, F32)
    p_ref[...] = jnp.zeros(p_ref.shape, BF16)

    def values(kb, h):
        pv = jnp.dot(values_fn(kb, h), p_ref[h], preferred_element_type=F32)
        a = a_ref[h]
        acc_ref[h] = a * acc_ref[h] + pv[:HEAD_DIM]
        l_ref[h] = a * l_ref[h] + pv[HEAD_DIM:HEAD_DIM + 1]

    def score_softmax(kb, h, last):
        s = score_fn(kb, h)
        if last and diag_mask is not None:
            s = diag_mask(s)
        m = m_ref[h]
        m_new = jnp.maximum(m, jnp.max(s, axis=0, keepdims=True))
        a = jnp.exp2(m - m_new)
        p = jnp.exp2(s - m_new)
        m_ref[h] = m_new
        a_ref[h] = a
        p_ref[h] = p.astype(BF16)

    def block(kb, last):
        prev = jnp.maximum(kb - 1, 0)
        for h in range(heads):
            values(prev, h)
            score_softmax(kb, h, last)

    if diag_mask is None:
        _grouped_loop(qi + 1, BLOCKS_PER_STEP, lambda kb: block(kb, False))
    else:
        _grouped_loop(qi, BLOCKS_PER_STEP, lambda kb: block(kb, False))
        block(qi, True)
    for h in range(heads):
        values(qi, h)
    for h in range(heads):
        hs = slice(h * HEAD_DIM, (h + 1) * HEAD_DIM)
        o = (acc_ref[h] / l_ref[h]).T
        o_ref[:, hs] = (o * _silu(g_ref[:, hs].astype(F32))).astype(o_ref.dtype)


def _fox_kernel(k_ref, cb_ref, qt_ref, vt_ref, g_ref, o_ref, qa_ref, *scratch):
    qi = pl.program_id(1)
    tq = o_ref.shape[0]
    tk = tq
    row = lax.broadcasted_iota(jnp.int32, (tk, tq), 0)
    col = lax.broadcasted_iota(jnp.int32, (tk, tq), 1)
    sel_row = lax.broadcasted_iota(jnp.int32, (HEAD_DIM, tq), 0)
    for h in range(H_FOX):
        qa_ref[h, 0:HEAD_DIM, :] = qt_ref[0, 0, h * HEAD_DIM:(h + 1) * HEAD_DIM, :]
        pick = (sel_row >= BIAS_PIECES * h) & (sel_row < BIAS_PIECES * (h + 1))
        qa_ref[h, HEAD_DIM:2 * HEAD_DIM, :] = pick.astype(BF16)

    def score_fn(kb, h):
        start = pl.multiple_of(kb * tk, tk)
        keys = jnp.concatenate([k_ref[pl.ds(start, tk), h * HEAD_DIM:(h + 1) * HEAD_DIM],
                                cb_ref[0, pl.ds(start, tk), :]], axis=1)
        return jnp.dot(keys, qa_ref[h], preferred_element_type=F32)

    def values_fn(kb, h):
        return vt_ref[0, kb, h * V_ROWS:(h + 1) * V_ROWS, :]

    def diag_mask(s):
        return jnp.where(row <= col, s, NEG)

    _attention_core(H_FOX, qi, score_fn, values_fn, diag_mask, g_ref, o_ref, *scratch)


def _fox_attention(h_plain, cb, fqt, fvt, batch, seq):
    tq = SEQ_TILE
    nb = seq // tq
    m = batch * seq
    return pl.pallas_call(
        _fox_kernel,
        out_shape=jax.ShapeDtypeStruct((m, D_FOX), BF16),
        grid=(batch, nb),
        in_specs=[pl.BlockSpec((seq, D_FOX), lambda b, i: (b, 0)),
                  pl.BlockSpec((1, seq, 128), lambda b, i: (b, 0, 0)),
                  pl.BlockSpec((1, 1, D_FOX, tq), lambda b, i: (b, i, 0, 0)),
                  pl.BlockSpec((1, nb, H_FOX * V_ROWS, tq), lambda b, i: (b, 0, 0, 0)),
                  pl.BlockSpec((tq, D_FOX), lambda b, i: (b * nb + i, 1))],
        out_specs=pl.BlockSpec((tq, D_FOX), lambda b, i: (b * nb + i, 0)),
        scratch_shapes=[pltpu.VMEM((H_FOX, 2 * HEAD_DIM, tq), BF16)] + _attention_scratch(H_FOX, tq),
        compiler_params=_cparams(("parallel", "parallel"), VMEM_LIMIT_BYTES),
        name="fox_attention",
    )(h_plain, cb, fqt, fvt, h_plain)


def _dsa_kernel(dk_ref, dvt_ref, ik_ref, dqt_ref, iqt_ref, iwt_ref, g_ref, o_ref, keys_ref, *scratch, topk):
    qi = pl.program_id(1)
    tq = o_ref.shape[0]
    tk = tq
    nkb = qi + 1
    row = lax.broadcasted_iota(jnp.int32, (tk, tq), 0)
    col = lax.broadcasted_iota(jnp.int32, (tk, tq), 1)

    def score_block(kb):
        start = pl.multiple_of(kb * tk, tk)
        ikb = ik_ref[0, pl.ds(start, tk), :]
        acc = jnp.zeros((tk, tq), F32)
        for h in range(IDX_HEADS):
            z = jnp.dot(ikb, iqt_ref[0, 0, h * IDX_DIM:(h + 1) * IDX_DIM, :], preferred_element_type=F32)
            acc = acc + jnp.maximum(z, 0.0) * iwt_ref[0, 0, h:h + 1, :]
        key = _sortable_key(lax.bitcast_convert_type(acc, jnp.int32))
        keys_ref[kb] = jnp.where((kb == qi) & (row > col), KEY_NEG, key)

    _grouped_loop(nkb, BLOCKS_PER_STEP, score_block)

    def count(pred):
        def body(kb, acc):
            hit = pred(keys_ref[kb], row + kb * tk).astype(jnp.int32)
            return acc + _partial_sums(hit)
        part = lax.fori_loop(0, nkb, body, jnp.zeros((COUNT_CHAINS * 8, tq), jnp.int32))
        return jnp.sum(part, axis=0, keepdims=True)

    n_nonneg = count(lambda k, _: k >= 0)
    nonneg = n_nonneg >= topk
    prefix0 = jnp.where(nonneg, 0, INT_MIN).astype(jnp.int32)
    n_all = jnp.full((1, tq), tk, jnp.int32) * nkb

    def bit_step(i, carry):
        prefix, n_ge = carry
        cand = prefix | (jnp.int32(1) << (30 - i))
        n = count(lambda k, _: k >= cand)
        keep = n >= topk
        return jnp.where(keep, cand, prefix), jnp.where(keep, n, n_ge)

    few = qi * tq + col[0:1, :] + 1 <= topk

    def settled(carry):
        return jnp.min(((carry[1] == topk) | few).astype(jnp.int32)) > 0

    def remaining_bits(first_bit):
        def run(carry):
            last_bit = min(first_bit + RADIX_CHECK_EVERY, 31)
            carry = lax.fori_loop(first_bit, last_bit, bit_step, carry)
            if last_bit == 31:
                return carry
            return lax.cond(settled(carry), lambda c: c, remaining_bits(last_bit), carry)
        return run

    carry = lax.fori_loop(0, RADIX_FIRST_CHECK, bit_step, (prefix0, jnp.where(nonneg, n_nonneg, n_all)))
    vstar, n_ge = lax.cond(settled(carry), lambda c: c, remaining_bits(RADIX_FIRST_CHECK), carry)

    tie = jnp.max(((n_ge > topk) & (vstar > KEY_NEG)).astype(jnp.int32)) > 0

    def tie_index(_):
        n_gt = count(lambda k, _: k > vstar)

        def idx_step(i, jx):
            cand = jx | (jnp.int32(1) << (29 - i))
            n = n_gt + count(lambda k, idx: (k == vstar) & (idx < cand))
            return jnp.where(n <= topk, cand, jx)
        return lax.fori_loop(0, 30, idx_step, jnp.zeros((1, tq), jnp.int32))

    def store_bias(kb, sel):
        keys_ref[kb] = lax.bitcast_convert_type(jnp.where(sel, 0.0, NEG).astype(F32), jnp.int32)

    def bias_with_ties(_):
        jx = tie_index(0)

        def body(kb, carry):
            k = keys_ref[kb]
            store_bias(kb, ((k > vstar) | ((k == vstar) & ((row + kb * tk) < jx))) & (k > KEY_NEG))
            return carry

        lax.fori_loop(0, nkb, body, 0)
        return 0

    def bias_without_ties(_):
        threshold = jnp.maximum(vstar, KEY_NEG + 1)

        def body(kb, carry):
            store_bias(kb, keys_ref[kb] >= threshold)
            return carry

        lax.fori_loop(0, nkb, body, 0)
        return 0

    lax.cond(tie, bias_with_ties, bias_without_ties, 0)

    def score_fn(kb, h):
        start = pl.multiple_of(kb * tk, tk)
        hs = slice(h * HEAD_DIM, (h + 1) * HEAD_DIM)
        s = jnp.dot(dk_ref[pl.ds(start, tk), hs], dqt_ref[0, 0, hs, :], preferred_element_type=F32)
        return s + lax.bitcast_convert_type(keys_ref[kb], F32)

    def values_fn(kb, h):
        return dvt_ref[0, kb, h * V_ROWS:(h + 1) * V_ROWS, :]

    _attention_core(H_DSA, qi, score_fn, values_fn, None, g_ref, o_ref, *scratch)


def _dsa_attention(dk, dvt, ik, dqt, iqt, iwt, h_plain, batch, seq, topk):
    tq = SEQ_TILE
    nb = seq // tq
    m = batch * seq
    return pl.pallas_call(
        functools.partial(_dsa_kernel, topk=topk),
        out_shape=jax.ShapeDtypeStruct((m, D_DSA), BF16),
        grid=(batch, nb),
        in_specs=[pl.BlockSpec((seq, D_DSA), lambda b, i: (b, 0)),
                  pl.BlockSpec((1, nb, H_DSA * V_ROWS, tq), lambda b, i: (b, 0, 0, 0)),
                  pl.BlockSpec((1, seq, IDX_DIM), lambda b, i: (b, 0, 0)),
                  pl.BlockSpec((1, 1, D_DSA, tq), lambda b, i: (b, i, 0, 0)),
                  pl.BlockSpec((1, 1, IDX_HEADS * IDX_DIM, tq), lambda b, i: (b, i, 0, 0)),
                  pl.BlockSpec((1, 1, IDX_HEADS, tq), lambda b, i: (b, i, 0, 0)),
                  pl.BlockSpec((tq, D_DSA), lambda b, i: (b * nb + i, 2))],
        out_specs=pl.BlockSpec((tq, D_DSA), lambda b, i: (b * nb + i, 0)),
        scratch_shapes=[pltpu.VMEM((nb, tq, tq), jnp.int32)] + _attention_scratch(H_DSA, tq),
        compiler_params=_cparams(("parallel", "parallel"), VMEM_LIMIT_BYTES),
        name="dsa_attention",
    )(dk, dvt, ik, dqt, iqt, iwt, h_plain)


def _memory_attention(q, gate, kv_ref, scale):
    heads = []
    for h in range(H_MEM):
        hs = slice(h * HEAD_DIM, (h + 1) * HEAD_DIM)
        qh = (q[:, hs].astype(F32) * scale).astype(BF16)
        k = kv_ref[0, :, hs]
        v = kv_ref[0, :, D_MEMG + h * HEAD_DIM:D_MEMG + (h + 1) * HEAD_DIM]
        s = lax.dot_general(qh, k, (((1,), (1,)), ((), ())), preferred_element_type=F32)
        m = jnp.max(s, axis=1, keepdims=True)
        p = jnp.exp(s - m)
        l = jnp.sum(p, axis=1, keepdims=True)
        o = jnp.dot(p.astype(BF16), v, preferred_element_type=F32) / l
        heads.append((o * _silu(gate[:, hs].astype(F32))).astype(BF16))
    return jnp.concatenate(heads, axis=1)


def _out_ln_kernel(yf_ref, yd_ref, q0_ref, q1_ref, g0_ref, g1_ref, kv_ref, w_ref, x_ref, g_ref, b_ref,
                   xo_ref, xb_ref, *, alpha, scale):
    for t in range(x_ref.shape[0] // SEQ_TILE):
        rows = slice(t * SEQ_TILE, (t + 1) * SEQ_TILE)
        mq = jnp.concatenate([q0_ref[rows, :], q1_ref[rows, :]], axis=1)
        mg = jnp.concatenate([g0_ref[rows, :], g1_ref[rows, :]], axis=1)
        ym = _memory_attention(mq, mg, kv_ref, scale)
        y = jnp.dot(yf_ref[rows, :], w_ref[0:D_FOX, :], preferred_element_type=F32)
        y = y + jnp.dot(yd_ref[rows, :], w_ref[D_FOX:D_FOX + D_DSA, :], preferred_element_type=F32)
        y = y + jnp.dot(ym, w_ref[D_FOX + D_DSA:, :], preferred_element_type=F32)
        z = alpha * x_ref[rows, :] + y
        mu = jnp.mean(z, axis=1, keepdims=True)
        zc = z - mu
        var = jnp.mean(zc * zc, axis=1, keepdims=True)
        out = zc * lax.rsqrt(var + LN_EPS) * g_ref[...] + b_ref[...]
        xo_ref[rows, :] = out
        xb_ref[rows, :] = out.astype(xb_ref.dtype)


def _out_ln(yf, yd, h_plain, q_col, kv, w_out, x, gain, bias, alpha, scale, seq):
    m, d = x.shape
    tm = 2 * SEQ_TILE
    nb = seq // tm
    n_mem = kv.shape[1]
    cq = q_col // 256
    col = lambda c: pl.BlockSpec((tm, 256), lambda i: (i, c))
    row = lambda n: pl.BlockSpec((tm, n), lambda i: (i, 0))
    const = lambda r, c: pl.BlockSpec((r, c), lambda i: (0, 0))
    return pl.pallas_call(
        functools.partial(_out_ln_kernel, alpha=alpha, scale=scale),
        out_shape=(jax.ShapeDtypeStruct((m, d), F32), jax.ShapeDtypeStruct((m, d), BF16)),
        grid=(m // tm,),
        in_specs=[row(D_FOX), row(D_DSA), col(cq), col(cq + 1), col(cq + 2), col(cq + 3),
                  pl.BlockSpec((1, n_mem, 2 * D_MEMG), lambda i: (i // nb, 0, 0)),
                  const(w_out.shape[0], d), row(d), const(1, d), const(1, d)],
        out_specs=(row(d), row(d)),
        compiler_params=_cparams(("parallel",), VMEM_LIMIT_BYTES),
        name="out_proj_layernorm",
    )(yf, yd, h_plain, h_plain, h_plain, h_plain, kv, w_out, x, gain, bias)


def _rope_tables(seq):
    pos = np.arange(seq, dtype=np.float32)

    def tables(half):
        inv_freq = (np.float32(ROPE_THETA) ** (-np.arange(half, dtype=np.float32) / np.float32(half))).astype(np.float32)
        ang = pos[:, None] * inv_freq[None, :]
        return np.cos(ang).astype(np.float32), np.sin(ang).astype(np.float32)

    c64, s64 = tables(HEAD_DIM // 2)
    c32, s32 = tables(IDX_DIM // 2)
    cos128 = np.concatenate([c64, c64], axis=1)
    sin128 = np.concatenate([-s64, s64], axis=1)
    zeros = np.zeros((seq, 128 - IDX_DIM), np.float32)
    cos_ik = np.concatenate([c32, c32, zeros], axis=1)
    sin_ik = np.concatenate([s32, s32, zeros], axis=1)
    transposed = tuple(np.ascontiguousarray(t.T) for t in (c64, s64, c32, s32))
    return cos128, sin128, transposed, cos_ik, sin_ik


def _prepare(seq, d_model, w_in, b_forget, w_mem_kv, w_out):
    depth = w_in.shape[0]
    splits = (D_FOX, D_FOX, D_FOX, D_FOX, H_FOX, D_DSA, D_DSA, D_DSA, D_DSA,
              IDX_HEADS * IDX_DIM, IDX_DIM, IDX_HEADS, D_MEMG, D_MEMG)
    offs = np.concatenate([[0], np.cumsum(splits)]).tolist()
    seg = lambda i: w_in[:, :, offs[i]:offs[i + 1]]
    (w_fq, w_fk, w_fv, w_fg, w_fl, w_dq, w_dk, w_dv, w_dg, w_iq, w_ik, w_iw, w_mq, w_mg) = [seg(i) for i in range(14)]
    pad = jnp.zeros((depth, d_model, 128 - IDX_DIM - IDX_HEADS - H_FOX), w_in.dtype)
    joined = lambda ws: jnp.concatenate(ws, axis=2).astype(BF16)
    cos128, sin128, t_tables, cos_ik, sin_ik = _rope_tables(seq)
    ts = PREP_TILE
    return dict(
        w_plain=jnp.concatenate([w_fk, w_fg, w_dg, w_mq, w_mg], axis=2).astype(BF16),
        q_col=2 * D_FOX + D_DSA,
        w_dk=w_dk.astype(BF16),
        w_t_dsa=joined([w_dq, w_dv, w_iq]),
        w_t_fox=joined([w_fq, w_fv]),
        w_small=jnp.concatenate([w_ik, w_iw, w_fl, pad], axis=2).astype(BF16),
        w_out=w_out.astype(BF16),
        w_mem=w_mem_kv.astype(BF16),
        bf_rows=jnp.zeros((depth, 1, 128), F32).at[:, 0, OFF_FL:OFF_FL + H_FOX].set(b_forget.astype(F32)),
        cos128=cos128, sin128=sin128, t_tables=t_tables, cos_ik=cos_ik, sin_ik=sin_ik,
        tri=jnp.asarray(np.tril(np.ones((ts, ts), np.float32)), BF16),
    )


def _mixers(xb, mem_b, l, p, batch, seq, n_mem):
    scale = HEAD_DIM ** -0.5
    topk = min(INDEX_TOPK, seq // 4)
    w_scale = (IDX_HEADS ** -0.5) * (IDX_DIM ** -0.5)
    kv = _proj_plain(mem_b, p["w_mem"][l], batch * n_mem, 2 * D_MEMG, "proj_mem_kv")
    kv = kv.reshape(batch, n_mem, 2 * D_MEMG)
    h_plain = _proj_plain(xb, p["w_plain"][l], 2 * SEQ_TILE, p["w_plain"].shape[2], "proj_plain")
    dk = _proj_rope(xb, p["w_dk"][l], p["cos128"], p["sin128"], seq, min(4 * SEQ_TILE, seq), "proj_dsa_key")
    dqt, dvt, iqt = _proj_t(xb, p["w_t_dsa"][l], p["t_tables"], batch, seq, scale * LOG2E,
                            (("rope_head_scaled", D_DSA), ("values", D_DSA), ("rope_index", IDX_HEADS * IDX_DIM)),
                            "proj_transposed_dsa")
    fqt, fvt = _proj_t(xb, p["w_t_fox"][l], p["t_tables"], batch, seq, scale * LOG2E,
                       (("scaled", D_FOX), ("values", D_FOX)), "proj_transposed_fox")
    ik, iwt, cb = _prep(xb, p["w_small"][l], p["bf_rows"][l], p["cos_ik"], p["sin_ik"], p["tri"], batch, seq, w_scale)
    y_fox = _fox_attention(h_plain, cb, fqt, fvt, batch, seq)
    y_dsa = _dsa_attention(dk, dvt, ik, dqt, iqt, iwt, h_plain, batch, seq, topk)
    return y_fox, y_dsa, h_plain, kv


def kernel(x, mem, w_in, b_forget, w_mem_kv, w_out, ln_gain, ln_bias):
    batch, seq, d_model = x.shape
    depth = w_in.shape[0]
    n_mem = mem.shape[1]
    m = batch * seq
    alpha = (2 * depth) ** 0.25
    assert seq % (2 * SEQ_TILE) == 0 and min(INDEX_TOPK, seq // 4) <= SEQ_TILE

    p = _prepare(seq, d_model, w_in, b_forget, w_mem_kv, w_out)
    mem_b = mem.reshape(batch * n_mem, d_model).astype(BF16)
    xf = x.reshape(m, d_model).astype(F32)
    xb = xf.astype(BF16)
    for l in range(depth):
        y_fox, y_dsa, h_plain, kv = _mixers(xb, mem_b, l, p, batch, seq, n_mem)
        xf, xb = _out_ln(y_fox, y_dsa, h_plain, p["q_col"], kv, p["w_out"][l], xf,
                         ln_gain[l][None, :].astype(F32), ln_bias[l][None, :].astype(F32),
                         alpha, HEAD_DIM ** -0.5, seq)
    return xf.reshape(batch, seq, d_model).astype(x.dtype)
```

```python
import functools

import numpy as np
import jax
import jax.numpy as jnp
from jax import lax
from jax.experimental import pallas as pl
from jax.experimental.pallas import tpu as pltpu

HEAD_DIM = 128
H_FOX = 6
H_DSA = 6
H_MEM = 4
D_FOX = H_FOX * HEAD_DIM
D_DSA = H_DSA * HEAD_DIM
D_MEMG = H_MEM * HEAD_DIM
IDX_HEADS = 16
IDX_DIM = 64
INDEX_TOPK = 256
ROPE_THETA = 10000.0
LN_EPS = 1e-5
NEG = -1e30
M_INIT = -1e29
LOG2E = float(np.log2(np.e))

SEQ_TILE = 256
PREP_TILE = 2 * SEQ_TILE
PREP_STEP = 4 * PREP_TILE
VMEM_LIMIT_BYTES = 56 * 1024 * 1024
OFF_IW = IDX_DIM
OFF_FL = IDX_DIM + IDX_HEADS
BIAS_PIECES = 3
RADIX_FIRST_CHECK = 23
RADIX_CHECK_EVERY = 4
BLOCKS_PER_STEP = 4
ONES_ROWS = 16
V_ROWS = HEAD_DIM + ONES_ROWS

F32 = jnp.float32
BF16 = jnp.bfloat16


def _sortable_key(bits):
    return bits ^ ((bits >> 31) & 0x7FFFFFFF)


_NEG_BITS = int(np.array(NEG, np.float32).view(np.int32))
KEY_NEG = int(np.int32(_NEG_BITS) ^ np.int32(0x7FFFFFFF))
INT_MIN = -(2 ** 31)


def _cparams(sem, vmem=None):
    return pltpu.CompilerParams(dimension_semantics=sem, vmem_limit_bytes=vmem)


def _silu(g):
    return g / (1.0 + jnp.exp(-g))


COUNT_CHAINS = 4


def _partial_sums(x):
    rows, lanes = x.shape
    grouped = x.reshape(rows // (COUNT_CHAINS * 8), COUNT_CHAINS * 8, lanes)
    return jnp.sum(grouped, axis=0)


def _grouped_loop(n, width, body):
    def group_step(w):
        def step(g, carry):
            for j in range(w):
                body(g * w + j)
            return carry
        return step

    lax.fori_loop(0, n // width, group_step(width), 0)
    w = width // 2
    while w >= 1:
        @pl.when((n & w) != 0)
        def _(w=w):
            group_step(w)((n // (2 * w)) * 2, 0)
        w //= 2


def _split_bf16(x, pieces):
    out = []
    r = x
    for _ in range(pieces - 1):
        p = r.astype(BF16)
        out.append(p)
        r = r - p.astype(F32)
    out.append(r.astype(BF16))
    return out


def _proj_plain_kernel(x_ref, w_ref, o_ref):
    o_ref[...] = jnp.dot(x_ref[...], w_ref[...], preferred_element_type=F32).astype(o_ref.dtype)


def _proj_plain(xb, w, tm, tn, name):
    m, d = xb.shape
    n = w.shape[1]
    return pl.pallas_call(
        _proj_plain_kernel,
        out_shape=jax.ShapeDtypeStruct((m, n), BF16),
        grid=(n // tn, m // tm),
        in_specs=[pl.BlockSpec((tm, d), lambda j, i: (i, 0)),
                  pl.BlockSpec((d, tn), lambda j, i: (0, j))],
        out_specs=pl.BlockSpec((tm, tn), lambda j, i: (i, j)),
        compiler_params=_cparams(("parallel", "parallel"), VMEM_LIMIT_BYTES),
        name=name,
    )(xb, w)


def _proj_rope_kernel(x_ref, w_ref, cos_ref, sin_ref, o_ref, *, heads):
    sub = 2 * SEQ_TILE
    for t in range(x_ref.shape[0] // sub):
        rows = slice(t * sub, (t + 1) * sub)
        y = jnp.dot(x_ref[rows, :], w_ref[...], preferred_element_type=F32)
        c = cos_ref[rows, :]
        s = sin_ref[rows, :]
        for h in range(heads):
            yh = y[:, h * HEAD_DIM:(h + 1) * HEAD_DIM]
            rot = pltpu.roll(yh, HEAD_DIM // 2, axis=1)
            o_ref[rows, h * HEAD_DIM:(h + 1) * HEAD_DIM] = (yh * c + rot * s).astype(o_ref.dtype)


def _proj_rope(xb, w, cos128, sin128, seq, tm, name):
    m, d = xb.shape
    n = w.shape[1]
    nseq = seq // tm
    return pl.pallas_call(
        functools.partial(_proj_rope_kernel, heads=n // HEAD_DIM),
        out_shape=jax.ShapeDtypeStruct((m, n), BF16),
        grid=(m // tm,),
        in_specs=[pl.BlockSpec((tm, d), lambda i: (i, 0)),
                  pl.BlockSpec((d, n), lambda i: (0, 0)),
                  pl.BlockSpec((tm, HEAD_DIM), lambda i: (i % nseq, 0)),
                  pl.BlockSpec((tm, HEAD_DIM), lambda i: (i % nseq, 0))],
        out_specs=pl.BlockSpec((tm, n), lambda i: (i, 0)),
        compiler_params=_cparams(("parallel",), VMEM_LIMIT_BYTES),
        name=name,
    )(xb, w, cos128, sin128)


def _proj_t_kernel(x_ref, w_ref, cos64_ref, sin64_ref, cos32_ref, sin32_ref, *out_refs, groups, scale):
    ts = SEQ_TILE
    for t in range(x_ref.shape[0] // ts):
        pos = slice(t * ts, (t + 1) * ts)
        yt = jnp.dot(x_ref[pos, :], w_ref[...], preferred_element_type=F32).T
        base = 0
        for (kind, rows), o_ref in zip(groups, out_refs):
            y = yt[base:base + rows]
            base += rows
            if kind == "values":
                ones = jnp.ones((ONES_ROWS, ts), o_ref.dtype)
                for h in range(rows // HEAD_DIM):
                    o_ref[0, t, h * V_ROWS:h * V_ROWS + HEAD_DIM, :] = y[h * HEAD_DIM:(h + 1) * HEAD_DIM].astype(o_ref.dtype)
                    o_ref[0, t, h * V_ROWS + HEAD_DIM:(h + 1) * V_ROWS, :] = ones
            elif kind == "scaled":
                o_ref[0, t] = (y * scale).astype(o_ref.dtype)
            else:
                dim, cos_ref, sin_ref, mul = ((HEAD_DIM, cos64_ref, sin64_ref, scale) if kind == "rope_head_scaled"
                                              else (IDX_DIM, cos32_ref, sin32_ref, 1.0))
                cos_t = cos_ref[:, pos]
                sin_t = sin_ref[:, pos]
                half = dim // 2
                for h in range(rows // dim):
                    x1 = y[h * dim:h * dim + half]
                    x2 = y[h * dim + half:(h + 1) * dim]
                    o_ref[0, t, h * dim:h * dim + half, :] = ((x1 * cos_t - x2 * sin_t) * mul).astype(o_ref.dtype)
                    o_ref[0, t, h * dim + half:(h + 1) * dim, :] = ((x2 * cos_t + x1 * sin_t) * mul).astype(o_ref.dtype)


def _proj_t(xb, w, tables, batch, seq, scale, groups, name):
    m, d = xb.shape
    ts = SEQ_TILE
    tiles = 2
    step = tiles * ts
    nb = seq // step
    rows = w.shape[1]
    assert rows == sum(r for _, r in groups) and seq % step == 0
    out_rows = [r // HEAD_DIM * V_ROWS if kind == "values" else r for kind, r in groups]
    blk = lambda r: pl.BlockSpec((1, tiles, r, ts), lambda i: (i // nb, i % nb, 0, 0))
    tab = lambda r: pl.BlockSpec((r, step), lambda i: (0, i % nb))
    return pl.pallas_call(
        functools.partial(_proj_t_kernel, groups=groups, scale=scale),
        out_shape=tuple(jax.ShapeDtypeStruct((batch, seq // ts, r, ts), BF16) for r in out_rows),
        grid=(m // step,),
        in_specs=[pl.BlockSpec((step, d), lambda i: (i, 0)),
                  pl.BlockSpec((d, rows), lambda i: (0, 0)),
                  tab(HEAD_DIM // 2), tab(HEAD_DIM // 2), tab(IDX_DIM // 2), tab(IDX_DIM // 2)],
        out_specs=tuple(blk(r) for r in out_rows),
        compiler_params=_cparams(("parallel",), VMEM_LIMIT_BYTES),
        name=name,
    )(xb, w, *tables)


def _prep_kernel(x_ref, w_ref, bf_ref, cos_ref, sin_ref, tri_ref, ik_ref, iwt_ref, cb_ref, carry_ref, *, w_scale):
    j = pl.program_id(1)
    raw_all = jnp.dot(x_ref[...], w_ref[...], preferred_element_type=F32)
    ts = tri_ref.shape[0]
    tri = tri_ref[...]
    q = IDX_DIM // 2
    lane = lax.broadcasted_iota(jnp.int32, (ts, 128), 1)
    r = lax.broadcasted_iota(jnp.int32, (128, 128), 0)
    c = lax.broadcasted_iota(jnp.int32, (128, 128), 1)
    head_row = (r >= OFF_FL) & (r < OFF_FL + H_FOX)

    @pl.when(j == 0)
    def _():
        carry_ref[...] = jnp.zeros_like(carry_ref)

    carry = carry_ref[...]
    for t in range(raw_all.shape[0] // ts):
        rows = slice(t * ts, (t + 1) * ts)
        raw = raw_all[rows]

        rot = jnp.where(lane < q, -pltpu.roll(raw, 128 - q, axis=1), pltpu.roll(raw, q, axis=1))
        ik = raw * cos_ref[rows, :] + rot * sin_ref[rows, :]
        ik_ref[0, rows, :] = ik[:, :IDX_DIM].astype(ik_ref.dtype)

        iw_t = raw.T[OFF_IW:OFF_IW + IDX_HEADS] * w_scale
        for u in range(ts // SEQ_TILE):
            iwt_ref[0, t * (ts // SEQ_TILE) + u] = iw_t[:, u * SEQ_TILE:(u + 1) * SEQ_TILE]

        z = raw + bf_ref[...]
        logf = jnp.minimum(z, 0.0) - jnp.log(1.0 + jnp.exp(-jnp.abs(z)))
        parts = jnp.dot(tri, jnp.concatenate(_split_bf16(logf, 3), axis=1), preferred_element_type=F32)
        cs = parts[:, 0:128] + parts[:, 128:256] + parts[:, 256:384] + carry
        carry = cs[ts - 1:ts, :]

        pieces = jnp.concatenate(_split_bf16(-LOG2E * cs, BIAS_PIECES), axis=1)
        place = jnp.concatenate([(head_row & (c == BIAS_PIECES * (r - OFF_FL) + k)).astype(BF16)
                                 for k in range(BIAS_PIECES)], axis=0)
        cb_ref[0, rows, :] = jnp.dot(pieces, place, preferred_element_type=F32).astype(cb_ref.dtype)
    carry_ref[...] = carry


def _prep(xb, w_small, bf_row, cos_ik, sin_ik, tri, batch, seq, w_scale):
    m, d = xb.shape
    ts = min(PREP_STEP, seq)
    assert seq % ts == 0 and ts % PREP_TILE == 0
    nb = seq // ts
    sub = ts // SEQ_TILE
    out_shapes = (jax.ShapeDtypeStruct((batch, seq, IDX_DIM), BF16),
                  jax.ShapeDtypeStruct((batch, seq // SEQ_TILE, IDX_HEADS, SEQ_TILE), F32),
                  jax.ShapeDtypeStruct((batch, seq, 128), BF16))
    return pl.pallas_call(
        functools.partial(_prep_kernel, w_scale=w_scale),
        out_shape=out_shapes,
        grid=(batch, nb),
        in_specs=[pl.BlockSpec((ts, d), lambda b, j: (b * nb + j, 0)),
                  pl.BlockSpec((d, 128), lambda b, j: (0, 0)),
                  pl.BlockSpec((1, 128), lambda b, j: (0, 0)),
                  pl.BlockSpec((ts, 128), lambda b, j: (j, 0)),
                  pl.BlockSpec((ts, 128), lambda b, j: (j, 0)),
                  pl.BlockSpec((PREP_TILE, PREP_TILE), lambda b, j: (0, 0))],
        out_specs=(pl.BlockSpec((1, ts, IDX_DIM), lambda b, j: (b, j, 0)),
                   pl.BlockSpec((1, sub, IDX_HEADS, SEQ_TILE), lambda b, j: (b, j, 0, 0)),
                   pl.BlockSpec((1, ts, 128), lambda b, j: (b, j, 0))),
        scratch_shapes=[pltpu.VMEM((1, 128), F32)],
        compiler_params=_cparams(("parallel", "arbitrary")),
        name="prep_small",
    )(xb, w_small, bf_row, cos_ik, sin_ik, tri)


def _attention_scratch(heads, tq):
    return [pltpu.VMEM((heads, 1, tq), F32),
            pltpu.VMEM((heads, 1, tq), F32),
            pltpu.VMEM((heads, HEAD_DIM, tq), F32),
            pltpu.VMEM((heads, 1, tq), F32),
            pltpu.VMEM((heads, tq, tq), BF16)]


def _attention_core(heads, qi, score_fn, values_fn, diag_mask, g_ref, o_ref,
                    m_ref, l_ref, acc_ref, a_ref, p_ref):
    m_ref[...] = jnp.full(m_ref.shape, M_INIT, F32)
    l_ref[...] = jnp.zeros(l_ref.shape, F32)
    acc_ref[...] = jnp.zeros(acc_ref.shape, F32)
    a_ref[...] = jnp.ones(a_ref.shape, F32)
    p_ref[...] = jnp.zeros(p_ref.shape, BF16)

    def values(kb, h):
        pv = jnp.dot(values_fn(kb, h), p_ref[h], preferred_element_type=F32)
        a = a_ref[h]
        acc_ref[h] = a * acc_ref[h] + pv[:HEAD_DIM]
        l_ref[h] = a * l_ref[h] + pv[HEAD_DIM:HEAD_DIM + 1]

    def score_softmax(kb, h, last):
        s = score_fn(kb, h)
        if last and diag_mask is not None:
            s = diag_mask(s)
        m = m_ref[h]
        m_new = jnp.maximum(m, jnp.max(s, axis=0, keepdims=True))
        a = jnp.exp2(m - m_new)
        p = jnp.exp2(s - m_new)
        m_ref[h] = m_new
        a_ref[h] = a
        p_ref[h] = p.astype(BF16)

    def block(kb, last):
        prev = jnp.maximum(kb - 1, 0)
        for h in range(heads):
            values(prev, h)
            score_softmax(kb, h, last)

    if diag_mask is None:
        _grouped_loop(qi + 1, BLOCKS_PER_STEP, lambda kb: block(kb, False))
    else:
        _grouped_loop(qi, BLOCKS_PER_STEP, lambda kb: block(kb, False))
        block(qi, True)
    for h in range(heads):
        values(qi, h)
    for h in range(heads):
        hs = slice(h * HEAD_DIM, (h + 1) * HEAD_DIM)
        o = (acc_ref[h] / l_ref[h]).T
        o_ref[:, hs] = (o * _silu(g_ref[:, hs].astype(F32))).astype(o_ref.dtype)


def _fox_kernel(k_ref, cb_ref, qt_ref, vt_ref, g_ref, o_ref, qa_ref, *scratch):
    qi = pl.program_id(1)
    tq = o_ref.shape[0]
    tk = tq
    row = lax.broadcasted_iota(jnp.int32, (tk, tq), 0)
    col = lax.broadcasted_iota(jnp.int32, (tk, tq), 1)
    sel_row = lax.broadcasted_iota(jnp.int32, (HEAD_DIM, tq), 0)
    for h in range(H_FOX):
        qa_ref[h, 0:HEAD_DIM, :] = qt_ref[0, 0, h * HEAD_DIM:(h + 1) * HEAD_DIM, :]
        pick = (sel_row >= BIAS_PIECES * h) & (sel_row < BIAS_PIECES * (h + 1))
        qa_ref[h, HEAD_DIM:2 * HEAD_DIM, :] = pick.astype(BF16)

    def score_fn(kb, h):
        start = pl.multiple_of(kb * tk, tk)
        keys = jnp.concatenate([k_ref[pl.ds(start, tk), h * HEAD_DIM:(h + 1) * HEAD_DIM],
                                cb_ref[0, pl.ds(start, tk), :]], axis=1)
        return jnp.dot(keys, qa_ref[h], preferred_element_type=F32)

    def values_fn(kb, h):
        return vt_ref[0, kb, h * V_ROWS:(h + 1) * V_ROWS, :]

    def diag_mask(s):
        return jnp.where(row <= col, s, NEG)

    _attention_core(H_FOX, qi, score_fn, values_fn, diag_mask, g_ref, o_ref, *scratch)


def _fox_attention(h_plain, cb, fqt, fvt, batch, seq):
    tq = SEQ_TILE
    nb = seq // tq
    m = batch * seq
    return pl.pallas_call(
        _fox_kernel,
        out_shape=jax.ShapeDtypeStruct((m, D_FOX), BF16),
        grid=(batch, nb),
        in_specs=[pl.BlockSpec((seq, D_FOX), lambda b, i: (b, 0)),
                  pl.BlockSpec((1, seq, 128), lambda b, i: (b, 0, 0)),
                  pl.BlockSpec((1, 1, D_FOX, tq), lambda b, i: (b, i, 0, 0)),
                  pl.BlockSpec((1, nb, H_FOX * V_ROWS, tq), lambda b, i: (b, 0, 0, 0)),
                  pl.BlockSpec((tq, D_FOX), lambda b, i: (b * nb + i, 1))],
        out_specs=pl.BlockSpec((tq, D_FOX), lambda b, i: (b * nb + i, 0)),
        scratch_shapes=[pltpu.VMEM((H_FOX, 2 * HEAD_DIM, tq), BF16)] + _attention_scratch(H_FOX, tq),
        compiler_params=_cparams(("parallel", "parallel"), VMEM_LIMIT_BYTES),
        name="fox_attention",
    )(h_plain, cb, fqt, fvt, h_plain)


def _dsa_kernel(dk_ref, dvt_ref, ik_ref, dqt_ref, iqt_ref, iwt_ref, g_ref, o_ref, keys_ref, *scratch, topk):
    qi = pl.program_id(1)
    tq = o_ref.shape[0]
    tk = tq
    nkb = qi + 1
    row = lax.broadcasted_iota(jnp.int32, (tk, tq), 0)
    col = lax.broadcasted_iota(jnp.int32, (tk, tq), 1)

    def score_block(kb):
        start = pl.multiple_of(kb * tk, tk)
        ikb = ik_ref[0, pl.ds(start, tk), :]
        acc = jnp.zeros((tk, tq), F32)
        for h in range(IDX_HEADS):
            z = jnp.dot(ikb, iqt_ref[0, 0, h * IDX_DIM:(h + 1) * IDX_DIM, :], preferred_element_type=F32)
            acc = acc + jnp.maximum(z, 0.0) * iwt_ref[0, 0, h:h + 1, :]
        key = _sortable_key(lax.bitcast_convert_type(acc, jnp.int32))
        keys_ref[kb] = jnp.where((kb == qi) & (row > col), KEY_NEG, key)

    _grouped_loop(nkb, BLOCKS_PER_STEP, score_block)

    def count(pred):
        def body(kb, acc):
            hit = pred(keys_ref[kb], row + kb * tk).astype(jnp.int32)
            return acc + _partial_sums(hit)
        part = lax.fori_loop(0, nkb, body, jnp.zeros((COUNT_CHAINS * 8, tq), jnp.int32))
        return jnp.sum(part, axis=0, keepdims=True)

    n_nonneg = count(lambda k, _: k >= 0)
    nonneg = n_nonneg >= topk
    prefix0 = jnp.where(nonneg, 0, INT_MIN).astype(jnp.int32)
    n_all = jnp.full((1, tq), tk, jnp.int32) * nkb

    def bit_step(i, carry):
        prefix, n_ge = carry
        cand = prefix | (jnp.int32(1) << (30 - i))
        n = count(lambda k, _: k >= cand)
        keep = n >= topk
        return jnp.where(keep, cand, prefix), jnp.where(keep, n, n_ge)

    few = qi * tq + col[0:1, :] + 1 <= topk

    def settled(carry):
        return jnp.min(((carry[1] == topk) | few).astype(jnp.int32)) > 0

    def remaining_bits(first_bit):
        def run(carry):
            last_bit = min(first_bit + RADIX_CHECK_EVERY, 31)
            carry = lax.fori_loop(first_bit, last_bit, bit_step, carry)
            if last_bit == 31:
                return carry
            return lax.cond(settled(carry), lambda c: c, remaining_bits(last_bit), carry)
        return run

    carry = lax.fori_loop(0, RADIX_FIRST_CHECK, bit_step, (prefix0, jnp.where(nonneg, n_nonneg, n_all)))
    vstar, n_ge = lax.cond(settled(carry), lambda c: c, remaining_bits(RADIX_FIRST_CHECK), carry)

    tie = jnp.max(((n_ge > topk) & (vstar > KEY_NEG)).astype(jnp.int32)) > 0

    def tie_index(_):
        n_gt = count(lambda k, _: k > vstar)

        def idx_step(i, jx):
            cand = jx | (jnp.int32(1) << (29 - i))
            n = n_gt + count(lambda k, idx: (k == vstar) & (idx < cand))
            return jnp.where(n <= topk, cand, jx)
        return lax.fori_loop(0, 30, idx_step, jnp.zeros((1, tq), jnp.int32))

    def store_bias(kb, sel):
        keys_ref[kb] = lax.bitcast_convert_type(jnp.where(sel, 0.0, NEG).astype(F32), jnp.int32)

    def bias_with_ties(_):
        jx = tie_index(0)

        def body(kb, carry):
            k = keys_ref[kb]
            store_bias(kb, ((k > vstar) | ((k == vstar) & ((row + kb * tk) < jx))) & (k > KEY_NEG))
            return carry

        lax.fori_loop(0, nkb, body, 0)
        return 0

    def bias_without_ties(_):
        threshold = jnp.maximum(vstar, KEY_NEG + 1)

        def body(kb, carry):
            store_bias(kb, keys_ref[kb] >= threshold)
            return carry

        lax.fori_loop(0, nkb, body, 0)
        return 0

    lax.cond(tie, bias_with_ties, bias_without_ties, 0)

    def score_fn(kb, h):
        start = pl.multiple_of(kb * tk, tk)
        hs = slice(h * HEAD_DIM, (h + 1) * HEAD_DIM)
        s = jnp.dot(dk_ref[pl.ds(start, tk), hs], dqt_ref[0, 0, hs, :], preferred_element_type=F32)
        return s + lax.bitcast_convert_type(keys_ref[kb], F32)

    def values_fn(kb, h):
        return dvt_ref[0, kb, h * V_ROWS:(h + 1) * V_ROWS, :]

    _attention_core(H_DSA, qi, score_fn, values_fn, None, g_ref, o_ref, *scratch)


def _dsa_attention(dk, dvt, ik, dqt, iqt, iwt, h_plain, batch, seq, topk):
    tq = SEQ_TILE
    nb = seq // tq
    m = batch * seq
    return pl.pallas_call(
        functools.partial(_dsa_kernel, topk=topk),
        out_shape=jax.ShapeDtypeStruct((m, D_DSA), BF16),
        grid=(batch, nb),
        in_specs=[pl.BlockSpec((seq, D_DSA), lambda b, i: (b, 0)),
                  pl.BlockSpec((1, nb, H_DSA * V_ROWS, tq), lambda b, i: (b, 0, 0, 0)),
                  pl.BlockSpec((1, seq, IDX_DIM), lambda b, i: (b, 0, 0)),
                  pl.BlockSpec((1, 1, D_DSA, tq), lambda b, i: (b, i, 0, 0)),
                  pl.BlockSpec((1, 1, IDX_HEADS * IDX_DIM, tq), lambda b, i: (b, i, 0, 0)),
                  pl.BlockSpec((1, 1, IDX_HEADS, tq), lambda b, i: (b, i, 0, 0)),
                  pl.BlockSpec((tq, D_DSA), lambda b, i: (b * nb + i, 2))],
        out_specs=pl.BlockSpec((tq, D_DSA), lambda b, i: (b * nb + i, 0)),
        scratch_shapes=[pltpu.VMEM((nb, tq, tq), jnp.int32)] + _attention_scratch(H_DSA, tq),
        compiler_params=_cparams(("parallel", "parallel"), VMEM_LIMIT_BYTES),
        name="dsa_attention",
    )(dk, dvt, ik, dqt, iqt, iwt, h_plain)


def _memory_attention(q, gate, kv_ref, scale):
    heads = []
    for h in range(H_MEM):
        hs = slice(h * HEAD_DIM, (h + 1) * HEAD_DIM)
        qh = (q[:, hs].astype(F32) * scale).astype(BF16)
        k = kv_ref[0, :, hs]
        v = kv_ref[0, :, D_MEMG + h * HEAD_DIM:D_MEMG + (h + 1) * HEAD_DIM]
        s = lax.dot_general(qh, k, (((1,), (1,)), ((), ())), preferred_element_type=F32)
        m = jnp.max(s, axis=1, keepdims=True)
        p = jnp.exp(s - m)
        l = jnp.sum(p, axis=1, keepdims=True)
        o = jnp.dot(p.astype(BF16), v, preferred_element_type=F32) / l
        heads.append((o * _silu(gate[:, hs].astype(F32))).astype(BF16))
    return jnp.concatenate(heads, axis=1)


def _out_ln_kernel(yf_ref, yd_ref, q0_ref, q1_ref, g0_ref, g1_ref, kv_ref, w_ref, x_ref, g_ref, b_ref,
                   xo_ref, xb_ref, *, alpha, scale):
    for t in range(x_ref.shape[0] // SEQ_TILE):
        rows = slice(t * SEQ_TILE, (t + 1) * SEQ_TILE)
        mq = jnp.concatenate([q0_ref[rows, :], q1_ref[rows, :]], axis=1)
        mg = jnp.concatenate([g0_ref[rows, :], g1_ref[rows, :]], axis=1)
        ym = _memory_attention(mq, mg, kv_ref, scale)
        y = jnp.dot(yf_ref[rows, :], w_ref[0:D_FOX, :], preferred_element_type=F32)
        y = y + jnp.dot(yd_ref[rows, :], w_ref[D_FOX:D_FOX + D_DSA, :], preferred_element_type=F32)
        y = y + jnp.dot(ym, w_ref[D_FOX + D_DSA:, :], preferred_element_type=F32)
        z = alpha * x_ref[rows, :] + y
        mu = jnp.mean(z, axis=1, keepdims=True)
        zc = z - mu
        var = jnp.mean(zc * zc, axis=1, keepdims=True)
        out = zc * lax.rsqrt(var + LN_EPS) * g_ref[...] + b_ref[...]
        xo_ref[rows, :] = out
        xb_ref[rows, :] = out.astype(xb_ref.dtype)


def _out_ln(yf, yd, h_plain, q_col, kv, w_out, x, gain, bias, alpha, scale, seq):
    m, d = x.shape
    tm = 2 * SEQ_TILE
    nb = seq // tm
    n_mem = kv.shape[1]
    cq = q_col // 256
    col = lambda c: pl.BlockSpec((tm, 256), lambda i: (i, c))
    row = lambda n: pl.BlockSpec((tm, n), lambda i: (i, 0))
    const = lambda r, c: pl.BlockSpec((r, c), lambda i: (0, 0))
    return pl.pallas_call(
        functools.partial(_out_ln_kernel, alpha=alpha, scale=scale),
        out_shape=(jax.ShapeDtypeStruct((m, d), F32), jax.ShapeDtypeStruct((m, d), BF16)),
        grid=(m // tm,),
        in_specs=[row(D_FOX), row(D_DSA), col(cq), col(cq + 1), col(cq + 2), col(cq + 3),
                  pl.BlockSpec((1, n_mem, 2 * D_MEMG), lambda i: (i // nb, 0, 0)),
                  const(w_out.shape[0], d), row(d), const(1, d), const(1, d)],
        out_specs=(row(d), row(d)),
        compiler_params=_cparams(("parallel",), VMEM_LIMIT_BYTES),
        name="out_proj_layernorm",
    )(yf, yd, h_plain, h_plain, h_plain, h_plain, kv, w_out, x, gain, bias)


def _rope_tables(seq):
    pos = np.arange(seq, dtype=np.float32)

    def tables(half):
        inv_freq = (np.float32(ROPE_THETA) ** (-np.arange(half, dtype=np.float32) / np.float32(half))).astype(np.float32)
        ang = pos[:, None] * inv_freq[None, :]
        return np.cos(ang).astype(np.float32), np.sin(ang).astype(np.float32)

    c64, s64 = tables(HEAD_DIM // 2)
    c32, s32 = tables(IDX_DIM // 2)
    cos128 = np.concatenate([c64, c64], axis=1)
    sin128 = np.concatenate([-s64, s64], axis=1)
    zeros = np.zeros((seq, 128 - IDX_DIM), np.float32)
    cos_ik = np.concatenate([c32, c32, zeros], axis=1)
    sin_ik = np.concatenate([s32, s32, zeros], axis=1)
    transposed = tuple(np.ascontiguousarray(t.T) for t in (c64, s64, c32, s32))
    return cos128, sin128, transposed, cos_ik, sin_ik


def _prepare(seq, d_model, w_in, b_forget, w_mem_kv, w_out):
    depth = w_in.shape[0]
    splits = (D_FOX, D_FOX, D_FOX, D_FOX, H_FOX, D_DSA, D_DSA, D_DSA, D_DSA,
              IDX_HEADS * IDX_DIM, IDX_DIM, IDX_HEADS, D_MEMG, D_MEMG)
    offs = np.concatenate([[0], np.cumsum(splits)]).tolist()
    seg = lambda i: w_in[:, :, offs[i]:offs[i + 1]]
    (w_fq, w_fk, w_fv, w_fg, w_fl, w_dq, w_dk, w_dv, w_dg, w_iq, w_ik, w_iw, w_mq, w_mg) = [seg(i) for i in range(14)]
    pad = jnp.zeros((depth, d_model, 128 - IDX_DIM - IDX_HEADS - H_FOX), w_in.dtype)
    joined = lambda ws: jnp.concatenate(ws, axis=2).astype(BF16)
    cos128, sin128, t_tables, cos_ik, sin_ik = _rope_tables(seq)
    ts = PREP_TILE
    return dict(
        w_plain=jnp.concatenate([w_fk, w_fg, w_dg, w_mq, w_mg], axis=2).astype(BF16),
        q_col=2 * D_FOX + D_DSA,
        w_dk=w_dk.astype(BF16),
        w_t_dsa=joined([w_dq, w_dv, w_iq]),
        w_t_fox=joined([w_fq, w_fv]),
        w_small=jnp.concatenate([w_ik, w_iw, w_fl, pad], axis=2).astype(BF16),
        w_out=w_out.astype(BF16),
        w_mem=w_mem_kv.astype(BF16),
        bf_rows=jnp.zeros((depth, 1, 128), F32).at[:, 0, OFF_FL:OFF_FL + H_FOX].set(b_forget.astype(F32)),
        cos128=cos128, sin128=sin128, t_tables=t_tables, cos_ik=cos_ik, sin_ik=sin_ik,
        tri=jnp.asarray(np.tril(np.ones((ts, ts), np.float32)), BF16),
    )


def _mixers(xb, mem_b, l, p, batch, seq, n_mem):
    scale = HEAD_DIM ** -0.5
    topk = min(INDEX_TOPK, seq // 4)
    w_scale = (IDX_HEADS ** -0.5) * (IDX_DIM ** -0.5)
    kv = _proj_plain(mem_b, p["w_mem"][l], batch * n_mem, 2 * D_MEMG, "proj_mem_kv")
    kv = kv.reshape(batch, n_mem, 2 * D_MEMG)
    h_plain = _proj_plain(xb, p["w_plain"][l], 2 * SEQ_TILE, p["w_plain"].shape[2], "proj_plain")
    dk = _proj_rope(xb, p["w_dk"][l], p["cos128"], p["sin128"], seq, min(4 * SEQ_TILE, seq), "proj_dsa_key")
    dqt, dvt, iqt = _proj_t(xb, p["w_t_dsa"][l], p["t_tables"], batch, seq, scale * LOG2E,
                            (("rope_head_scaled", D_DSA), ("values", D_DSA), ("rope_index", IDX_HEADS * IDX_DIM)),
                            "proj_transposed_dsa")
    fqt, fvt = _proj_t(xb, p["w_t_fox"][l], p["t_tables"], batch, seq, scale * LOG2E,
                       (("scaled", D_FOX), ("values", D_FOX)), "proj_transposed_fox")
    ik, iwt, cb = _prep(xb, p["w_small"][l], p["bf_rows"][l], p["cos_ik"], p["sin_ik"], p["tri"], batch, seq, w_scale)
    y_fox = _fox_attention(h_plain, cb, fqt, fvt, batch, seq)
    y_dsa = _dsa_attention(dk, dvt, ik, dqt, iqt, iwt, h_plain, batch, seq, topk)
    return y_fox, y_dsa, h_plain, kv


def kernel(x, mem, w_in, b_forget, w_mem_kv, w_out, ln_gain, ln_bias):
    batch, seq, d_model = x.shape
    depth = w_in.shape[0]
    n_mem = mem.shape[1]
    m = batch * seq
    alpha = (2 * depth) ** 0.25
    assert seq % (2 * SEQ_TILE) == 0 and min(INDEX_TOPK, seq // 4) <= SEQ_TILE

    p = _prepare(seq, d_model, w_in, b_forget, w_mem_kv, w_out)
    mem_b = mem.reshape(batch * n_mem, d_model).astype(BF16)
    xf = x.reshape(m, d_model).astype(F32)
    xb = xf.astype(BF16)
    for l in range(depth):
        y_fox, y_dsa, h_plain, kv = _mixers(xb, mem_b, l, p, batch, seq, n_mem)
        xf, xb = _out_ln(y_fox, y_dsa, h_plain, p["q_col"], kv, p["w_out"][l], xf,
                         ln_gain[l][None, :].astype(F32), ln_bias[l][None, :].astype(F32),
                         alpha, HEAD_DIM ** -0.5, seq)
    return xf.reshape(batch, seq, d_model).astype(x.dtype)
```

```python
import functools

import numpy as np
import jax
import jax.numpy as jnp
from jax import lax
from jax.experimental import pallas as pl
from jax.experimental.pallas import tpu as pltpu

HEAD_DIM = 128
H_FOX = 6
H_DSA = 6
H_MEM = 4
D_FOX = H_FOX * HEAD_DIM
D_DSA = H_DSA * HEAD_DIM
D_MEMG = H_MEM * HEAD_DIM
IDX_HEADS = 16
IDX_DIM = 64
INDEX_TOPK = 256
ROPE_THETA = 10000.0
LN_EPS = 1e-5
NEG = -1e30
M_INIT = -1e29
LOG2E = float(np.log2(np.e))

SEQ_TILE = 256
PREP_TILE = 2 * SEQ_TILE
PREP_STEP = 4 * PREP_TILE
VMEM_LIMIT_BYTES = 56 * 1024 * 1024
OFF_IW = IDX_DIM
OFF_FL = IDX_DIM + IDX_HEADS
BIAS_PIECES = 3
RADIX_FIRST_CHECK = 23
RADIX_CHECK_EVERY = 4
BLOCKS_PER_STEP = 4
ONES_ROWS = 16
V_ROWS = HEAD_DIM + ONES_ROWS

F32 = jnp.float32
BF16 = jnp.bfloat16


def _sortable_key(bits):
    return bits ^ ((bits >> 31) & 0x7FFFFFFF)


_NEG_BITS = int(np.array(NEG, np.float32).view(np.int32))
KEY_NEG = int(np.int32(_NEG_BITS) ^ np.int32(0x7FFFFFFF))
INT_MIN = -(2 ** 31)


def _cparams(sem, vmem=None):
    return pltpu.CompilerParams(dimension_semantics=sem, vmem_limit_bytes=vmem)


def _silu(g):
    return g / (1.0 + jnp.exp(-g))


COUNT_CHAINS = 4


def _partial_sums(x):
    rows, lanes = x.shape
    grouped = x.reshape(rows // (COUNT_CHAINS * 8), COUNT_CHAINS * 8, lanes)
    return jnp.sum(grouped, axis=0)


def _grouped_loop(n, width, body):
    def group_step(w):
        def step(g, carry):
            for j in range(w):
                body(g * w + j)
            return carry
        return step

    lax.fori_loop(0, n // width, group_step(width), 0)
    w = width // 2
    while w >= 1:
        @pl.when((n & w) != 0)
        def _(w=w):
            group_step(w)((n // (2 * w)) * 2, 0)
        w //= 2


def _split_bf16(x, pieces):
    out = []
    r = x
    for _ in range(pieces - 1):
        p = r.astype(BF16)
        out.append(p)
        r = r - p.astype(F32)
    out.append(r.astype(BF16))
    return out


def _proj_plain_kernel(x_ref, w_ref, o_ref):
    o_ref[...] = jnp.dot(x_ref[...], w_ref[...], preferred_element_type=F32).astype(o_ref.dtype)


def _proj_plain(xb, w, tm, tn, name):
    m, d = xb.shape
    n = w.shape[1]
    return pl.pallas_call(
        _proj_plain_kernel,
        out_shape=jax.ShapeDtypeStruct((m, n), BF16),
        grid=(n // tn, m // tm),
        in_specs=[pl.BlockSpec((tm, d), lambda j, i: (i, 0)),
                  pl.BlockSpec((d, tn), lambda j, i: (0, j))],
        out_specs=pl.BlockSpec((tm, tn), lambda j, i: (i, j)),
        compiler_params=_cparams(("parallel", "parallel"), VMEM_LIMIT_BYTES),
        name=name,
    )(xb, w)


def _proj_rope_kernel(x_ref, w_ref, cos_ref, sin_ref, o_ref, *, heads):
    sub = 2 * SEQ_TILE
    for t in range(x_ref.shape[0] // sub):
        rows = slice(t * sub, (t + 1) * sub)
        y = jnp.dot(x_ref[rows, :], w_ref[...], preferred_element_type=F32)
        c = cos_ref[rows, :]
        s = sin_ref[rows, :]
        for h in range(heads):
            yh = y[:, h * HEAD_DIM:(h + 1) * HEAD_DIM]
            rot = pltpu.roll(yh, HEAD_DIM // 2, axis=1)
            o_ref[rows, h * HEAD_DIM:(h + 1) * HEAD_DIM] = (yh * c + rot * s).astype(o_ref.dtype)


def _proj_rope(xb, w, cos128, sin128, seq, tm, name):
    m, d = xb.shape
    n = w.shape[1]
    nseq = seq // tm
    return pl.pallas_call(
        functools.partial(_proj_rope_kernel, heads=n // HEAD_DIM),
        out_shape=jax.ShapeDtypeStruct((m, n), BF16),
        grid=(m // tm,),
        in_specs=[pl.BlockSpec((tm, d), lambda i: (i, 0)),
                  pl.BlockSpec((d, n), lambda i: (0, 0)),
                  pl.BlockSpec((tm, HEAD_DIM), lambda i: (i % nseq, 0)),
                  pl.BlockSpec((tm, HEAD_DIM), lambda i: (i % nseq, 0))],
        out_specs=pl.BlockSpec((tm, n), lambda i: (i, 0)),
        compiler_params=_cparams(("parallel",), VMEM_LIMIT_BYTES),
        name=name,
    )(xb, w, cos128, sin128)


def _proj_t_kernel(x_ref, w_ref, cos64_ref, sin64_ref, cos32_ref, sin32_ref, *out_refs, groups, scale):
    ts = SEQ_TILE
    for t in range(x_ref.shape[0] // ts):
        pos = slice(t * ts, (t + 1) * ts)
        yt = jnp.dot(x_ref[pos, :], w_ref[...], preferred_element_type=F32).T
        base = 0
        for (kind, rows), o_ref in zip(groups, out_refs):
            y = yt[base:base + rows]
            base += rows
            if kind == "values":
                ones = jnp.ones((ONES_ROWS, ts), o_ref.dtype)
                for h in range(rows // HEAD_DIM):
                    o_ref[0, t, h * V_ROWS:h * V_ROWS + HEAD_DIM, :] = y[h * HEAD_DIM:(h + 1) * HEAD_DIM].astype(o_ref.dtype)
                    o_ref[0, t, h * V_ROWS + HEAD_DIM:(h + 1) * V_ROWS, :] = ones
            elif kind == "scaled":
                o_ref[0, t] = (y * scale).astype(o_ref.dtype)
            else:
                dim, cos_ref, sin_ref, mul = ((HEAD_DIM, cos64_ref, sin64_ref, scale) if kind == "rope_head_scaled"
                                              else (IDX_DIM, cos32_ref, sin32_ref, 1.0))
                cos_t = cos_ref[:, pos]
                sin_t = sin_ref[:, pos]
                half = dim // 2
                for h in range(rows // dim):
                    x1 = y[h * dim:h * dim + half]
                    x2 = y[h * dim + half:(h + 1) * dim]
                    o_ref[0, t, h * dim:h * dim + half, :] = ((x1 * cos_t - x2 * sin_t) * mul).astype(o_ref.dtype)
                    o_ref[0, t, h * dim + half:(h + 1) * dim, :] = ((x2 * cos_t + x1 * sin_t) * mul).astype(o_ref.dtype)


def _proj_t(xb, w, tables, batch, seq, scale, groups, name):
    m, d = xb.shape
    ts = SEQ_TILE
    tiles = 4
    step = tiles * ts
    nb = seq // step
    rows = w.shape[1]
    assert rows == sum(r for _, r in groups) and seq % step == 0
    out_rows = [r // HEAD_DIM * V_ROWS if kind == "values" else r for kind, r in groups]
    blk = lambda r: pl.BlockSpec((1, tiles, r, ts), lambda i: (i // nb, i % nb, 0, 0))
    tab = lambda r: pl.BlockSpec((r, step), lambda i: (0, i % nb))
    return pl.pallas_call(
        functools.partial(_proj_t_kernel, groups=groups, scale=scale),
        out_shape=tuple(jax.ShapeDtypeStruct((batch, seq // ts, r, ts), BF16) for r in out_rows),
        grid=(m // step,),
        in_specs=[pl.BlockSpec((step, d), lambda i: (i, 0)),
                  pl.BlockSpec((d, rows), lambda i: (0, 0)),
                  tab(HEAD_DIM // 2), tab(HEAD_DIM // 2), tab(IDX_DIM // 2), tab(IDX_DIM // 2)],
        out_specs=tuple(blk(r) for r in out_rows),
        compiler_params=_cparams(("parallel",), VMEM_LIMIT_BYTES),
        name=name,
    )(xb, w, *tables)


def _prep_kernel(x_ref, w_ref, bf_ref, cos_ref, sin_ref, tri_ref, ik_ref, iwt_ref, cb_ref, carry_ref, *, w_scale):
    j = pl.program_id(1)
    raw_all = jnp.dot(x_ref[...], w_ref[...], preferred_element_type=F32)
    ts = tri_ref.shape[0]
    tri = tri_ref[...]
    q = IDX_DIM // 2
    lane = lax.broadcasted_iota(jnp.int32, (ts, 128), 1)
    r = lax.broadcasted_iota(jnp.int32, (128, 128), 0)
    c = lax.broadcasted_iota(jnp.int32, (128, 128), 1)
    head_row = (r >= OFF_FL) & (r < OFF_FL + H_FOX)

    @pl.when(j == 0)
    def _():
        carry_ref[...] = jnp.zeros_like(carry_ref)

    carry = carry_ref[...]
    for t in range(raw_all.shape[0] // ts):
        rows = slice(t * ts, (t + 1) * ts)
        raw = raw_all[rows]

        rot = jnp.where(lane < q, -pltpu.roll(raw, 128 - q, axis=1), pltpu.roll(raw, q, axis=1))
        ik = raw * cos_ref[rows, :] + rot * sin_ref[rows, :]
        ik_ref[0, rows, :] = ik[:, :IDX_DIM].astype(ik_ref.dtype)

        iw_t = raw.T[OFF_IW:OFF_IW + IDX_HEADS] * w_scale
        for u in range(ts // SEQ_TILE):
            iwt_ref[0, t * (ts // SEQ_TILE) + u] = iw_t[:, u * SEQ_TILE:(u + 1) * SEQ_TILE]

        z = raw + bf_ref[...]
        logf = jnp.minimum(z, 0.0) - jnp.log(1.0 + jnp.exp(-jnp.abs(z)))
        parts = jnp.dot(tri, jnp.concatenate(_split_bf16(logf, 3), axis=1), preferred_element_type=F32)
        cs = parts[:, 0:128] + parts[:, 128:256] + parts[:, 256:384] + carry
        carry = cs[ts - 1:ts, :]

        pieces = jnp.concatenate(_split_bf16(-LOG2E * cs, BIAS_PIECES), axis=1)
        place = jnp.concatenate([(head_row & (c == BIAS_PIECES * (r - OFF_FL) + k)).astype(BF16)
                                 for k in range(BIAS_PIECES)], axis=0)
        cb_ref[0, rows, :] = jnp.dot(pieces, place, preferred_element_type=F32).astype(cb_ref.dtype)
    carry_ref[...] = carry


def _prep(xb, w_small, bf_row, cos_ik, sin_ik, tri, batch, seq, w_scale):
    m, d = xb.shape
    ts = min(PREP_STEP, seq)
    assert seq % ts == 0 and ts % PREP_TILE == 0
    nb = seq // ts
    sub = ts // SEQ_TILE
    out_shapes = (jax.ShapeDtypeStruct((batch, seq, IDX_DIM), BF16),
                  jax.ShapeDtypeStruct((batch, seq // SEQ_TILE, IDX_HEADS, SEQ_TILE), F32),
                  jax.ShapeDtypeStruct((batch, seq, 128), BF16))
    return pl.pallas_call(
        functools.partial(_prep_kernel, w_scale=w_scale),
        out_shape=out_shapes,
        grid=(batch, nb),
        in_specs=[pl.BlockSpec((ts, d), lambda b, j: (b * nb + j, 0)),
                  pl.BlockSpec((d, 128), lambda b, j: (0, 0)),
                  pl.BlockSpec((1, 128), lambda b, j: (0, 0)),
                  pl.BlockSpec((ts, 128), lambda b, j: (j, 0)),
                  pl.BlockSpec((ts, 128), lambda b, j: (j, 0)),
                  pl.BlockSpec((PREP_TILE, PREP_TILE), lambda b, j: (0, 0))],
        out_specs=(pl.BlockSpec((1, ts, IDX_DIM), lambda b, j: (b, j, 0)),
                   pl.BlockSpec((1, sub, IDX_HEADS, SEQ_TILE), lambda b, j: (b, j, 0, 0)),
                   pl.BlockSpec((1, ts, 128), lambda b, j: (b, j, 0))),
        scratch_shapes=[pltpu.VMEM((1, 128), F32)],
        compiler_params=_cparams(("parallel", "arbitrary")),
        name="prep_small",
    )(xb, w_small, bf_row, cos_ik, sin_ik, tri)


def _attention_scratch(heads, tq):
    return [pltpu.VMEM((heads, 1, tq), F32),
            pltpu.VMEM((heads, 1, tq), F32),
            pltpu.VMEM((heads, HEAD_DIM, tq), F32),
            pltpu.VMEM((heads, 1, tq), F32),
            pltpu.VMEM((heads, tq, tq), BF16)]


def _attention_core(heads, qi, score_fn, values_fn, diag_mask, g_ref, o_ref,
                    m_ref, l_ref, acc_ref, a_ref, p_ref):
    m_ref[...] = jnp.full(m_ref.shape, M_INIT, F32)
    l_ref[...] = jnp.zeros(l_ref.shape, F32)
    acc_ref[...] = jnp.zeros(acc_ref.shape, F32)
    a_ref[...] = jnp.ones(a_ref.shape, F32)
    p_ref[...] = jnp.zeros(p_ref.shape, BF16)

    def values(kb, h):
        pv = jnp.dot(values_fn(kb, h), p_ref[h], preferred_element_type=F32)
        a = a_ref[h]
        acc_ref[h] = a * acc_ref[h] + pv[:HEAD_DIM]
        l_ref[h] = a * l_ref[h] + pv[HEAD_DIM:HEAD_DIM + 1]

    def score_softmax(kb, h, last):
        s = score_fn(kb, h)
        if last and diag_mask is not None:
            s = diag_mask(s)
        m = m_ref[h]
        m_new = jnp.maximum(m, jnp.max(s, axis=0, keepdims=True))
        a = jnp.exp2(m - m_new)
        p = jnp.exp2(s - m_new)
        m_ref[h] = m_new
        a_ref[h] = a
        p_ref[h] = p.astype(BF16)

    def block(kb, last):
        prev = jnp.maximum(kb - 1, 0)
        for h in range(heads):
            values(prev, h)
            score_softmax(kb, h, last)

    if diag_mask is None:
        _grouped_loop(qi + 1, BLOCKS_PER_STEP, lambda kb: block(kb, False))
    else:
        _grouped_loop(qi, BLOCKS_PER_STEP, lambda kb: block(kb, False))
        block(qi, True)
    for h in range(heads):
        values(qi, h)
    for h in range(heads):
        hs = slice(h * HEAD_DIM, (h + 1) * HEAD_DIM)
        o = (acc_ref[h] / l_ref[h]).T
        o_ref[:, hs] = (o * _silu(g_ref[:, hs].astype(F32))).astype(o_ref.dtype)


def _fox_kernel(k_ref, cb_ref, qt_ref, vt_ref, g_ref, o_ref, qa_ref, *scratch):
    qi = pl.program_id(1)
    tq = o_ref.shape[0]
    tk = tq
    row = lax.broadcasted_iota(jnp.int32, (tk, tq), 0)
    col = lax.broadcasted_iota(jnp.int32, (tk, tq), 1)
    sel_row = lax.broadcasted_iota(jnp.int32, (HEAD_DIM, tq), 0)
    for h in range(H_FOX):
        qa_ref[h, 0:HEAD_DIM, :] = qt_ref[0, 0, h * HEAD_DIM:(h + 1) * HEAD_DIM, :]
        pick = (sel_row >= BIAS_PIECES * h) & (sel_row < BIAS_PIECES * (h + 1))
        qa_ref[h, HEAD_DIM:2 * HEAD_DIM, :] = pick.astype(BF16)

    def score_fn(kb, h):
        start = pl.multiple_of(kb * tk, tk)
        keys = jnp.concatenate([k_ref[pl.ds(start, tk), h * HEAD_DIM:(h + 1) * HEAD_DIM],
                                cb_ref[0, pl.ds(start, tk), :]], axis=1)
        return jnp.dot(keys, qa_ref[h], preferred_element_type=F32)

    def values_fn(kb, h):
        return vt_ref[0, kb, h * V_ROWS:(h + 1) * V_ROWS, :]

    def diag_mask(s):
        return jnp.where(row <= col, s, NEG)

    _attention_core(H_FOX, qi, score_fn, values_fn, diag_mask, g_ref, o_ref, *scratch)


def _fox_attention(h_plain, cb, fqt, fvt, batch, seq):
    tq = SEQ_TILE
    nb = seq // tq
    m = batch * seq
    return pl.pallas_call(
        _fox_kernel,
        out_shape=jax.ShapeDtypeStruct((m, D_FOX), BF16),
        grid=(batch, nb),
        in_specs=[pl.BlockSpec((seq, D_FOX), lambda b, i: (b, 0)),
                  pl.BlockSpec((1, seq, 128), lambda b, i: (b, 0, 0)),
                  pl.BlockSpec((1, 1, D_FOX, tq), lambda b, i: (b, i, 0, 0)),
                  pl.BlockSpec((1, nb, H_FOX * V_ROWS, tq), lambda b, i: (b, 0, 0, 0)),
                  pl.BlockSpec((tq, D_FOX), lambda b, i: (b * nb + i, 1))],
        out_specs=pl.BlockSpec((tq, D_FOX), lambda b, i: (b * nb + i, 0)),
        scratch_shapes=[pltpu.VMEM((H_FOX, 2 * HEAD_DIM, tq), BF16)] + _attention_scratch(H_FOX, tq),
        compiler_params=_cparams(("parallel", "parallel"), VMEM_LIMIT_BYTES),
        name="fox_attention",
    )(h_plain, cb, fqt, fvt, h_plain)


def _dsa_kernel(dk_ref, dvt_ref, ik_ref, dqt_ref, iqt_ref, iwt_ref, g_ref, o_ref, keys_ref, *scratch, topk):
    qi = pl.program_id(1)
    tq = o_ref.shape[0]
    tk = tq
    nkb = qi + 1
    row = lax.broadcasted_iota(jnp.int32, (tk, tq), 0)
    col = lax.broadcasted_iota(jnp.int32, (tk, tq), 1)

    def score_block(kb):
        start = pl.multiple_of(kb * tk, tk)
        ikb = ik_ref[0, pl.ds(start, tk), :]
        acc = jnp.zeros((tk, tq), F32)
        for h in range(IDX_HEADS):
            z = jnp.dot(ikb, iqt_ref[0, 0, h * IDX_DIM:(h + 1) * IDX_DIM, :], preferred_element_type=F32)
            acc = acc + jnp.maximum(z, 0.0) * iwt_ref[0, 0, h:h + 1, :]
        key = _sortable_key(lax.bitcast_convert_type(acc, jnp.int32))
        keys_ref[kb] = jnp.where((kb == qi) & (row > col), KEY_NEG, key)

    _grouped_loop(nkb, BLOCKS_PER_STEP, score_block)

    def count(pred):
        def body(kb, acc):
            hit = pred(keys_ref[kb], row + kb * tk).astype(jnp.int32)
            return acc + _partial_sums(hit)
        part = lax.fori_loop(0, nkb, body, jnp.zeros((COUNT_CHAINS * 8, tq), jnp.int32))
        return jnp.sum(part, axis=0, keepdims=True)

    n_nonneg = count(lambda k, _: k >= 0)
    nonneg = n_nonneg >= topk
    prefix0 = jnp.where(nonneg, 0, INT_MIN).astype(jnp.int32)
    n_all = jnp.full((1, tq), tk, jnp.int32) * nkb

    def bit_step(i, carry):
        prefix, n_ge = carry
        cand = prefix | (jnp.int32(1) << (30 - i))
        n = count(lambda k, _: k >= cand)
        keep = n >= topk
        return jnp.where(keep, cand, prefix), jnp.where(keep, n, n_ge)

    few = qi * tq + col[0:1, :] + 1 <= topk

    def settled(carry):
        return jnp.min(((carry[1] == topk) | few).astype(jnp.int32)) > 0

    def remaining_bits(first_bit):
        def run(carry):
            last_bit = min(first_bit + RADIX_CHECK_EVERY, 31)
            carry = lax.fori_loop(first_bit, last_bit, bit_step, carry)
            if last_bit == 31:
                return carry
            return lax.cond(settled(carry), lambda c: c, remaining_bits(last_bit), carry)
        return run

    carry = lax.fori_loop(0, RADIX_FIRST_CHECK, bit_step, (prefix0, jnp.where(nonneg, n_nonneg, n_all)))
    vstar, n_ge = lax.cond(settled(carry), lambda c: c, remaining_bits(RADIX_FIRST_CHECK), carry)

    tie = jnp.max(((n_ge > topk) & (vstar > KEY_NEG)).astype(jnp.int32)) > 0

    def tie_index(_):
        n_gt = count(lambda k, _: k > vstar)

        def idx_step(i, jx):
            cand = jx | (jnp.int32(1) << (29 - i))
            n = n_gt + count(lambda k, idx: (k == vstar) & (idx < cand))
            return jnp.where(n <= topk, cand, jx)
        return lax.fori_loop(0, 30, idx_step, jnp.zeros((1, tq), jnp.int32))

    def store_bias(kb, sel):
        keys_ref[kb] = lax.bitcast_convert_type(jnp.where(sel, 0.0, NEG).astype(F32), jnp.int32)

    def bias_with_ties(_):
        jx = tie_index(0)

        def body(kb, carry):
            k = keys_ref[kb]
            store_bias(kb, ((k > vstar) | ((k == vstar) & ((row + kb * tk) < jx))) & (k > KEY_NEG))
            return carry

        lax.fori_loop(0, nkb, body, 0)
        return 0

    def bias_without_ties(_):
        threshold = jnp.maximum(vstar, KEY_NEG + 1)

        def body(kb, carry):
            store_bias(kb, keys_ref[kb] >= threshold)
            return carry

        lax.fori_loop(0, nkb, body, 0)
        return 0

    lax.cond(tie, bias_with_ties, bias_without_ties, 0)

    def score_fn(kb, h):
        start = pl.multiple_of(kb * tk, tk)
        hs = slice(h * HEAD_DIM, (h + 1) * HEAD_DIM)
        s = jnp.dot(dk_ref[pl.ds(start, tk), hs], dqt_ref[0, 0, hs, :], preferred_element_type=F32)
        return s + lax.bitcast_convert_type(keys_ref[kb], F32)

    def values_fn(kb, h):
        return dvt_ref[0, kb, h * V_ROWS:(h + 1) * V_ROWS, :]

    _attention_core(H_DSA, qi, score_fn, values_fn, None, g_ref, o_ref, *scratch)


def _dsa_attention(dk, dvt, ik, dqt, iqt, iwt, h_plain, batch, seq, topk):
    tq = SEQ_TILE
    nb = seq // tq
    m = batch * seq
    return pl.pallas_call(
        functools.partial(_dsa_kernel, topk=topk),
        out_shape=jax.ShapeDtypeStruct((m, D_DSA), BF16),
        grid=(batch, nb),
        in_specs=[pl.BlockSpec((seq, D_DSA), lambda b, i: (b, 0)),
                  pl.BlockSpec((1, nb, H_DSA * V_ROWS, tq), lambda b, i: (b, 0, 0, 0)),
                  pl.BlockSpec((1, seq, IDX_DIM), lambda b, i: (b, 0, 0)),
                  pl.BlockSpec((1, 1, D_DSA, tq), lambda b, i: (b, i, 0, 0)),
                  pl.BlockSpec((1, 1, IDX_HEADS * IDX_DIM, tq), lambda b, i: (b, i, 0, 0)),
                  pl.BlockSpec((1, 1, IDX_HEADS, tq), lambda b, i: (b, i, 0, 0)),
                  pl.BlockSpec((tq, D_DSA), lambda b, i: (b * nb + i, 2))],
        out_specs=pl.BlockSpec((tq, D_DSA), lambda b, i: (b * nb + i, 0)),
        scratch_shapes=[pltpu.VMEM((nb, tq, tq), jnp.int32)] + _attention_scratch(H_DSA, tq),
        compiler_params=_cparams(("parallel", "parallel"), VMEM_LIMIT_BYTES),
        name="dsa_attention",
    )(dk, dvt, ik, dqt, iqt, iwt, h_plain)


def _memory_attention(q, gate, kv_ref, scale):
    heads = []
    for h in range(H_MEM):
        hs = slice(h * HEAD_DIM, (h + 1) * HEAD_DIM)
        qh = (q[:, hs].astype(F32) * scale).astype(BF16)
        k = kv_ref[0, :, hs]
        v = kv_ref[0, :, D_MEMG + h * HEAD_DIM:D_MEMG + (h + 1) * HEAD_DIM]
        s = lax.dot_general(qh, k, (((1,), (1,)), ((), ())), preferred_element_type=F32)
        m = jnp.max(s, axis=1, keepdims=True)
        p = jnp.exp(s - m)
        l = jnp.sum(p, axis=1, keepdims=True)
        o = jnp.dot(p.astype(BF16), v, preferred_element_type=F32) / l
        heads.append((o * _silu(gate[:, hs].astype(F32))).astype(BF16))
    return jnp.concatenate(heads, axis=1)


def _out_ln_kernel(yf_ref, yd_ref, q0_ref, q1_ref, g0_ref, g1_ref, kv_ref, w_ref, x_ref, g_ref, b_ref,
                   xo_ref, xb_ref, *, alpha, scale):
    for t in range(x_ref.shape[0] // SEQ_TILE):
        rows = slice(t * SEQ_TILE, (t + 1) * SEQ_TILE)
        mq = jnp.concatenate([q0_ref[rows, :], q1_ref[rows, :]], axis=1)
        mg = jnp.concatenate([g0_ref[rows, :], g1_ref[rows, :]], axis=1)
        ym = _memory_attention(mq, mg, kv_ref, scale)
        y = jnp.dot(yf_ref[rows, :], w_ref[0:D_FOX, :], preferred_element_type=F32)
        y = y + jnp.dot(yd_ref[rows, :], w_ref[D_FOX:D_FOX + D_DSA, :], preferred_element_type=F32)
        y = y + jnp.dot(ym, w_ref[D_FOX + D_DSA:, :], preferred_element_type=F32)
        z = alpha * x_ref[rows, :] + y
        mu = jnp.mean(z, axis=1, keepdims=True)
        zc = z - mu
        var = jnp.mean(zc * zc, axis=1, keepdims=True)
        out = zc * lax.rsqrt(var + LN_EPS) * g_ref[...] + b_ref[...]
        xo_ref[rows, :] = out
        xb_ref[rows, :] = out.astype(xb_ref.dtype)


def _out_ln(yf, yd, h_plain, q_col, kv, w_out, x, gain, bias, alpha, scale, seq):
    m, d = x.shape
    tm = 2 * SEQ_TILE
    nb = seq // tm
    n_mem = kv.shape[1]
    cq = q_col // 256
    col = lambda c: pl.BlockSpec((tm, 256), lambda i: (i, c))
    row = lambda n: pl.BlockSpec((tm, n), lambda i: (i, 0))
    const = lambda r, c: pl.BlockSpec((r, c), lambda i: (0, 0))
    return pl.pallas_call(
        functools.partial(_out_ln_kernel, alpha=alpha, scale=scale),
        out_shape=(jax.ShapeDtypeStruct((m, d), F32), jax.ShapeDtypeStruct((m, d), BF16)),
        grid=(m // tm,),
        in_specs=[row(D_FOX), row(D_DSA), col(cq), col(cq + 1), col(cq + 2), col(cq + 3),
                  pl.BlockSpec((1, n_mem, 2 * D_MEMG), lambda i: (i // nb, 0, 0)),
                  const(w_out.shape[0], d), row(d), const(1, d), const(1, d)],
        out_specs=(row(d), row(d)),
        compiler_params=_cparams(("parallel",), VMEM_LIMIT_BYTES),
        name="out_proj_layernorm",
    )(yf, yd, h_plain, h_plain, h_plain, h_plain, kv, w_out, x, gain, bias)


def _rope_tables(seq):
    pos = np.arange(seq, dtype=np.float32)

    def tables(half):
        inv_freq = (np.float32(ROPE_THETA) ** (-np.arange(half, dtype=np.float32) / np.float32(half))).astype(np.float32)
        ang = pos[:, None] * inv_freq[None, :]
        return np.cos(ang).astype(np.float32), np.sin(ang).astype(np.float32)

    c64, s64 = tables(HEAD_DIM // 2)
    c32, s32 = tables(IDX_DIM // 2)
    cos128 = np.concatenate([c64, c64], axis=1)
    sin128 = np.concatenate([-s64, s64], axis=1)
    zeros = np.zeros((seq, 128 - IDX_DIM), np.float32)
    cos_ik = np.concatenate([c32, c32, zeros], axis=1)
    sin_ik = np.concatenate([s32, s32, zeros], axis=1)
    transposed = tuple(np.ascontiguousarray(t.T) for t in (c64, s64, c32, s32))
    return cos128, sin128, transposed, cos_ik, sin_ik


def _prepare(seq, d_model, w_in, b_forget, w_mem_kv, w_out):
    depth = w_in.shape[0]
    splits = (D_FOX, D_FOX, D_FOX, D_FOX, H_FOX, D_DSA, D_DSA, D_DSA, D_DSA,
              IDX_HEADS * IDX_DIM, IDX_DIM, IDX_HEADS, D_MEMG, D_MEMG)
    offs = np.concatenate([[0], np.cumsum(splits)]).tolist()
    seg = lambda i: w_in[:, :, offs[i]:offs[i + 1]]
    (w_fq, w_fk, w_fv, w_fg, w_fl, w_dq, w_dk, w_dv, w_dg, w_iq, w_ik, w_iw, w_mq, w_mg) = [seg(i) for i in range(14)]
    pad = jnp.zeros((depth, d_model, 128 - IDX_DIM - IDX_HEADS - H_FOX), w_in.dtype)
    joined = lambda ws: jnp.concatenate(ws, axis=2).astype(BF16)
    cos128, sin128, t_tables, cos_ik, sin_ik = _rope_tables(seq)
    ts = PREP_TILE
    return dict(
        w_plain=jnp.concatenate([w_fk, w_fg, w_dg, w_mq, w_mg], axis=2).astype(BF16),
        q_col=2 * D_FOX + D_DSA,
        w_dk=w_dk.astype(BF16),
        w_t_dsa=joined([w_dq, w_dv, w_iq]),
        w_t_fox=joined([w_fq, w_fv]),
        w_small=jnp.concatenate([w_ik, w_iw, w_fl, pad], axis=2).astype(BF16),
        w_out=w_out.astype(BF16),
        w_mem=w_mem_kv.astype(BF16),
        bf_rows=jnp.zeros((depth, 1, 128), F32).at[:, 0, OFF_FL:OFF_FL + H_FOX].set(b_forget.astype(F32)),
        cos128=cos128, sin128=sin128, t_tables=t_tables, cos_ik=cos_ik, sin_ik=sin_ik,
        tri=jnp.asarray(np.tril(np.ones((ts, ts), np.float32)), BF16),
    )


def _mixers(xb, mem_b, l, p, batch, seq, n_mem):
    scale = HEAD_DIM ** -0.5
    topk = min(INDEX_TOPK, seq // 4)
    w_scale = (IDX_HEADS ** -0.5) * (IDX_DIM ** -0.5)
    kv = _proj_plain(mem_b, p["w_mem"][l], batch * n_mem, 2 * D_MEMG, "proj_mem_kv")
    kv = kv.reshape(batch, n_mem, 2 * D_MEMG)
    h_plain = _proj_plain(xb, p["w_plain"][l], 2 * SEQ_TILE, p["w_plain"].shape[2], "proj_plain")
    dk = _proj_rope(xb, p["w_dk"][l], p["cos128"], p["sin128"], seq, min(4 * SEQ_TILE, seq), "proj_dsa_key")
    dqt, dvt, iqt = _proj_t(xb, p["w_t_dsa"][l], p["t_tables"], batch, seq, scale * LOG2E,
                            (("rope_head_scaled", D_DSA), ("values", D_DSA), ("rope_index", IDX_HEADS * IDX_DIM)),
                            "proj_transposed_dsa")
    fqt, fvt = _proj_t(xb, p["w_t_fox"][l], p["t_tables"], batch, seq, scale * LOG2E,
                       (("scaled", D_FOX), ("values", D_FOX)), "proj_transposed_fox")
    ik, iwt, cb = _prep(xb, p["w_small"][l], p["bf_rows"][l], p["cos_ik"], p["sin_ik"], p["tri"], batch, seq, w_scale)
    y_fox = _fox_attention(h_plain, cb, fqt, fvt, batch, seq)
    y_dsa = _dsa_attention(dk, dvt, ik, dqt, iqt, iwt, h_plain, batch, seq, topk)
    return y_fox, y_dsa, h_plain, kv


def kernel(x, mem, w_in, b_forget, w_mem_kv, w_out, ln_gain, ln_bias):
    batch, seq, d_model = x.shape
    depth = w_in.shape[0]
    n_mem = mem.shape[1]
    m = batch * seq
    alpha = (2 * depth) ** 0.25
    assert seq % (2 * SEQ_TILE) == 0 and min(INDEX_TOPK, seq // 4) <= SEQ_TILE

    p = _prepare(seq, d_model, w_in, b_forget, w_mem_kv, w_out)
    mem_b = mem.reshape(batch * n_mem, d_model).astype(BF16)
    xf = x.reshape(m, d_model).astype(F32)
    xb = xf.astype(BF16)
    for l in range(depth):
        y_fox, y_dsa, h_plain, kv = _mixers(xb, mem_b, l, p, batch, seq, n_mem)
        xf, xb = _out_ln(y_fox, y_dsa, h_plain, p["q_col"], kv, p["w_out"][l], xf,
                         ln_gain[l][None, :].astype(F32), ln_bias[l][None, :].astype(F32),
                         alpha, HEAD_DIM ** -0.5, seq)
    return xf.reshape(batch, seq, d_model).astype(x.dtype)
```

```python
import functools

import numpy as np
import jax
import jax.numpy as jnp
from jax import lax
from jax.experimental import pallas as pl
from jax.experimental.pallas import tpu as pltpu

HEAD_DIM = 128
H_FOX = 6
H_DSA = 6
H_MEM = 4
D_FOX = H_FOX * HEAD_DIM
D_DSA = H_DSA * HEAD_DIM
D_MEMG = H_MEM * HEAD_DIM
IDX_HEADS = 16
IDX_DIM = 64
INDEX_TOPK = 256
ROPE_THETA = 10000.0
LN_EPS = 1e-5
NEG = -1e30
M_INIT = -1e29
LOG2E = float(np.log2(np.e))

SEQ_TILE = 256
PREP_TILE = 2 * SEQ_TILE
PREP_STEP = 4 * PREP_TILE
VMEM_LIMIT_BYTES = 56 * 1024 * 1024
OFF_IW = IDX_DIM
OFF_FL = IDX_DIM + IDX_HEADS
BIAS_PIECES = 3
RADIX_FIRST_CHECK = 23
RADIX_CHECK_EVERY = 4
BLOCKS_PER_STEP = 4
ONES_ROWS = 16
V_ROWS = HEAD_DIM + ONES_ROWS

F32 = jnp.float32
BF16 = jnp.bfloat16


def _sortable_key(bits):
    return bits ^ ((bits >> 31) & 0x7FFFFFFF)


_NEG_BITS = int(np.array(NEG, np.float32).view(np.int32))
KEY_NEG = int(np.int32(_NEG_BITS) ^ np.int32(0x7FFFFFFF))
INT_MIN = -(2 ** 31)


def _cparams(sem, vmem=None):
    return pltpu.CompilerParams(dimension_semantics=sem, vmem_limit_bytes=vmem)


def _silu(g):
    return g / (1.0 + jnp.exp(-g))


COUNT_CHAINS = 4


def _partial_sums(x):
    rows, lanes = x.shape
    grouped = x.reshape(rows // (COUNT_CHAINS * 8), COUNT_CHAINS * 8, lanes)
    return jnp.sum(grouped, axis=0)


def _grouped_loop(n, width, body):
    def group_step(w):
        def step(g, carry):
            for j in range(w):
                body(g * w + j)
            return carry
        return step

    lax.fori_loop(0, n // width, group_step(width), 0)
    w = width // 2
    while w >= 1:
        @pl.when((n & w) != 0)
        def _(w=w):
            group_step(w)((n // (2 * w)) * 2, 0)
        w //= 2


def _split_bf16(x, pieces):
    out = []
    r = x
    for _ in range(pieces - 1):
        p = r.astype(BF16)
        out.append(p)
        r = r - p.astype(F32)
    out.append(r.astype(BF16))
    return out


def _proj_plain_kernel(x_ref, w_ref, o_ref):
    o_ref[...] = jnp.dot(x_ref[...], w_ref[...], preferred_element_type=F32).astype(o_ref.dtype)


def _proj_plain(xb, w, tm, tn, name):
    m, d = xb.shape
    n = w.shape[1]
    return pl.pallas_call(
        _proj_plain_kernel,
        out_shape=jax.ShapeDtypeStruct((m, n), BF16),
        grid=(n // tn, m // tm),
        in_specs=[pl.BlockSpec((tm, d), lambda j, i: (i, 0)),
                  pl.BlockSpec((d, tn), lambda j, i: (0, j))],
        out_specs=pl.BlockSpec((tm, tn), lambda j, i: (i, j)),
        compiler_params=_cparams(("parallel", "parallel"), VMEM_LIMIT_BYTES),
        name=name,
    )(xb, w)


def _proj_rope_kernel(x_ref, w_ref, cos_ref, sin_ref, o_ref, *, heads):
    sub = 2 * SEQ_TILE
    for t in range(x_ref.shape[0] // sub):
        rows = slice(t * sub, (t + 1) * sub)
        y = jnp.dot(x_ref[rows, :], w_ref[...], preferred_element_type=F32)
        c = cos_ref[rows, :]
        s = sin_ref[rows, :]
        for h in range(heads):
            yh = y[:, h * HEAD_DIM:(h + 1) * HEAD_DIM]
            rot = pltpu.roll(yh, HEAD_DIM // 2, axis=1)
            o_ref[rows, h * HEAD_DIM:(h + 1) * HEAD_DIM] = (yh * c + rot * s).astype(o_ref.dtype)


def _proj_rope(xb, w, cos128, sin128, seq, tm, name):
    m, d = xb.shape
    n = w.shape[1]
    nseq = seq // tm
    return pl.pallas_call(
        functools.partial(_proj_rope_kernel, heads=n // HEAD_DIM),
        out_shape=jax.ShapeDtypeStruct((m, n), BF16),
        grid=(m // tm,),
        in_specs=[pl.BlockSpec((tm, d), lambda i: (i, 0)),
                  pl.BlockSpec((d, n), lambda i: (0, 0)),
                  pl.BlockSpec((tm, HEAD_DIM), lambda i: (i % nseq, 0)),
                  pl.BlockSpec((tm, HEAD_DIM), lambda i: (i % nseq, 0))],
        out_specs=pl.BlockSpec((tm, n), lambda i: (i, 0)),
        compiler_params=_cparams(("parallel",), VMEM_LIMIT_BYTES),
        name=name,
    )(xb, w, cos128, sin128)


def _proj_t_kernel(x_ref, w_ref, cos64_ref, sin64_ref, cos32_ref, sin32_ref, *out_refs, groups, scale):
    ts = SEQ_TILE
    for t in range(x_ref.shape[0] // ts):
        pos = slice(t * ts, (t + 1) * ts)
        yt = jnp.dot(x_ref[pos, :], w_ref[...], preferred_element_type=F32).T
        base = 0
        for (kind, rows), o_ref in zip(groups, out_refs):
            y = yt[base:base + rows]
            base += rows
            if kind == "values":
                ones = jnp.ones((ONES_ROWS, ts), o_ref.dtype)
                for h in range(rows // HEAD_DIM):
                    o_ref[0, t, h * V_ROWS:h * V_ROWS + HEAD_DIM, :] = y[h * HEAD_DIM:(h + 1) * HEAD_DIM].astype(o_ref.dtype)
                    o_ref[0, t, h * V_ROWS + HEAD_DIM:(h + 1) * V_ROWS, :] = ones
            elif kind == "scaled":
                o_ref[0, t] = (y * scale).astype(o_ref.dtype)
            else:
                dim, cos_ref, sin_ref, mul = ((HEAD_DIM, cos64_ref, sin64_ref, scale) if kind == "rope_head_scaled"
                                              else (IDX_DIM, cos32_ref, sin32_ref, 1.0))
                cos_t = cos_ref[:, pos]
                sin_t = sin_ref[:, pos]
                half = dim // 2
                for h in range(rows // dim):
                    x1 = y[h * dim:h * dim + half]
                    x2 = y[h * dim + half:(h + 1) * dim]
                    o_ref[0, t, h * dim:h * dim + half, :] = ((x1 * cos_t - x2 * sin_t) * mul).astype(o_ref.dtype)
                    o_ref[0, t, h * dim + half:(h + 1) * dim, :] = ((x2 * cos_t + x1 * sin_t) * mul).astype(o_ref.dtype)


def _proj_t(xb, w, tables, batch, seq, scale, groups, name):
    m, d = xb.shape
    ts = SEQ_TILE
    tiles = 4
    step = tiles * ts
    nb = seq // step
    rows = w.shape[1]
    assert rows == sum(r for _, r in groups) and seq % step == 0
    out_rows = [r // HEAD_DIM * V_ROWS if kind == "values" else r for kind, r in groups]
    blk = lambda r: pl.BlockSpec((1, tiles, r, ts), lambda i: (i // nb, i % nb, 0, 0))
    tab = lambda r: pl.BlockSpec((r, step), lambda i: (0, i % nb))
    return pl.pallas_call(
        functools.partial(_proj_t_kernel, groups=groups, scale=scale),
        out_shape=tuple(jax.ShapeDtypeStruct((batch, seq // ts, r, ts), BF16) for r in out_rows),
        grid=(m // step,),
        in_specs=[pl.BlockSpec((step, d), lambda i: (i, 0)),
                  pl.BlockSpec((d, rows), lambda i: (0, 0)),
                  tab(HEAD_DIM // 2), tab(HEAD_DIM // 2), tab(IDX_DIM // 2), tab(IDX_DIM // 2)],
        out_specs=tuple(blk(r) for r in out_rows),
        compiler_params=_cparams(("parallel",), VMEM_LIMIT_BYTES),
        name=name,
    )(xb, w, *tables)


def _prep_kernel(x_ref, w_ref, bf_ref, cos_ref, sin_ref, tri_ref, ik_ref, iwt_ref, cb_ref, carry_ref, *, w_scale):
    j = pl.program_id(1)
    raw_all = jnp.dot(x_ref[...], w_ref[...], preferred_element_type=F32)
    ts = tri_ref.shape[0]
    tri = tri_ref[...]
    q = IDX_DIM // 2
    lane = lax.broadcasted_iota(jnp.int32, (ts, 128), 1)
    r = lax.broadcasted_iota(jnp.int32, (128, 128), 0)
    c = lax.broadcasted_iota(jnp.int32, (128, 128), 1)
    head_row = (r >= OFF_FL) & (r < OFF_FL + H_FOX)

    @pl.when(j == 0)
    def _():
        carry_ref[...] = jnp.zeros_like(carry_ref)

    carry = carry_ref[...]
    for t in range(raw_all.shape[0] // ts):
        rows = slice(t * ts, (t + 1) * ts)
        raw = raw_all[rows]

        rot = jnp.where(lane < q, -pltpu.roll(raw, 128 - q, axis=1), pltpu.roll(raw, q, axis=1))
        ik = raw * cos_ref[rows, :] + rot * sin_ref[rows, :]
        ik_ref[0, rows, :] = ik[:, :IDX_DIM].astype(ik_ref.dtype)

        iw_t = raw.T[OFF_IW:OFF_IW + IDX_HEADS] * w_scale
        for u in range(ts // SEQ_TILE):
            iwt_ref[0, t * (ts // SEQ_TILE) + u] = iw_t[:, u * SEQ_TILE:(u + 1) * SEQ_TILE]

        z = raw + bf_ref[...]
        logf = jnp.minimum(z, 0.0) - jnp.log(1.0 + jnp.exp(-jnp.abs(z)))
        parts = jnp.dot(tri, jnp.concatenate(_split_bf16(logf, 3), axis=1), preferred_element_type=F32)
        cs = parts[:, 0:128] + parts[:, 128:256] + parts[:, 256:384] + carry
        carry = cs[ts - 1:ts, :]

        pieces = jnp.concatenate(_split_bf16(-LOG2E * cs, BIAS_PIECES), axis=1)
        place = jnp.concatenate([(head_row & (c == BIAS_PIECES * (r - OFF_FL) + k)).astype(BF16)
                                 for k in range(BIAS_PIECES)], axis=0)
        cb_ref[0, rows, :] = jnp.dot(pieces, place, preferred_element_type=F32).astype(cb_ref.dtype)
    carry_ref[...] = carry


def _prep(xb, w_small, bf_row, cos_ik, sin_ik, tri, batch, seq, w_scale):
    m, d = xb.shape
    ts = min(PREP_STEP, seq)
    assert seq % ts == 0 and ts % PREP_TILE == 0
    nb = seq // ts
    sub = ts // SEQ_TILE
    out_shapes = (jax.ShapeDtypeStruct((batch, seq, IDX_DIM), BF16),
                  jax.ShapeDtypeStruct((batch, seq // SEQ_TILE, IDX_HEADS, SEQ_TILE), F32),
                  jax.ShapeDtypeStruct((batch, seq, 128), BF16))
    return pl.pallas_call(
        functools.partial(_prep_kernel, w_scale=w_scale),
        out_shape=out_shapes,
        grid=(batch, nb),
        in_specs=[pl.BlockSpec((ts, d), lambda b, j: (b * nb + j, 0)),
                  pl.BlockSpec((d, 128), lambda b, j: (0, 0)),
                  pl.BlockSpec((1, 128), lambda b, j: (0, 0)),
                  pl.BlockSpec((ts, 128), lambda b, j: (j, 0)),
                  pl.BlockSpec((ts, 128), lambda b, j: (j, 0)),
                  pl.BlockSpec((PREP_TILE, PREP_TILE), lambda b, j: (0, 0))],
        out_specs=(pl.BlockSpec((1, ts, IDX_DIM), lambda b, j: (b, j, 0)),
                   pl.BlockSpec((1, sub, IDX_HEADS, SEQ_TILE), lambda b, j: (b, j, 0, 0)),
                   pl.BlockSpec((1, ts, 128), lambda b, j: (b, j, 0))),
        scratch_shapes=[pltpu.VMEM((1, 128), F32)],
        compiler_params=_cparams(("parallel", "arbitrary")),
        name="prep_small",
    )(xb, w_small, bf_row, cos_ik, sin_ik, tri)


def _attention_scratch(heads, tq):
    return [pltpu.VMEM((heads, 1, tq), F32),
            pltpu.VMEM((heads, 1, tq), F32),
            pltpu.VMEM((heads, HEAD_DIM, tq), F32),
            pltpu.VMEM((heads, 1, tq), F32),
            pltpu.VMEM((heads, tq, tq), BF16)]


def _attention_core(heads, qi, score_fn, values_fn, diag_mask, g_ref, o_ref,
                    m_ref, l_ref, acc_ref, a_ref, p_ref):
    m_ref[...] = jnp.full(m_ref.shape, M_INIT, F32)
    l_ref[...] = jnp.zeros(l_ref.shape, F32)
    acc_ref[...] = jnp.zeros(acc_ref.shape, F32)
    a_ref[...] = jnp.ones(a_ref.shape, F32)
    p_ref[...] = jnp.zeros(p_ref.shape, BF16)

    def values(kb, h):
        pv = jnp.dot(values_fn(kb, h), p_ref[h], preferred_element_type=F32)
        a = a_ref[h]
        acc_ref[h] = a * acc_ref[h] + pv[:HEAD_DIM]
        l_ref[h] = a * l_ref[h] + pv[HEAD_DIM:HEAD_DIM + 1]

    def score_softmax(kb, h, last):
        s = score_fn(kb, h)
        if last and diag_mask is not None:
            s = diag_mask(s)
        m = m_ref[h]
        m_new = jnp.maximum(m, jnp.max(s, axis=0, keepdims=True))
        a = jnp.exp2(m - m_new)
        p = jnp.exp2(s - m_new)
        m_ref[h] = m_new
        a_ref[h] = a
        p_ref[h] = p.astype(BF16)

    def block(kb, last):
        prev = jnp.maximum(kb - 1, 0)
        for h in range(heads):
            values(prev, h)
            score_softmax(kb, h, last)

    if diag_mask is None:
        _grouped_loop(qi + 1, BLOCKS_PER_STEP, lambda kb: block(kb, False))
    else:
        _grouped_loop((qi >> 1) << 1, BLOCKS_PER_STEP, lambda kb: block(kb, False))

        @pl.when((qi & 1) == 1)
        def _():
            block(qi - 1, False)
            block(qi, True)

        @pl.when((qi & 1) == 0)
        def _():
            block(qi, True)
    for h in range(heads):
        values(qi, h)
    for h in range(heads):
        hs = slice(h * HEAD_DIM, (h + 1) * HEAD_DIM)
        o = (acc_ref[h] / l_ref[h]).T
        o_ref[:, hs] = (o * _silu(g_ref[:, hs].astype(F32))).astype(o_ref.dtype)


def _fox_kernel(k_ref, cb_ref, qt_ref, vt_ref, g_ref, o_ref, qa_ref, *scratch):
    qi = pl.program_id(1)
    tq = o_ref.shape[0]
    tk = tq
    row = lax.broadcasted_iota(jnp.int32, (tk, tq), 0)
    col = lax.broadcasted_iota(jnp.int32, (tk, tq), 1)
    sel_row = lax.broadcasted_iota(jnp.int32, (HEAD_DIM, tq), 0)
    for h in range(H_FOX):
        qa_ref[h, 0:HEAD_DIM, :] = qt_ref[0, 0, h * HEAD_DIM:(h + 1) * HEAD_DIM, :]
        pick = (sel_row >= BIAS_PIECES * h) & (sel_row < BIAS_PIECES * (h + 1))
        qa_ref[h, HEAD_DIM:2 * HEAD_DIM, :] = pick.astype(BF16)

    def score_fn(kb, h):
        start = pl.multiple_of(kb * tk, tk)
        keys = jnp.concatenate([k_ref[pl.ds(start, tk), h * HEAD_DIM:(h + 1) * HEAD_DIM],
                                cb_ref[0, pl.ds(start, tk), :]], axis=1)
        return jnp.dot(keys, qa_ref[h], preferred_element_type=F32)

    def values_fn(kb, h):
        return vt_ref[0, kb, h * V_ROWS:(h + 1) * V_ROWS, :]

    def diag_mask(s):
        return jnp.where(row <= col, s, NEG)

    _attention_core(H_FOX, qi, score_fn, values_fn, diag_mask, g_ref, o_ref, *scratch)


def _fox_attention(h_plain, cb, fqt, fvt, batch, seq):
    tq = SEQ_TILE
    nb = seq // tq
    m = batch * seq
    return pl.pallas_call(
        _fox_kernel,
        out_shape=jax.ShapeDtypeStruct((m, D_FOX), BF16),
        grid=(batch, nb),
        in_specs=[pl.BlockSpec((seq, D_FOX), lambda b, i: (b, 0)),
                  pl.BlockSpec((1, seq, 128), lambda b, i: (b, 0, 0)),
                  pl.BlockSpec((1, 1, D_FOX, tq), lambda b, i: (b, i, 0, 0)),
                  pl.BlockSpec((1, nb, H_FOX * V_ROWS, tq), lambda b, i: (b, 0, 0, 0)),
                  pl.BlockSpec((tq, D_FOX), lambda b, i: (b * nb + i, 1))],
        out_specs=pl.BlockSpec((tq, D_FOX), lambda b, i: (b * nb + i, 0)),
        scratch_shapes=[pltpu.VMEM((H_FOX, 2 * HEAD_DIM, tq), BF16)] + _attention_scratch(H_FOX, tq),
        compiler_params=_cparams(("parallel", "parallel"), VMEM_LIMIT_BYTES),
        name="fox_attention",
    )(h_plain, cb, fqt, fvt, h_plain)


def _dsa_kernel(dk_ref, dvt_ref, ik_ref, dqt_ref, iqt_ref, iwt_ref, g_ref, o_ref, keys_ref, *scratch, topk):
    qi = pl.program_id(1)
    tq = o_ref.shape[0]
    tk = tq
    nkb = qi + 1
    row = lax.broadcasted_iota(jnp.int32, (tk, tq), 0)
    col = lax.broadcasted_iota(jnp.int32, (tk, tq), 1)

    def score_block(kb):
        start = pl.multiple_of(kb * tk, tk)
        ikb = ik_ref[0, pl.ds(start, tk), :]
        acc = jnp.zeros((tk, tq), F32)
        for h in range(IDX_HEADS):
            z = jnp.dot(ikb, iqt_ref[0, 0, h * IDX_DIM:(h + 1) * IDX_DIM, :], preferred_element_type=F32)
            acc = acc + jnp.maximum(z, 0.0) * iwt_ref[0, 0, h:h + 1, :]
        key = _sortable_key(lax.bitcast_convert_type(acc, jnp.int32))
        keys_ref[kb] = jnp.where((kb == qi) & (row > col), KEY_NEG, key)

    _grouped_loop(nkb, BLOCKS_PER_STEP, score_block)

    def count(pred):
        def body(kb, acc):
            hit = pred(keys_ref[kb], row + kb * tk).astype(jnp.int32)
            return acc + _partial_sums(hit)
        part = lax.fori_loop(0, nkb, body, jnp.zeros((COUNT_CHAINS * 8, tq), jnp.int32))
        return jnp.sum(part, axis=0, keepdims=True)

    n_nonneg = count(lambda k, _: k >= 0)
    nonneg = n_nonneg >= topk
    prefix0 = jnp.where(nonneg, 0, INT_MIN).astype(jnp.int32)
    n_all = jnp.full((1, tq), tk, jnp.int32) * nkb

    def bit_step(i, carry):
        prefix, n_ge = carry
        cand = prefix | (jnp.int32(1) << (30 - i))
        n = count(lambda k, _: k >= cand)
        keep = n >= topk
        return jnp.where(keep, cand, prefix), jnp.where(keep, n, n_ge)

    few = qi * tq + col[0:1, :] + 1 <= topk

    def settled(carry):
        return jnp.min(((carry[1] == topk) | few).astype(jnp.int32)) > 0

    def remaining_bits(first_bit):
        def run(carry):
            last_bit = min(first_bit + RADIX_CHECK_EVERY, 31)
            carry = lax.fori_loop(first_bit, last_bit, bit_step, carry)
            if last_bit == 31:
                return carry
            return lax.cond(settled(carry), lambda c: c, remaining_bits(last_bit), carry)
        return run

    carry = lax.fori_loop(0, RADIX_FIRST_CHECK, bit_step, (prefix0, jnp.where(nonneg, n_nonneg, n_all)))
    vstar, n_ge = lax.cond(settled(carry), lambda c: c, remaining_bits(RADIX_FIRST_CHECK), carry)

    tie = jnp.max(((n_ge > topk) & (vstar > KEY_NEG)).astype(jnp.int32)) > 0

    def tie_index(_):
        n_gt = count(lambda k, _: k > vstar)

        def idx_step(i, jx):
            cand = jx | (jnp.int32(1) << (29 - i))
            n = n_gt + count(lambda k, idx: (k == vstar) & (idx < cand))
            return jnp.where(n <= topk, cand, jx)
        return lax.fori_loop(0, 30, idx_step, jnp.zeros((1, tq), jnp.int32))

    def store_bias(kb, sel):
        keys_ref[kb] = lax.bitcast_convert_type(jnp.where(sel, 0.0, NEG).astype(F32), jnp.int32)

    def bias_with_ties(_):
        jx = tie_index(0)

        def body(kb, carry):
            k = keys_ref[kb]
            store_bias(kb, ((k > vstar) | ((k == vstar) & ((row + kb * tk) < jx))) & (k > KEY_NEG))
            return carry

        lax.fori_loop(0, nkb, body, 0)
        return 0

    def bias_without_ties(_):
        threshold = jnp.maximum(vstar, KEY_NEG + 1)

        def body(kb, carry):
            store_bias(kb, keys_ref[kb] >= threshold)
            return carry

        lax.fori_loop(0, nkb, body, 0)
        return 0

    lax.cond(tie, bias_with_ties, bias_without_ties, 0)

    def score_fn(kb, h):
        start = pl.multiple_of(kb * tk, tk)
        hs = slice(h * HEAD_DIM, (h + 1) * HEAD_DIM)
        s = jnp.dot(dk_ref[pl.ds(start, tk), hs], dqt_ref[0, 0, hs, :], preferred_element_type=F32)
        return s + lax.bitcast_convert_type(keys_ref[kb], F32)

    def values_fn(kb, h):
        return dvt_ref[0, kb, h * V_ROWS:(h + 1) * V_ROWS, :]

    _attention_core(H_DSA, qi, score_fn, values_fn, None, g_ref, o_ref, *scratch)


def _dsa_attention(dk, dvt, ik, dqt, iqt, iwt, h_plain, batch, seq, topk):
    tq = SEQ_TILE
    nb = seq // tq
    m = batch * seq
    return pl.pallas_call(
        functools.partial(_dsa_kernel, topk=topk),
        out_shape=jax.ShapeDtypeStruct((m, D_DSA), BF16),
        grid=(batch, nb),
        in_specs=[pl.BlockSpec((seq, D_DSA), lambda b, i: (b, 0)),
                  pl.BlockSpec((1, nb, H_DSA * V_ROWS, tq), lambda b, i: (b, 0, 0, 0)),
                  pl.BlockSpec((1, seq, IDX_DIM), lambda b, i: (b, 0, 0)),
                  pl.BlockSpec((1, 1, D_DSA, tq), lambda b, i: (b, i, 0, 0)),
                  pl.BlockSpec((1, 1, IDX_HEADS * IDX_DIM, tq), lambda b, i: (b, i, 0, 0)),
                  pl.BlockSpec((1, 1, IDX_HEADS, tq), lambda b, i: (b, i, 0, 0)),
                  pl.BlockSpec((tq, D_DSA), lambda b, i: (b * nb + i, 2))],
        out_specs=pl.BlockSpec((tq, D_DSA), lambda b, i: (b * nb + i, 0)),
        scratch_shapes=[pltpu.VMEM((nb, tq, tq), jnp.int32)] + _attention_scratch(H_DSA, tq),
        compiler_params=_cparams(("parallel", "parallel"), VMEM_LIMIT_BYTES),
        name="dsa_attention",
    )(dk, dvt, ik, dqt, iqt, iwt, h_plain)


def _memory_attention(q, gate, kv_ref, scale):
    heads = []
    for h in range(H_MEM):
        hs = slice(h * HEAD_DIM, (h + 1) * HEAD_DIM)
        qh = (q[:, hs].astype(F32) * scale).astype(BF16)
        k = kv_ref[0, :, hs]
        v = kv_ref[0, :, D_MEMG + h * HEAD_DIM:D_MEMG + (h + 1) * HEAD_DIM]
        s = lax.dot_general(qh, k, (((1,), (1,)), ((), ())), preferred_element_type=F32)
        m = jnp.max(s, axis=1, keepdims=True)
        p = jnp.exp(s - m)
        l = jnp.sum(p, axis=1, keepdims=True)
        o = jnp.dot(p.astype(BF16), v, preferred_element_type=F32) / l
        heads.append((o * _silu(gate[:, hs].astype(F32))).astype(BF16))
    return jnp.concatenate(heads, axis=1)


def _out_ln_kernel(yf_ref, yd_ref, q0_ref, q1_ref, g0_ref, g1_ref, kv_ref, w_ref, x_ref, g_ref, b_ref,
                   xo_ref, xb_ref, *, alpha, scale):
    for t in range(x_ref.shape[0] // SEQ_TILE):
        rows = slice(t * SEQ_TILE, (t + 1) * SEQ_TILE)
        mq = jnp.concatenate([q0_ref[rows, :], q1_ref[rows, :]], axis=1)
        mg = jnp.concatenate([g0_ref[rows, :], g1_ref[rows, :]], axis=1)
        ym = _memory_attention(mq, mg, kv_ref, scale)
        y = jnp.dot(yf_ref[rows, :], w_ref[0:D_FOX, :], preferred_element_type=F32)
        y = y + jnp.dot(yd_ref[rows, :], w_ref[D_FOX:D_FOX + D_DSA, :], preferred_element_type=F32)
        y = y + jnp.dot(ym, w_ref[D_FOX + D_DSA:, :], preferred_element_type=F32)
        z = alpha * x_ref[rows, :] + y
        mu = jnp.mean(z, axis=1, keepdims=True)
        zc = z - mu
        var = jnp.mean(zc * zc, axis=1, keepdims=True)
        out = zc * lax.rsqrt(var + LN_EPS) * g_ref[...] + b_ref[...]
        xo_ref[rows, :] = out
        xb_ref[rows, :] = out.astype(xb_ref.dtype)


def _out_ln(yf, yd, h_plain, q_col, kv, w_out, x, gain, bias, alpha, scale, seq):
    m, d = x.shape
    tm = 2 * SEQ_TILE
    nb = seq // tm
    n_mem = kv.shape[1]
    cq = q_col // 256
    col = lambda c: pl.BlockSpec((tm, 256), lambda i: (i, c))
    row = lambda n: pl.BlockSpec((tm, n), lambda i: (i, 0))
    const = lambda r, c: pl.BlockSpec((r, c), lambda i: (0, 0))
    return pl.pallas_call(
        functools.partial(_out_ln_kernel, alpha=alpha, scale=scale),
        out_shape=(jax.ShapeDtypeStruct((m, d), F32), jax.ShapeDtypeStruct((m, d), BF16)),
        grid=(m // tm,),
        in_specs=[row(D_FOX), row(D_DSA), col(cq), col(cq + 1), col(cq + 2), col(cq + 3),
                  pl.BlockSpec((1, n_mem, 2 * D_MEMG), lambda i: (i // nb, 0, 0)),
                  const(w_out.shape[0], d), row(d), const(1, d), const(1, d)],
        out_specs=(row(d), row(d)),
        compiler_params=_cparams(("parallel",), VMEM_LIMIT_BYTES),
        name="out_proj_layernorm",
    )(yf, yd, h_plain, h_plain, h_plain, h_plain, kv, w_out, x, gain, bias)


def _rope_tables(seq):
    pos = np.arange(seq, dtype=np.float32)

    def tables(half):
        inv_freq = (np.float32(ROPE_THETA) ** (-np.arange(half, dtype=np.float32) / np.float32(half))).astype(np.float32)
        ang = pos[:, None] * inv_freq[None, :]
        return np.cos(ang).astype(np.float32), np.sin(ang).astype(np.float32)

    c64, s64 = tables(HEAD_DIM // 2)
    c32, s32 = tables(IDX_DIM // 2)
    cos128 = np.concatenate([c64, c64], axis=1)
    sin128 = np.concatenate([-s64, s64], axis=1)
    zeros = np.zeros((seq, 128 - IDX_DIM), np.float32)
    cos_ik = np.concatenate([c32, c32, zeros], axis=1)
    sin_ik = np.concatenate([s32, s32, zeros], axis=1)
    transposed = tuple(np.ascontiguousarray(t.T) for t in (c64, s64, c32, s32))
    return cos128, sin128, transposed, cos_ik, sin_ik


def _prepare(seq, d_model, w_in, b_forget, w_mem_kv, w_out):
    depth = w_in.shape[0]
    splits = (D_FOX, D_FOX, D_FOX, D_FOX, H_FOX, D_DSA, D_DSA, D_DSA, D_DSA,
              IDX_HEADS * IDX_DIM, IDX_DIM, IDX_HEADS, D_MEMG, D_MEMG)
    offs = np.concatenate([[0], np.cumsum(splits)]).tolist()
    seg = lambda i: w_in[:, :, offs[i]:offs[i + 1]]
    (w_fq, w_fk, w_fv, w_fg, w_fl, w_dq, w_dk, w_dv, w_dg, w_iq, w_ik, w_iw, w_mq, w_mg) = [seg(i) for i in range(14)]
    pad = jnp.zeros((depth, d_model, 128 - IDX_DIM - IDX_HEADS - H_FOX), w_in.dtype)
    joined = lambda ws: jnp.concatenate(ws, axis=2).astype(BF16)
    cos128, sin128, t_tables, cos_ik, sin_ik = _rope_tables(seq)
    ts = PREP_TILE
    return dict(
        w_plain=jnp.concatenate([w_fk, w_fg, w_dg, w_mq, w_mg], axis=2).astype(BF16),
        q_col=2 * D_FOX + D_DSA,
        w_dk=w_dk.astype(BF16),
        w_t_dsa=joined([w_dq, w_dv, w_iq]),
        w_t_fox=joined([w_fq, w_fv]),
        w_small=jnp.concatenate([w_ik, w_iw, w_fl, pad], axis=2).astype(BF16),
        w_out=w_out.astype(BF16),
        w_mem=w_mem_kv.astype(BF16),
        bf_rows=jnp.zeros((depth, 1, 128), F32).at[:, 0, OFF_FL:OFF_FL + H_FOX].set(b_forget.astype(F32)),
        cos128=cos128, sin128=sin128, t_tables=t_tables, cos_ik=cos_ik, sin_ik=sin_ik,
        tri=jnp.asarray(np.tril(np.ones((ts, ts), np.float32)), BF16),
    )


def _mixers(xb, mem_b, l, p, batch, seq, n_mem):
    scale = HEAD_DIM ** -0.5
    topk = min(INDEX_TOPK, seq // 4)
    w_scale = (IDX_HEADS ** -0.5) * (IDX_DIM ** -0.5)
    kv = _proj_plain(mem_b, p["w_mem"][l], batch * n_mem, 2 * D_MEMG, "proj_mem_kv")
    kv = kv.reshape(batch, n_mem, 2 * D_MEMG)
    h_plain = _proj_plain(xb, p["w_plain"][l], 2 * SEQ_TILE, p["w_plain"].shape[2], "proj_plain")
    dk = _proj_rope(xb, p["w_dk"][l], p["cos128"], p["sin128"], seq, min(4 * SEQ_TILE, seq), "proj_dsa_key")
    dqt, dvt, iqt = _proj_t(xb, p["w_t_dsa"][l], p["t_tables"], batch, seq, scale * LOG2E,
                            (("rope_head_scaled", D_DSA), ("values", D_DSA), ("rope_index", IDX_HEADS * IDX_DIM)),
                            "proj_transposed_dsa")
    fqt, fvt = _proj_t(xb, p["w_t_fox"][l], p["t_tables"], batch, seq, scale * LOG2E,
                       (("scaled", D_FOX), ("values", D_FOX)), "proj_transposed_fox")
    ik, iwt, cb = _prep(xb, p["w_small"][l], p["bf_rows"][l], p["cos_ik"], p["sin_ik"], p["tri"], batch, seq, w_scale)
    y_fox = _fox_attention(h_plain, cb, fqt, fvt, batch, seq)
    y_dsa = _dsa_attention(dk, dvt, ik, dqt, iqt, iwt, h_plain, batch, seq, topk)
    return y_fox, y_dsa, h_plain, kv


def kernel(x, mem, w_in, b_forget, w_mem_kv, w_out, ln_gain, ln_bias):
    batch, seq, d_model = x.shape
    depth = w_in.shape[0]
    n_mem = mem.shape[1]
    m = batch * seq
    alpha = (2 * depth) ** 0.25
    assert seq % (2 * SEQ_TILE) == 0 and min(INDEX_TOPK, seq // 4) <= SEQ_TILE

    p = _prepare(seq, d_model, w_in, b_forget, w_mem_kv, w_out)
    mem_b = mem.reshape(batch * n_mem, d_model).astype(BF16)
    xf = x.reshape(m, d_model).astype(F32)
    xb = xf.astype(BF16)
    for l in range(depth):
        y_fox, y_dsa, h_plain, kv = _mixers(xb, mem_b, l, p, batch, seq, n_mem)
        xf, xb = _out_ln(y_fox, y_dsa, h_plain, p["q_col"], kv, p["w_out"][l], xf,
                         ln_gain[l][None, :].astype(F32), ln_bias[l][None, :].astype(F32),
                         alpha, HEAD_DIM ** -0.5, seq)
    return xf.reshape(batch, seq, d_model).astype(x.dtype)
```
